```python
import jax, jax.numpy as jnp
from jax import lax
import numpy as np

D_MODEL = 1024
BATCH = 8
SEQ = 2048
DEPTH = 1

ATTN_HEADS = 8
HEAD_DIM = 64
ATTN_WIDTH = ATTN_HEADS * HEAD_DIM
MOBA_BLOCK = 256
MOBA_TOPK = 3
ROPE_THETA = 500000.0
ROPE_DIM = HEAD_DIM // 4
QUERY_CHUNK = 16
POOL_WINDOWS = (2, 4, 8, 16)
POOL_GROUPS = len(POOL_WINDOWS)
POOL_GROUP_WIDTH = 128
POOL_WIDTH = POOL_GROUPS * POOL_GROUP_WIDTH
D_FF = 4 * D_MODEL
IN_WIDTH = 3 * ATTN_WIDTH + POOL_WIDTH + 2 * D_MODEL
RMS_EPS = 1e-6

kernel_name = "hybrid_moba_multiscale_pool_gated_block"


def rms_norm(x, g):
    xf = x.astype(jnp.float32)
    y = xf * lax.rsqrt(jnp.mean(xf * xf, axis=-1, keepdims=True) + RMS_EPS)
    return (y * g.astype(jnp.float32)).astype(x.dtype)


def partial_rope(x, pos):
    half = ROPE_DIM // 2
    inv_freq = ROPE_THETA ** (-jnp.arange(half, dtype=jnp.float32) / half)
    ang = pos.astype(jnp.float32)[:, None] * inv_freq[None, :]
    cos = jnp.cos(ang)[None, :, None, :]
    sin = jnp.sin(ang)[None, :, None, :]
    xr = x[..., :ROPE_DIM].astype(jnp.float32)
    x1, x2 = xr[..., :half], xr[..., half:]
    rot = jnp.concatenate([x1 * cos - x2 * sin, x2 * cos + x1 * sin], axis=-1)
    return jnp.concatenate([rot.astype(x.dtype), x[..., ROPE_DIM:]], axis=-1)


def moba_attention(q, k, v):
    B, H, S, Dh = q.shape
    nb = -(-S // MOBA_BLOCK)
    pad = nb * MOBA_BLOCK - S
    kb = jnp.pad(k, ((0, 0), (0, 0), (0, pad), (0, 0))).reshape(B, H, nb, MOBA_BLOCK, Dh)
    vb = jnp.pad(v, ((0, 0), (0, 0), (0, pad), (0, 0))).reshape(B, H, nb, MOBA_BLOCK, Dh)
    counts = jnp.clip(S - jnp.arange(nb) * MOBA_BLOCK, 1, MOBA_BLOCK).astype(jnp.float32)
    k_mean = jnp.sum(kb.astype(jnp.float32), axis=3) / counts[:, None]
    gate = jnp.einsum('bhsd,bhnd->bhsn', q.astype(jnp.float32), k_mean)
    q_blk = jnp.arange(S) // MOBA_BLOCK
    fully_past = jnp.arange(nb)[None, :] < q_blk[:, None]
    gate = jnp.where(fully_past, gate, -jnp.inf)
    n_sel = min(MOBA_TOPK, nb)
    _, top_idx = lax.top_k(gate, n_sel)
    top_ok = top_idx < q_blk[:, None]

    scale = Dh ** -0.5
    n_chunks = S // QUERY_CHUNK
    bi = jnp.arange(B)[:, None, None, None]
    hi = jnp.arange(H)[None, :, None, None]

    def chunk(c):
        s0 = c * QUERY_CHUNK
        qc = lax.dynamic_slice_in_dim(q, s0, QUERY_CHUNK, axis=2)
        idx = lax.dynamic_slice_in_dim(top_idx, s0, QUERY_CHUNK, axis=2)
        ok = lax.dynamic_slice_in_dim(top_ok, s0, QUERY_CHUNK, axis=2)
        blk = s0 // MOBA_BLOCK
        k_own = lax.dynamic_index_in_dim(kb, blk, axis=2, keepdims=False)
        v_own = lax.dynamic_index_in_dim(vb, blk, axis=2, keepdims=False)
        k_sel = kb[bi, hi, idx]
        v_sel = vb[bi, hi, idx]
        s_own = jnp.einsum('bhqd,bhkd->bhqk', qc, k_own,
                           preferred_element_type=jnp.float32) * scale
        qpos = s0 + jnp.arange(QUERY_CHUNK)
        kpos = blk * MOBA_BLOCK + jnp.arange(MOBA_BLOCK)
        s_own = jnp.where(kpos[None, :] <= qpos[:, None], s_own, -jnp.inf)
        s_sel = jnp.einsum('bhqd,bhqnkd->bhqnk', qc, k_sel,
                           preferred_element_type=jnp.float32) * scale
        s_sel = jnp.where(ok[..., None], s_sel, -jnp.inf)
        s_sel = s_sel.reshape(B, H, QUERY_CHUNK, n_sel * MOBA_BLOCK)
        p = jax.nn.softmax(jnp.concatenate([s_own, s_sel], axis=-1), axis=-1).astype(v.dtype)
        p_own = p[..., :MOBA_BLOCK]
        p_sel = p[..., MOBA_BLOCK:].reshape(B, H, QUERY_CHUNK, n_sel, MOBA_BLOCK)
        o = (jnp.einsum('bhqk,bhkd->bhqd', p_own, v_own, preferred_element_type=jnp.float32)
             + jnp.einsum('bhqnk,bhqnkd->bhqd', p_sel, v_sel, preferred_element_type=jnp.float32))
        return o.astype(q.dtype)

    out = lax.map(chunk, jnp.arange(n_chunks))
    return out.transpose(1, 2, 0, 3, 4).reshape(B, H, S, Dh)


def multiscale_pool(u, w_grp, chan_scale):
    B, S, _ = u.shape
    uf = u.astype(jnp.float32).reshape(B, S, POOL_GROUPS, POOL_GROUP_WIDTH)
    csum = jnp.cumsum(uf, axis=1)
    t = jnp.arange(S)
    pooled = []
    for g, w in enumerate(POOL_WINDOWS):
        c = csum[:, :, g]
        c_lag = jnp.pad(c[:, :S - w], ((0, 0), (w, 0), (0, 0)))
        cnt = jnp.minimum(t + 1, w).astype(jnp.float32)
        pooled.append((c - c_lag) / cnt[None, :, None])
    pooled = jnp.stack(pooled, axis=2) - uf
    y = jnp.einsum('bsgc,gcd->bsgd', pooled, w_grp.astype(jnp.float32))
    y = y.reshape(B, S, POOL_WIDTH) * chan_scale.astype(jnp.float32)
    return y.astype(u.dtype)


def setup_inputs(seed: int = 0) -> dict:
    key = jax.random.key(seed)
    ks = jax.random.split(key, 14)
    f32 = jnp.float32

    def nrm(k, shape, fan_in):
        return jax.random.normal(k, shape, f32) * (fan_in ** -0.5)

    x = jax.random.normal(ks[0], (BATCH, SEQ, D_MODEL), f32)
    norm_mix = 1.0 + 0.02 * jax.random.normal(ks[1], (DEPTH, D_MODEL), f32)
    w_in = nrm(ks[2], (DEPTH, D_MODEL, IN_WIDTH), D_MODEL)
    q_norm = 1.0 + 0.02 * jax.random.normal(ks[3], (DEPTH, HEAD_DIM), f32)
    k_norm = 1.0 + 0.02 * jax.random.normal(ks[4], (DEPTH, HEAD_DIM), f32)
    w_pool_grp = nrm(ks[5], (DEPTH, POOL_GROUPS, POOL_GROUP_WIDTH, POOL_GROUP_WIDTH), POOL_GROUP_WIDTH)
    pool_scale = 1.0 + 0.02 * jax.random.normal(ks[6], (DEPTH, POOL_WIDTH), f32)
    w_up_attn = nrm(ks[7], (DEPTH, ATTN_WIDTH, D_MODEL), ATTN_WIDTH)
    w_up_pool = nrm(ks[8], (DEPTH, POOL_WIDTH, D_MODEL), POOL_WIDTH)
    w_out = nrm(ks[9], (DEPTH, D_MODEL, D_MODEL), D_MODEL)
    norm_mlp = 1.0 + 0.02 * jax.random.normal(ks[10], (DEPTH, D_MODEL), f32)
    w_ff1 = nrm(ks[11], (DEPTH, D_MODEL, D_FF), D_MODEL)
    w_ff2 = nrm(ks[12], (DEPTH, D_FF, D_MODEL), D_FF)
    return {"x": x, "norm_mix": norm_mix, "w_in": w_in, "q_norm": q_norm, "k_norm": k_norm,
            "w_pool_grp": w_pool_grp, "pool_scale": pool_scale, "w_up_attn": w_up_attn,
            "w_up_pool": w_up_pool, "w_out": w_out, "norm_mlp": norm_mlp,
            "w_ff1": w_ff1, "w_ff2": w_ff2}


def reference(x, norm_mix, w_in, q_norm, k_norm, w_pool_grp, pool_scale, w_up_attn,
              w_up_pool, w_out, norm_mlp, w_ff1, w_ff2):
    B, S, _ = x.shape
    pos = jnp.arange(S)
    splits = [ATTN_WIDTH, 2 * ATTN_WIDTH, 3 * ATTN_WIDTH,
              3 * ATTN_WIDTH + POOL_WIDTH, 3 * ATTN_WIDTH + POOL_WIDTH + D_MODEL]
    for l in range(DEPTH):
        h = rms_norm(x, norm_mix[l])
        proj = h @ w_in[l]
        q, k, v, u, g_a, g_p = jnp.split(proj, splits, axis=-1)
        q = partial_rope(rms_norm(q.reshape(B, S, ATTN_HEADS, HEAD_DIM), q_norm[l]), pos)
        k = partial_rope(rms_norm(k.reshape(B, S, ATTN_HEADS, HEAD_DIM), k_norm[l]), pos)
        v = v.reshape(B, S, ATTN_HEADS, HEAD_DIM)
        o_attn = moba_attention(q.transpose(0, 2, 1, 3), k.transpose(0, 2, 1, 3),
                                v.transpose(0, 2, 1, 3))
        o_attn = o_attn.transpose(0, 2, 1, 3).reshape(B, S, ATTN_WIDTH)
        o_pool = multiscale_pool(u, w_pool_grp[l], pool_scale[l])
        gate_a = jax.nn.sigmoid(g_a.astype(jnp.float32))
        gate_p = jax.nn.sigmoid(g_p.astype(jnp.float32))
        merged = (gate_a * (o_attn @ w_up_attn[l]).astype(jnp.float32)
                  + gate_p * (o_pool @ w_up_pool[l]).astype(jnp.float32)).astype(x.dtype)
        x = x + merged @ w_out[l]
        h2 = rms_norm(x, norm_mlp[l])
        x = x + jnp.square(jax.nn.relu(h2 @ w_ff1[l])) @ w_ff2[l]
    return x
```

```python
import functools

import jax
import jax.numpy as jnp
from jax import lax
from jax.experimental import pallas as pl
from jax.experimental.pallas import tpu as pltpu

D_MODEL = 1024
HEADS = 8
HEAD_DIM = 64
ATTN_WIDTH = HEADS * HEAD_DIM
MOBA_BLOCK = 256
MOBA_TOPK = 3
ROPE_THETA = 500000.0
ROPE_HALF = HEAD_DIM // 8
POOL_WINDOWS = (2, 4, 8, 16)
POOL_GROUP_WIDTH = 128
POOL_WIDTH = len(POOL_WINDOWS) * POOL_GROUP_WIDTH
D_FF = 4 * D_MODEL
RMS_EPS = 1e-6

LANES = 128
TOKENS = MOBA_BLOCK
MASKED = -1e30
FF_CHUNK = 1024
VMEM_LIMIT_BYTES = 58 * 1024 * 1024

_Q0, _K0, _V0, _U0 = 0, ATTN_WIDTH, 2 * ATTN_WIDTH, 3 * ATTN_WIDTH
_GA0 = 3 * ATTN_WIDTH + POOL_WIDTH
_GP0 = _GA0 + D_MODEL
IN_WIDTH = _GP0 + D_MODEL


def _bdot(a, b):
    return jnp.dot(a, b, preferred_element_type=jnp.float32)


def _sigmoid(x):
    return 1.0 / (1.0 + jnp.exp(-x))


def _rms_scale(t):
    return lax.rsqrt(jnp.mean(t * t, axis=0, keepdims=True) + RMS_EPS)


def _head_norm_rope(t, gain, cos, sin):
    y = t * _rms_scale(t) * gain
    x1 = y[0:ROPE_HALF, :]
    x2 = y[ROPE_HALF:2 * ROPE_HALF, :]
    return jnp.concatenate(
        [x1 * cos - x2 * sin, x2 * cos + x1 * sin, y[2 * ROPE_HALF:, :]], axis=0)


def _shift_tokens(tiles, shift):
    lane = lax.broadcasted_iota(jnp.int32, tiles[0].shape, 1)
    rolled = [pltpu.roll(t, shift, axis=1) for t in tiles]
    out = [rolled[0]]
    for c in range(1, len(tiles)):
        out.append(jnp.where(lane < shift, rolled[c - 1], rolled[c]))
    return out


def _layer_kernel(x_ref, cos_ref, sin_ref, gmix_ref, gmlp_ref, qn_ref, kn_ref, ps_ref,
                  win_ref, wgrp_ref, wupa_ref, wupp_ref, wout_ref, wff1_ref, wff2_ref,
                  o_ref,
                  kbuf, vbuf, kmbuf, uprev, qbuf, bias_ref, m_ref, l_ref, acc_ref):
    f32, bf16 = jnp.float32, jnp.bfloat16
    i = pl.program_id(1)

    @pl.when(i == 0)
    def _start_of_sequence():
        kmbuf[...] = jnp.zeros_like(kmbuf)
        uprev[...] = jnp.zeros_like(uprev)

    xT = x_ref[...].T
    hT = (xT * _rms_scale(xT) * gmix_ref[...]).astype(bf16)

    def proj(lo, hi):
        return _bdot(win_ref[lo:hi, :], hT)

    cos = cos_ref[...]
    sin = sin_ref[...]

    kT = proj(_K0, _K0 + ATTN_WIDTH)
    for h in range(HEADS):
        k_h = _head_norm_rope(kT[h * HEAD_DIM:(h + 1) * HEAD_DIM, :], kn_ref[...], cos, sin)
        k_tok = k_h.T
        kbuf[i, h] = k_tok.astype(bf16)
        kmean = jnp.sum(k_tok, axis=0, keepdims=True) * (1.0 / MOBA_BLOCK)
        kmbuf[h, pl.ds(i, 1), :] = kmean
    vbuf[i] = proj(_V0, _V0 + ATTN_WIDTH).astype(bf16)

    qT = proj(_Q0, _Q0 + ATTN_WIDTH)
    i_vec = jnp.full((1, TOKENS), i, jnp.int32)
    n_cand = kmbuf.shape[1] // 2 - 1
    for h in range(HEADS):
        q_h = _head_norm_rope(qT[h * HEAD_DIM:(h + 1) * HEAD_DIM, :], qn_ref[...], cos, sin)
        q_h = (q_h * HEAD_DIM ** -0.5).astype(bf16)
        qbuf[h] = q_h
        gate = _bdot(kmbuf[h].astype(bf16), q_h)
        neg_inf = jnp.float32(-jnp.inf)
        g = [jnp.where(i_vec > j, gate[j:j + 1, :], neg_inf) for j in range(n_cand)]
        for j in range(n_cand):
            rank = jnp.zeros((1, TOKENS), jnp.int32)
            for jp in range(n_cand):
                if jp == j:
                    continue
                ahead = g[jp] >= g[j] if jp < j else g[jp] > g[j]
                rank = rank + ahead.astype(jnp.int32)
            chosen = jnp.logical_and(rank < MOBA_TOPK, i_vec > j)
            bias_ref[h, j:j + 1, :] = jnp.where(chosen, 0.0, MASKED).astype(f32)

    key_pos = lax.broadcasted_iota(jnp.int32, (MOBA_BLOCK, TOKENS), 0)
    qry_pos = lax.broadcasted_iota(jnp.int32, (MOBA_BLOCK, TOKENS), 1)
    causal = key_pos <= qry_pos
    for h in range(HEADS):
        rows = slice(h * HEAD_DIM, (h + 1) * HEAD_DIM)
        s = jnp.where(causal, _bdot(kbuf[i, h], qbuf[h]), MASKED)
        m = jnp.max(s, axis=0, keepdims=True)
        p = jnp.exp(s - m)
        m_ref[h:h + 1, :] = m
        l_ref[h:h + 1, :] = jnp.sum(p, axis=0, keepdims=True)
        acc_ref[rows, :] = _bdot(vbuf[i, rows, :], p.astype(bf16))

    def past_block(j, carry):
        for h in range(HEADS):
            rows = slice(h * HEAD_DIM, (h + 1) * HEAD_DIM)
            s = _bdot(kbuf[j, h], qbuf[h]) + bias_ref[h, pl.ds(j, 1), :]
            m_old = m_ref[h:h + 1, :]
            m_new = jnp.maximum(m_old, jnp.max(s, axis=0, keepdims=True))
            alpha = jnp.exp(m_old - m_new)
            p = jnp.exp(s - m_new)
            m_ref[h:h + 1, :] = m_new
            l_ref[h:h + 1, :] = alpha * l_ref[h:h + 1, :] + jnp.sum(p, axis=0, keepdims=True)
            acc_ref[rows, :] = alpha * acc_ref[rows, :] + _bdot(vbuf[j, rows, :], p.astype(bf16))
        return carry

    lax.fori_loop(0, i, past_block, 0)

    o_heads = []
    for h in range(HEADS):
        rows = slice(h * HEAD_DIM, (h + 1) * HEAD_DIM)
        o_heads.append(acc_ref[rows, :] * (1.0 / l_ref[h:h + 1, :]))
    oT = jnp.concatenate(o_heads, axis=0).astype(bf16)

    uT = proj(_U0, _U0 + POOL_WIDTH)
    tok = i * TOKENS + lax.broadcasted_iota(jnp.int32, (1, TOKENS), 1)
    pool_out = []
    for gi, w in enumerate(POOL_WINDOWS):
        rows = slice(gi * POOL_GROUP_WIDTH, (gi + 1) * POOL_GROUP_WIDTH)
        u_g = uT[rows, :]
        tiles = [uprev[rows, :]] + [u_g[:, c * LANES:(c + 1) * LANES]
                                    for c in range(TOKENS // LANES)]
        shift = 1
        while shift < w:
            shifted = _shift_tokens(tiles, shift)
            tiles = [a + b for a, b in zip(tiles, shifted)]
            shift *= 2
        win_sum = jnp.concatenate(tiles[1:], axis=1)
        inv_cnt = 1.0 / jnp.minimum(tok + 1, w).astype(f32)
        pooled = win_sum * inv_cnt - u_g
        y = _bdot(wgrp_ref[gi], pooled.astype(bf16)) * ps_ref[rows, :]
        pool_out.append(y.astype(bf16))
    uprev[...] = uT[:, TOKENS - LANES:]
    opT = jnp.concatenate(pool_out, axis=0)

    merged = _sigmoid(proj(_GA0, _GA0 + D_MODEL)) * _bdot(wupa_ref[...], oT)
    merged = merged + _sigmoid(proj(_GP0, _GP0 + D_MODEL)) * _bdot(wupp_ref[...], opT)
    x1T = xT + _bdot(wout_ref[...], merged.astype(bf16))

    h2T = (x1T * _rms_scale(x1T) * gmlp_ref[...]).astype(bf16)
    yT = x1T
    for c in range(D_FF // FF_CHUNK):
        rows = slice(c * FF_CHUNK, (c + 1) * FF_CHUNK)
        f = jnp.maximum(_bdot(wff1_ref[rows, :], h2T), 0.0)
        yT = yT + _bdot(wff2_ref[:, rows], (f * f).astype(bf16))
    o_ref[...] = yT.T


def _bcast_cols(v, n):
    return jnp.broadcast_to(v.astype(jnp.float32)[:, None], (v.shape[0], n))


@jax.jit
def kernel(x, norm_mix, w_in, q_norm, k_norm, w_pool_grp, pool_scale, w_up_attn, w_up_pool,
           w_out, norm_mlp, w_ff1, w_ff2):
    batch, seq, d_model = x.shape
    assert d_model == D_MODEL and seq % TOKENS == 0 and w_in.shape[0] == 1
    n_blocks = seq // TOKENS
    bf16 = jnp.bfloat16

    def wT(w):
        return w.T.astype(bf16)

    half = jnp.arange(ROPE_HALF, dtype=jnp.float32)
    inv_freq = ROPE_THETA ** (-half / ROPE_HALF)
    ang = inv_freq[:, None] * jnp.arange(seq).astype(jnp.float32)[None, :]

    whole = pl.BlockSpec(memory_space=pltpu.VMEM)
    x_spec = pl.BlockSpec((None, TOKENS, D_MODEL), lambda b, i: (b, i, 0))
    rope_spec = pl.BlockSpec((ROPE_HALF, TOKENS), lambda b, i: (0, i))

    return pl.pallas_call(
        _layer_kernel,
        grid=(batch, n_blocks),
        in_specs=[x_spec, rope_spec, rope_spec] + [whole] * 12,
        out_specs=x_spec,
        out_shape=jax.ShapeDtypeStruct(x.shape, x.dtype),
        scratch_shapes=[
            pltpu.VMEM((n_blocks, HEADS, MOBA_BLOCK, HEAD_DIM), bf16),
            pltpu.VMEM((n_blocks, ATTN_WIDTH, MOBA_BLOCK), bf16),
            pltpu.VMEM((HEADS, 2 * n_blocks, HEAD_DIM), jnp.float32),
            pltpu.VMEM((POOL_WIDTH, LANES), jnp.float32),
            pltpu.VMEM((HEADS, HEAD_DIM, TOKENS), bf16),
            pltpu.VMEM((HEADS, n_blocks, TOKENS), jnp.float32),
            pltpu.VMEM((HEADS, TOKENS), jnp.float32),
            pltpu.VMEM((HEADS, TOKENS), jnp.float32),
            pltpu.VMEM((ATTN_WIDTH, TOKENS), jnp.float32),
        ],
        compiler_params=pltpu.CompilerParams(
            dimension_semantics=("arbitrary", "arbitrary"),
            vmem_limit_bytes=VMEM_LIMIT_BYTES),
        name="moba_pool_layer",
    )(x, jnp.cos(ang), jnp.sin(ang),
      _bcast_cols(norm_mix[0], TOKENS), _bcast_cols(norm_mlp[0], TOKENS),
      _bcast_cols(q_norm[0], TOKENS), _bcast_cols(k_norm[0], TOKENS),
      _bcast_cols(pool_scale[0], TOKENS),
      wT(w_in[0]), jnp.swapaxes(w_pool_grp[0], 1, 2).astype(bf16),
      wT(w_up_attn[0]), wT(w_up_pool[0]), wT(w_out[0]), wT(w_ff1[0]), wT(w_ff2[0]))
```

```python
import functools

import jax
import jax.numpy as jnp
from jax import lax
from jax.experimental import pallas as pl
from jax.experimental.pallas import tpu as pltpu

D_MODEL = 1024
HEADS = 8
HEAD_DIM = 64
ATTN_WIDTH = HEADS * HEAD_DIM
MOBA_BLOCK = 256
MOBA_TOPK = 3
ROPE_THETA = 500000.0
ROPE_HALF = HEAD_DIM // 8
POOL_WINDOWS = (2, 4, 8, 16)
POOL_GROUP_WIDTH = 128
POOL_WIDTH = len(POOL_WINDOWS) * POOL_GROUP_WIDTH
D_FF = 4 * D_MODEL
RMS_EPS = 1e-6

LANES = 128
TOKENS = MOBA_BLOCK
MASKED = -1e30
KQ_DIM = LANES
BIAS_ROWS = 8
V_ROWS = HEAD_DIM + 16
Q_SCALE = HEAD_DIM ** -0.5 * 1.4426950408889634
FF_CHUNK = 1024
VMEM_LIMIT_BYTES = 58 * 1024 * 1024

_Q0, _K0, _V0, _U0 = 0, ATTN_WIDTH, 2 * ATTN_WIDTH, 3 * ATTN_WIDTH
_GA0 = 3 * ATTN_WIDTH + POOL_WIDTH
_GP0 = _GA0 + D_MODEL
IN_WIDTH = _GP0 + D_MODEL


def _bdot(a, b):
    return jnp.dot(a, b, preferred_element_type=jnp.float32)


def _sigmoid(x):
    return 1.0 / (1.0 + jnp.exp(-x))


def _rms_scale(t):
    return lax.rsqrt(jnp.mean(t * t, axis=0, keepdims=True) + RMS_EPS)


def _head_norm_rope(t, gain, cos, sin):
    y = t * _rms_scale(t) * gain
    x1 = y[0:ROPE_HALF, :]
    x2 = y[ROPE_HALF:2 * ROPE_HALF, :]
    return jnp.concatenate(
        [x1 * cos - x2 * sin, x2 * cos + x1 * sin, y[2 * ROPE_HALF:, :]], axis=0)


def _shift_tokens(tiles, shift):
    lane = lax.broadcasted_iota(jnp.int32, tiles[0].shape, 1)
    rolled = [pltpu.roll(t, shift, axis=1) for t in tiles]
    out = [rolled[0]]
    for c in range(1, len(tiles)):
        out.append(jnp.where(lane < shift, rolled[c - 1], rolled[c]))
    return out


def _layer_kernel(x_ref, cos_ref, sin_ref, gmix_ref, gmlp_ref, qn_ref, kn_ref, ps_ref,
                  win_ref, wgrp_ref, wupa_ref, wupp_ref, wout_ref, wff1_ref, wff2_ref,
                  o_ref,
                  kbuf, vbuf, kmbuf, uprev, qbuf, m_ref, alpha_ref, acc_ref, s_ref, p_ref):
    f32, bf16 = jnp.float32, jnp.bfloat16
    i = pl.program_id(1)

    @pl.when(i == 0)
    def _start_of_sequence():
        kmbuf[...] = jnp.zeros_like(kmbuf)
        uprev[...] = jnp.zeros_like(uprev)

    xT = x_ref[...].T
    hT = (xT * _rms_scale(xT) * gmix_ref[...]).astype(bf16)

    def proj(lo, hi):
        return _bdot(win_ref[lo:hi, :], hT)

    cos = cos_ref[...]
    sin = sin_ref[...]

    n_blocks = kbuf.shape[0]
    kT = proj(_K0, _K0 + ATTN_WIDTH)
    vT = proj(_V0, _V0 + ATTN_WIDTH)
    pad_row = lax.broadcasted_iota(jnp.int32, (KQ_DIM - HEAD_DIM, TOKENS), 0)
    own_block = jnp.where(pad_row == i, 1.0, 0.0)
    ones_row = jnp.where(
        lax.broadcasted_iota(jnp.int32, (V_ROWS - HEAD_DIM, TOKENS), 0) == 0, 1.0, 0.0)
    for h in range(HEADS):
        rows = slice(h * HEAD_DIM, (h + 1) * HEAD_DIM)
        k_h = _head_norm_rope(kT[rows, :], kn_ref[...], cos, sin)
        k_tok = jnp.concatenate([k_h, own_block], axis=0).T
        kbuf[i, h] = k_tok.astype(bf16)
        kmean = jnp.sum(k_tok, axis=0, keepdims=True) * (1.0 / MOBA_BLOCK)
        kmbuf[h, pl.ds(i, 1), :] = kmean
        vbuf[i, h] = jnp.concatenate([vT[rows, :], ones_row], axis=0).astype(bf16)

    qT = proj(_Q0, _Q0 + ATTN_WIDTH)
    blk = lax.broadcasted_iota(jnp.int32, (BIAS_ROWS, TOKENS), 0)
    fully_past = blk < i
    for h in range(HEADS):
        rows = slice(h * HEAD_DIM, (h + 1) * HEAD_DIM)
        q_h = _head_norm_rope(qT[rows, :], qn_ref[...], cos, sin)
        q_h = (q_h * Q_SCALE).astype(bf16)
        gate = _bdot(kmbuf[h, :, :HEAD_DIM].astype(bf16), q_h)[:BIAS_ROWS, :]
        g = jnp.where(fully_past, gate, -jnp.inf)
        rank = jnp.zeros((BIAS_ROWS, TOKENS), jnp.int32)
        for jp in range(n_blocks - 1):
            g_jp = g[jp:jp + 1, :]
            ahead = jnp.logical_or(g_jp > g, jnp.logical_and(g_jp == g, blk > jp))
            rank = rank + ahead.astype(jnp.int32)
        chosen = jnp.logical_and(rank < MOBA_TOPK, fully_past)
        bias = jnp.where(jnp.logical_or(chosen, blk == i), 0.0, MASKED)
        zeros = jnp.zeros((KQ_DIM - HEAD_DIM - BIAS_ROWS, TOKENS), f32)
        qbuf[h] = jnp.concatenate(
            [q_h, jnp.concatenate([bias, zeros], axis=0).astype(bf16)], axis=0)

    key_pos = lax.broadcasted_iota(jnp.int32, (MOBA_BLOCK, TOKENS), 0)
    qry_pos = lax.broadcasted_iota(jnp.int32, (MOBA_BLOCK, TOKENS), 1)
    causal = key_pos <= qry_pos

    def attend(j, own):
        for h in range(HEADS):
            s = _bdot(kbuf[j, h], qbuf[h])
            if own:
                s = jnp.where(causal, s, MASKED)
            s_ref[h] = s
            m_new = jnp.max(s, axis=0, keepdims=True)
            if not own:
                m_old = m_ref[h:h + 1, :]
                m_new = jnp.maximum(m_old, m_new)
                alpha_ref[h:h + 1, :] = jnp.exp2(m_old - m_new)
            m_ref[h:h + 1, :] = m_new
        for h in range(HEADS):
            p_ref[h] = jnp.exp2(s_ref[h] - m_ref[h:h + 1, :]).astype(bf16)
        for h in range(HEADS):
            pv = _bdot(vbuf[j, h], p_ref[h])
            acc_ref[h] = pv if own else acc_ref[h] * alpha_ref[h:h + 1, :] + pv

    attend(i, True)

    def past_block(j, carry):
        attend(j, False)
        return carry

    lax.fori_loop(0, i, past_block, 0)

    o_heads = [acc_ref[h, :HEAD_DIM, :] * (1.0 / acc_ref[h, HEAD_DIM:HEAD_DIM + 1, :])
               for h in range(HEADS)]
    oT = jnp.concatenate(o_heads, axis=0).astype(bf16)

    uT = proj(_U0, _U0 + POOL_WIDTH)
    tok = i * TOKENS + lax.broadcasted_iota(jnp.int32, (1, TOKENS), 1)
    pool_out = []
    for gi, w in enumerate(POOL_WINDOWS):
        rows = slice(gi * POOL_GROUP_WIDTH, (gi + 1) * POOL_GROUP_WIDTH)
        u_g = uT[rows, :]
        tiles = [uprev[rows, :]] + [u_g[:, c * LANES:(c + 1) * LANES]
                                    for c in range(TOKENS // LANES)]
        shift = 1
        while shift < w:
            shifted = _shift_tokens(tiles, shift)
            tiles = [a + b for a, b in zip(tiles, shifted)]
            shift *= 2
        win_sum = jnp.concatenate(tiles[1:], axis=1)
        inv_cnt = 1.0 / jnp.minimum(tok + 1, w).astype(f32)
        pooled = win_sum * inv_cnt - u_g
        y = _bdot(wgrp_ref[gi], pooled.astype(bf16)) * ps_ref[rows, :]
        pool_out.append(y.astype(bf16))
    uprev[...] = uT[:, TOKENS - LANES:]
    opT = jnp.concatenate(pool_out, axis=0)

    merged = _sigmoid(proj(_GA0, _GA0 + D_MODEL)) * _bdot(wupa_ref[...], oT)
    merged = merged + _sigmoid(proj(_GP0, _GP0 + D_MODEL)) * _bdot(wupp_ref[...], opT)
    x1T = xT + _bdot(wout_ref[...], merged.astype(bf16))

    h2T = (x1T * _rms_scale(x1T) * gmlp_ref[...]).astype(bf16)
    yT = x1T
    for c in range(D_FF // FF_CHUNK):
        rows = slice(c * FF_CHUNK, (c + 1) * FF_CHUNK)
        f = jnp.maximum(_bdot(wff1_ref[rows, :], h2T), 0.0)
        yT = yT + _bdot(wff2_ref[:, rows], (f * f).astype(bf16))
    o_ref[...] = yT.T


def _bcast_cols(v, n):
    return jnp.broadcast_to(v.astype(jnp.float32)[:, None], (v.shape[0], n))


@jax.jit
def kernel(x, norm_mix, w_in, q_norm, k_norm, w_pool_grp, pool_scale, w_up_attn, w_up_pool,
           w_out, norm_mlp, w_ff1, w_ff2):
    batch, seq, d_model = x.shape
    assert d_model == D_MODEL and seq % TOKENS == 0 and w_in.shape[0] == 1
    n_blocks = seq // TOKENS
    assert n_blocks <= BIAS_ROWS
    bf16 = jnp.bfloat16

    def wT(w):
        return w.T.astype(bf16)

    half = jnp.arange(ROPE_HALF, dtype=jnp.float32)
    inv_freq = ROPE_THETA ** (-half / ROPE_HALF)
    ang = inv_freq[:, None] * jnp.arange(seq).astype(jnp.float32)[None, :]

    whole = pl.BlockSpec(memory_space=pltpu.VMEM)
    x_spec = pl.BlockSpec((None, TOKENS, D_MODEL), lambda b, i: (b, i, 0))
    rope_spec = pl.BlockSpec((ROPE_HALF, TOKENS), lambda b, i: (0, i))

    return pl.pallas_call(
        _layer_kernel,
        grid=(batch, n_blocks),
        in_specs=[x_spec, rope_spec, rope_spec] + [whole] * 12,
        out_specs=x_spec,
        out_shape=jax.ShapeDtypeStruct(x.shape, x.dtype),
        scratch_shapes=[
            pltpu.VMEM((n_blocks, HEADS, MOBA_BLOCK, KQ_DIM), bf16),
            pltpu.VMEM((n_blocks, HEADS, V_ROWS, MOBA_BLOCK), bf16),
            pltpu.VMEM((HEADS, 2 * BIAS_ROWS, KQ_DIM), jnp.float32),
            pltpu.VMEM((POOL_WIDTH, LANES), jnp.float32),
            pltpu.VMEM((HEADS, KQ_DIM, TOKENS), bf16),
            pltpu.VMEM((HEADS, TOKENS), jnp.float32),
            pltpu.VMEM((HEADS, TOKENS), jnp.float32),
            pltpu.VMEM((HEADS, V_ROWS, TOKENS), jnp.float32),
            pltpu.VMEM((HEADS, MOBA_BLOCK, TOKENS), jnp.float32),
            pltpu.VMEM((HEADS, MOBA_BLOCK, TOKENS), bf16),
        ],
        compiler_params=pltpu.CompilerParams(
            dimension_semantics=("arbitrary", "arbitrary"),
            vmem_limit_bytes=VMEM_LIMIT_BYTES),
        name="moba_pool_layer",
    )(x, jnp.cos(ang), jnp.sin(ang),
      _bcast_cols(norm_mix[0], TOKENS), _bcast_cols(norm_mlp[0], TOKENS),
      _bcast_cols(q_norm[0], TOKENS), _bcast_cols(k_norm[0], TOKENS),
      _bcast_cols(pool_scale[0], TOKENS),
      wT(w_in[0]), jnp.swapaxes(w_pool_grp[0], 1, 2).astype(bf16),
      wT(w_up_attn[0]), wT(w_up_pool[0]), wT(w_out[0]), wT(w_ff1[0]), wT(w_ff2[0]))
```

```python
import functools

import jax
import jax.numpy as jnp
from jax import lax
from jax.experimental import pallas as pl
from jax.experimental.pallas import tpu as pltpu

D_MODEL = 1024
HEADS = 8
HEAD_DIM = 64
ATTN_WIDTH = HEADS * HEAD_DIM
MOBA_BLOCK = 256
MOBA_TOPK = 3
ROPE_THETA = 500000.0
ROPE_HALF = HEAD_DIM // 8
POOL_WINDOWS = (2, 4, 8, 16)
POOL_GROUP_WIDTH = 128
POOL_WIDTH = len(POOL_WINDOWS) * POOL_GROUP_WIDTH
D_FF = 4 * D_MODEL
RMS_EPS = 1e-6

LANES = 128
TOKENS = MOBA_BLOCK
MASKED = -1e30
KQ_DIM = LANES
BIAS_ROWS = 8
V_ROWS = HEAD_DIM + 16
Q_SCALE = HEAD_DIM ** -0.5 * 1.4426950408889634
FF_CHUNK = 1024
VMEM_LIMIT_BYTES = 58 * 1024 * 1024

_Q0, _K0, _V0, _U0 = 0, ATTN_WIDTH, 2 * ATTN_WIDTH, 3 * ATTN_WIDTH
_GA0 = 3 * ATTN_WIDTH + POOL_WIDTH
_GP0 = _GA0 + D_MODEL
IN_WIDTH = _GP0 + D_MODEL


def _bdot(a, b):
    return jnp.dot(a, b, preferred_element_type=jnp.float32)


def _wdot(w_ref, rows, act, cols=slice(None)):
    mid = (rows.start + rows.stop) // 2
    return jnp.concatenate([_bdot(w_ref[rows.start:mid, cols], act),
                            _bdot(w_ref[mid:rows.stop, cols], act)], axis=0)


def _sigmoid(x):
    return 1.0 / (1.0 + jnp.exp(-x))


def _rms_scale(t):
    return lax.rsqrt(jnp.mean(t * t, axis=0, keepdims=True) + RMS_EPS)


def _head_norm_rope(t, gain, cos, sin):
    y = t * _rms_scale(t) * gain
    x1 = y[0:ROPE_HALF, :]
    x2 = y[ROPE_HALF:2 * ROPE_HALF, :]
    return jnp.concatenate(
        [x1 * cos - x2 * sin, x2 * cos + x1 * sin, y[2 * ROPE_HALF:, :]], axis=0)


def _shift_tokens(tiles, shift):
    lane = lax.broadcasted_iota(jnp.int32, tiles[0].shape, 1)
    rolled = [pltpu.roll(t, shift, axis=1) for t in tiles]
    out = [rolled[0]]
    for c in range(1, len(tiles)):
        out.append(jnp.where(lane < shift, rolled[c - 1], rolled[c]))
    return out


def _layer_kernel(x_ref, cos_ref, sin_ref, gmix_ref, gmlp_ref, qn_ref, kn_ref, ps_ref,
                  win_ref, wgrp_ref, wupa_ref, wupp_ref, wout_ref, wff1_ref, wff2_ref,
                  o_ref,
                  kbuf, vbuf, kmbuf, uprev, qbuf, m_ref, alpha_ref, acc_ref, s_ref, p_ref):
    f32, bf16 = jnp.float32, jnp.bfloat16
    i = pl.program_id(1)

    @pl.when(i == 0)
    def _start_of_sequence():
        kmbuf[...] = jnp.zeros_like(kmbuf)
        uprev[...] = jnp.zeros_like(uprev)

    xT = x_ref[...].T
    hT = (xT * _rms_scale(xT) * gmix_ref[...]).astype(bf16)

    def proj(lo, hi):
        return _wdot(win_ref, slice(lo, hi), hT)

    cos = cos_ref[...]
    sin = sin_ref[...]

    n_blocks = kbuf.shape[0]
    kT = proj(_K0, _K0 + ATTN_WIDTH)
    vT = proj(_V0, _V0 + ATTN_WIDTH)
    pad_row = lax.broadcasted_iota(jnp.int32, (KQ_DIM - HEAD_DIM, TOKENS), 0)
    own_block = jnp.where(pad_row == i, 1.0, 0.0)
    ones_row = jnp.where(
        lax.broadcasted_iota(jnp.int32, (V_ROWS - HEAD_DIM, TOKENS), 0) == 0, 1.0, 0.0)
    for h in range(HEADS):
        rows = slice(h * HEAD_DIM, (h + 1) * HEAD_DIM)
        k_h = _head_norm_rope(kT[rows, :], kn_ref[...], cos, sin)
        k_tok = jnp.concatenate([k_h, own_block], axis=0).T
        kbuf[i, h] = k_tok.astype(bf16)
        kmean = jnp.sum(k_tok, axis=0, keepdims=True) * (1.0 / MOBA_BLOCK)
        kmbuf[h, pl.ds(i, 1), :] = kmean
        vbuf[i, h] = jnp.concatenate([vT[rows, :], ones_row], axis=0).astype(bf16)

    qT = proj(_Q0, _Q0 + ATTN_WIDTH)
    blk = lax.broadcasted_iota(jnp.int32, (BIAS_ROWS, TOKENS), 0)
    fully_past = blk < i
    for h in range(HEADS):
        rows = slice(h * HEAD_DIM, (h + 1) * HEAD_DIM)
        q_h = _head_norm_rope(qT[rows, :], qn_ref[...], cos, sin)
        q_h = (q_h * Q_SCALE).astype(bf16)
        gate = _bdot(kmbuf[h, :, :HEAD_DIM].astype(bf16), q_h)[:BIAS_ROWS, :]
        g = jnp.where(fully_past, gate, -jnp.inf)
        rank = jnp.zeros((BIAS_ROWS, TOKENS), jnp.int32)
        for jp in range(n_blocks - 1):
            g_jp = g[jp:jp + 1, :]
            ahead = jnp.logical_or(g_jp > g, jnp.logical_and(g_jp == g, blk > jp))
            rank = rank + ahead.astype(jnp.int32)
        chosen = jnp.logical_and(rank < MOBA_TOPK, fully_past)
        bias = jnp.where(jnp.logical_or(chosen, blk == i), 0.0, MASKED)
        zeros = jnp.zeros((KQ_DIM - HEAD_DIM - BIAS_ROWS, TOKENS), f32)
        qbuf[h] = jnp.concatenate(
            [q_h, jnp.concatenate([bias, zeros], axis=0).astype(bf16)], axis=0)

    key_pos = lax.broadcasted_iota(jnp.int32, (MOBA_BLOCK, TOKENS), 0)
    qry_pos = lax.broadcasted_iota(jnp.int32, (MOBA_BLOCK, TOKENS), 1)
    causal = key_pos <= qry_pos

    def attend(j, own):
        for h in range(HEADS):
            s = _bdot(kbuf[j, h], qbuf[h])
            if own:
                s = jnp.where(causal, s, MASKED)
            s_ref[h] = s
            m_new = jnp.max(s, axis=0, keepdims=True)
            if not own:
                m_old = m_ref[h:h + 1, :]
                m_new = jnp.maximum(m_old, m_new)
                alpha_ref[h:h + 1, :] = jnp.exp2(m_old - m_new)
            m_ref[h:h + 1, :] = m_new
        for h in range(HEADS):
            p_ref[h] = jnp.exp2(s_ref[h] - m_ref[h:h + 1, :]).astype(bf16)
        for h in range(HEADS):
            pv = _bdot(vbuf[j, h], p_ref[h])
            acc_ref[h] = pv if own else acc_ref[h] * alpha_ref[h:h + 1, :] + pv

    attend(i, True)

    def past_block(j, carry):
        attend(j, False)
        return carry

    lax.fori_loop(0, i, past_block, 0)

    o_heads = [acc_ref[h, :HEAD_DIM, :] * (1.0 / acc_ref[h, HEAD_DIM:HEAD_DIM + 1, :])
               for h in range(HEADS)]
    oT = jnp.concatenate(o_heads, axis=0).astype(bf16)

    uT = proj(_U0, _U0 + POOL_WIDTH)
    tok = i * TOKENS + lax.broadcasted_iota(jnp.int32, (1, TOKENS), 1)
    pool_out = []
    for gi, w in enumerate(POOL_WINDOWS):
        rows = slice(gi * POOL_GROUP_WIDTH, (gi + 1) * POOL_GROUP_WIDTH)
        u_g = uT[rows, :]
        tiles = [uprev[rows, :]] + [u_g[:, c * LANES:(c + 1) * LANES]
                                    for c in range(TOKENS // LANES)]
        shift = 1
        while shift < w:
            shifted = _shift_tokens(tiles, shift)
            tiles = [a + b for a, b in zip(tiles, shifted)]
            shift *= 2
        win_sum = jnp.concatenate(tiles[1:], axis=1)
        inv_cnt = 1.0 / jnp.minimum(tok + 1, w).astype(f32)
        pooled = win_sum * inv_cnt - u_g
        y = _bdot(wgrp_ref[gi], pooled.astype(bf16)) * ps_ref[rows, :]
        pool_out.append(y.astype(bf16))
    uprev[...] = uT[:, TOKENS - LANES:]
    opT = jnp.concatenate(pool_out, axis=0)

    all_rows = slice(0, D_MODEL)
    merged = _sigmoid(proj(_GA0, _GA0 + D_MODEL)) * _wdot(wupa_ref, all_rows, oT)
    merged = merged + _sigmoid(proj(_GP0, _GP0 + D_MODEL)) * _wdot(wupp_ref, all_rows, opT)
    x1T = xT + _wdot(wout_ref, all_rows, merged.astype(bf16))

    h2T = (x1T * _rms_scale(x1T) * gmlp_ref[...]).astype(bf16)
    yT = x1T
    for c in range(D_FF // FF_CHUNK):
        rows = slice(c * FF_CHUNK, (c + 1) * FF_CHUNK)
        f = jnp.maximum(_wdot(wff1_ref, rows, h2T), 0.0)
        yT = yT + _wdot(wff2_ref, all_rows, (f * f).astype(bf16), rows)
    o_ref[...] = yT.T


def _bcast_cols(v, n):
    return jnp.broadcast_to(v.astype(jnp.float32)[:, None], (v.shape[0], n))


@jax.jit
def kernel(x, norm_mix, w_in, q_norm, k_norm, w_pool_grp, pool_scale, w_up_attn, w_up_pool,
           w_out, norm_mlp, w_ff1, w_ff2):
    batch, seq, d_model = x.shape
    assert d_model == D_MODEL and seq % TOKENS == 0 and w_in.shape[0] == 1
    n_blocks = seq // TOKENS
    assert n_blocks <= BIAS_ROWS
    bf16 = jnp.bfloat16

    def wT(w):
        return w.T.astype(bf16)

    half = jnp.arange(ROPE_HALF, dtype=jnp.float32)
    inv_freq = ROPE_THETA ** (-half / ROPE_HALF)
    ang = inv_freq[:, None] * jnp.arange(seq).astype(jnp.float32)[None, :]

    whole = pl.BlockSpec(memory_space=pltpu.VMEM)
    x_spec = pl.BlockSpec((None, TOKENS, D_MODEL), lambda b, i: (b, i, 0))
    rope_spec = pl.BlockSpec((ROPE_HALF, TOKENS), lambda b, i: (0, i))

    return pl.pallas_call(
        _layer_kernel,
        grid=(batch, n_blocks),
        in_specs=[x_spec, rope_spec, rope_spec] + [whole] * 12,
        out_specs=x_spec,
        out_shape=jax.ShapeDtypeStruct(x.shape, x.dtype),
        scratch_shapes=[
            pltpu.VMEM((n_blocks, HEADS, MOBA_BLOCK, KQ_DIM), bf16),
            pltpu.VMEM((n_blocks, HEADS, V_ROWS, MOBA_BLOCK), bf16),
            pltpu.VMEM((HEADS, 2 * BIAS_ROWS, KQ_DIM), jnp.float32),
            pltpu.VMEM((POOL_WIDTH, LANES), jnp.float32),
            pltpu.VMEM((HEADS, KQ_DIM, TOKENS), bf16),
            pltpu.VMEM((HEADS, TOKENS), jnp.float32),
            pltpu.VMEM((HEADS, TOKENS), jnp.float32),
            pltpu.VMEM((HEADS, V_ROWS, TOKENS), jnp.float32),
            pltpu.VMEM((HEADS, MOBA_BLOCK, TOKENS), jnp.float32),
            pltpu.VMEM((HEADS, MOBA_BLOCK, TOKENS), bf16),
        ],
        compiler_params=pltpu.CompilerParams(
            dimension_semantics=("arbitrary", "arbitrary"),
            vmem_limit_bytes=VMEM_LIMIT_BYTES),
        name="moba_pool_layer",
    )(x, jnp.cos(ang), jnp.sin(ang),
      _bcast_cols(norm_mix[0], TOKENS), _bcast_cols(norm_mlp[0], TOKENS),
      _bcast_cols(q_norm[0], TOKENS), _bcast_cols(k_norm[0], TOKENS),
      _bcast_cols(pool_scale[0], TOKENS),
      wT(w_in[0]), jnp.swapaxes(w_pool_grp[0], 1, 2).astype(bf16),
      wT(w_up_attn[0]), wT(w_up_pool[0]), wT(w_out[0]), wT(w_ff1[0]), wT(w_ff2[0]))
```

```python
import jax
import jax.numpy as jnp
from jax import lax
from jax.experimental import pallas as pl
from jax.experimental.pallas import tpu as pltpu

D_MODEL = 1024
HEADS = 8
HEAD_DIM = 64
ATTN_WIDTH = HEADS * HEAD_DIM
MOBA_BLOCK = 256
MOBA_TOPK = 3
ROPE_THETA = 500000.0
ROPE_HALF = HEAD_DIM // 8
POOL_WINDOWS = (2, 4, 8, 16)
POOL_GROUP_WIDTH = 128
POOL_WIDTH = len(POOL_WINDOWS) * POOL_GROUP_WIDTH
D_FF = 4 * D_MODEL
RMS_EPS = 1e-6

LANES = 128
TOKENS = MOBA_BLOCK
MASKED = -1e30
KQ_DIM = LANES
BIAS_ROWS = 8
V_ROWS = HEAD_DIM + 16
Q_SCALE = HEAD_DIM ** -0.5 * 1.4426950408889634
FF_CHUNK = 1024
OUT_CHUNK = 512
VMEM_LIMIT_BYTES = 58 * 1024 * 1024

_Q0, _K0, _V0, _U0 = 0, ATTN_WIDTH, 2 * ATTN_WIDTH, 3 * ATTN_WIDTH
_GA0 = 3 * ATTN_WIDTH + POOL_WIDTH
_GP0 = _GA0 + D_MODEL
IN_WIDTH = _GP0 + D_MODEL


def _bdot(a, b):
    return jnp.dot(a, b, preferred_element_type=jnp.float32)


def _wdot2(w_ref, rows, act, cols=slice(None)):
    mid = (rows.start + rows.stop) // 2
    return [_bdot(w_ref[rows.start:mid, cols], act), _bdot(w_ref[mid:rows.stop, cols], act)]


def _halves(rows):
    mid = (rows.start + rows.stop) // 2
    return [slice(rows.start, mid), slice(mid, rows.stop)]


def _sigmoid(x):
    return 1.0 / (1.0 + jnp.exp(-x))


def _rms_scale(t):
    return lax.rsqrt(jnp.mean(t * t, axis=0, keepdims=True) + RMS_EPS)


def _head_norm_rope(t, gain, cos, sin):
    y = t * _rms_scale(t) * gain
    x1 = y[0:ROPE_HALF, :]
    x2 = y[ROPE_HALF:2 * ROPE_HALF, :]
    return jnp.concatenate(
        [x1 * cos - x2 * sin, x2 * cos + x1 * sin, y[2 * ROPE_HALF:, :]], axis=0)


def _shift_tokens(tiles, shift):
    lane = lax.broadcasted_iota(jnp.int32, tiles[0].shape, 1)
    rolled = [pltpu.roll(t, shift, axis=1) for t in tiles]
    out = [rolled[0]]
    for c in range(1, len(tiles)):
        out.append(jnp.where(lane < shift, rolled[c - 1], rolled[c]))
    return out


def _layer_kernel(x_ref, cos_ref, sin_ref, gmix_ref, gmlp_ref, qn_ref, kn_ref, ps_ref,
                  win_ref, wgrp_ref, wupa_ref, wupp_ref, wout_ref, wff1_ref, wff2_ref,
                  o_ref,
                  kbuf, vbuf, kmbuf, uprev, qbuf, m_ref, alpha_ref, acc_ref, s_ref, p_ref,
                  o_sc, op_sc, sga_sc, sgp_sc, mg_sc, x1_sc, h2_sc, f_sc):
    f32, bf16 = jnp.float32, jnp.bfloat16
    i = pl.program_id(1)

    @pl.when(i == 0)
    def _start_of_sequence():
        kmbuf[...] = jnp.zeros_like(kmbuf)
        uprev[...] = jnp.zeros_like(uprev)

    xT = x_ref[...].T
    hT = (xT * _rms_scale(xT) * gmix_ref[...]).astype(bf16)

    def proj2(lo, width):
        return _wdot2(win_ref, slice(lo, lo + width), hT)

    def head_rows(parts, h):
        per_part = HEADS // 2
        hh = h % per_part
        return parts[h // per_part][hh * HEAD_DIM:(hh + 1) * HEAD_DIM, :]

    cos = cos_ref[...]
    sin = sin_ref[...]

    n_blocks = kbuf.shape[0]
    k_parts = proj2(_K0, ATTN_WIDTH)
    q_parts = proj2(_Q0, ATTN_WIDTH)
    v_parts = proj2(_V0, ATTN_WIDTH)
    u_parts = proj2(_U0, POOL_WIDTH)
    all_rows = slice(0, D_MODEL)
    for rows, ga, gp in zip(_halves(all_rows), proj2(_GA0, D_MODEL), proj2(_GP0, D_MODEL)):
        sga_sc[rows, :] = _sigmoid(ga)
        sgp_sc[rows, :] = _sigmoid(gp)
    pad_row = lax.broadcasted_iota(jnp.int32, (KQ_DIM - HEAD_DIM, TOKENS), 0)
    own_block = jnp.where(pad_row == i, 1.0, 0.0)
    ones_row = jnp.where(
        lax.broadcasted_iota(jnp.int32, (V_ROWS - HEAD_DIM, TOKENS), 0) == 0, 1.0, 0.0)
    for h in range(HEADS):
        k_h = _head_norm_rope(head_rows(k_parts, h), kn_ref[...], cos, sin)
        k_tok = jnp.concatenate([k_h, own_block], axis=0).T
        kbuf[i, h] = k_tok.astype(bf16)
        kmean = jnp.sum(k_tok, axis=0, keepdims=True) * (1.0 / MOBA_BLOCK)
        kmbuf[h, pl.ds(i, 1), :] = kmean
        vbuf[i, h, :HEAD_DIM, :] = head_rows(v_parts, h).astype(bf16)
        vbuf[i, h, HEAD_DIM:, :] = ones_row.astype(bf16)

    blk = lax.broadcasted_iota(jnp.int32, (BIAS_ROWS, TOKENS), 0)
    fully_past = blk < i
    for h in range(HEADS):
        q_h = _head_norm_rope(head_rows(q_parts, h), qn_ref[...], cos, sin)
        q_h = (q_h * Q_SCALE).astype(bf16)
        gate = _bdot(kmbuf[h, :, :HEAD_DIM].astype(bf16), q_h)[:BIAS_ROWS, :]
        g = jnp.where(fully_past, gate, -jnp.inf)
        rank = jnp.zeros((BIAS_ROWS, TOKENS), jnp.int32)
        for jp in range(n_blocks - 1):
            g_jp = g[jp:jp + 1, :]
            ahead = jnp.logical_or(g_jp > g, jnp.logical_and(g_jp == g, blk > jp))
            rank = rank + ahead.astype(jnp.int32)
        chosen = jnp.logical_and(rank < MOBA_TOPK, fully_past)
        bias = jnp.where(jnp.logical_or(chosen, blk == i), 0.0, MASKED)
        zeros = jnp.zeros((KQ_DIM - HEAD_DIM - BIAS_ROWS, TOKENS), f32)
        qbuf[h, :HEAD_DIM, :] = q_h
        qbuf[h, HEAD_DIM:, :] = jnp.concatenate([bias, zeros], axis=0).astype(bf16)

    tok = i * TOKENS + lax.broadcasted_iota(jnp.int32, (1, TOKENS), 1)
    groups_per_part = len(POOL_WINDOWS) // 2
    for gi, w in enumerate(POOL_WINDOWS):
        rows = slice(gi * POOL_GROUP_WIDTH, (gi + 1) * POOL_GROUP_WIDTH)
        gg = gi % groups_per_part
        u_g = u_parts[gi // groups_per_part][gg * POOL_GROUP_WIDTH:(gg + 1) * POOL_GROUP_WIDTH, :]
        tiles = [uprev[rows, :]] + [u_g[:, c * LANES:(c + 1) * LANES]
                                    for c in range(TOKENS // LANES)]
        uprev[rows, :] = tiles[-1]
        shift = 1
        while shift < w:
            shifted = _shift_tokens(tiles, shift)
            tiles = [a + b for a, b in zip(tiles, shifted)]
            shift *= 2
        win_sum = jnp.concatenate(tiles[1:], axis=1)
        inv_cnt = 1.0 / jnp.minimum(tok + 1, w).astype(f32)
        pooled = win_sum * inv_cnt - u_g
        y = _bdot(wgrp_ref[gi], pooled.astype(bf16)) * ps_ref[rows, :]
        op_sc[rows, :] = y.astype(bf16)

    key_pos = lax.broadcasted_iota(jnp.int32, (MOBA_BLOCK, TOKENS), 0)
    qry_pos = lax.broadcasted_iota(jnp.int32, (MOBA_BLOCK, TOKENS), 1)
    causal = key_pos <= qry_pos

    def attend(j, own):
        for h in range(HEADS):
            s = _bdot(kbuf[j, h], qbuf[h])
            if own:
                s = jnp.where(causal, s, MASKED)
            s_ref[h] = s
            m_new = jnp.max(s, axis=0, keepdims=True)
            if not own:
                m_old = m_ref[h:h + 1, :]
                m_new = jnp.maximum(m_old, m_new)
                alpha_ref[h:h + 1, :] = jnp.exp2(m_old - m_new)
            m_ref[h:h + 1, :] = m_new
        for h in range(HEADS):
            p_ref[h] = jnp.exp2(s_ref[h] - m_ref[h:h + 1, :]).astype(bf16)
        for h in range(HEADS):
            pv = _bdot(vbuf[j, h], p_ref[h])
            acc_ref[h] = pv if own else acc_ref[h] * alpha_ref[h:h + 1, :] + pv

    attend(i, True)

    def past_block(j, carry):
        attend(j, False)
        return carry

    lax.fori_loop(0, i, past_block, 0)

    for h in range(HEADS):
        o_h = acc_ref[h, :HEAD_DIM, :] * (1.0 / acc_ref[h, HEAD_DIM:HEAD_DIM + 1, :])
        o_sc[h * HEAD_DIM:(h + 1) * HEAD_DIM, :] = o_h.astype(bf16)

    upa_parts = _wdot2(wupa_ref, all_rows, o_sc[...])
    upp_parts = _wdot2(wupp_ref, all_rows, op_sc[...])
    for rows, upa, upp in zip(_halves(all_rows), upa_parts, upp_parts):
        mg_sc[rows, :] = (sga_sc[rows, :] * upa + sgp_sc[rows, :] * upp).astype(bf16)
    sumsq = jnp.zeros((1, TOKENS), f32)
    for rows, d in zip(_halves(all_rows), _wdot2(wout_ref, all_rows, mg_sc[...])):
        x1 = xT[rows, :] + d
        x1_sc[rows, :] = x1
        h2_sc[rows, :] = (x1 * gmlp_ref[rows, :]).astype(bf16)
        sumsq = sumsq + jnp.sum(x1 * x1, axis=0, keepdims=True)
    r = lax.rsqrt(sumsq * (1.0 / D_MODEL) + RMS_EPS)
    r2 = r * r

    h2T = h2_sc[...]
    for c in range(D_FF // FF_CHUNK):
        chunk = slice(c * FF_CHUNK, (c + 1) * FF_CHUNK)
        for rows, z in zip(_halves(chunk), _wdot2(wff1_ref, chunk, h2T)):
            f = jnp.maximum(z, 0.0)
            f_sc[rows, :] = (f * f).astype(bf16)
    fT = f_sc[...]
    for c in range(D_MODEL // OUT_CHUNK):
        chunk = slice(c * OUT_CHUNK, (c + 1) * OUT_CHUNK)
        for rows, y in zip(_halves(chunk), _wdot2(wff2_ref, chunk, fT)):
            o_ref[:, rows] = (x1_sc[rows, :] + y * r2).T


def _bcast_cols(v, n):
    return jnp.broadcast_to(v.astype(jnp.float32)[:, None], (v.shape[0], n))


@jax.jit
def kernel(x, norm_mix, w_in, q_norm, k_norm, w_pool_grp, pool_scale, w_up_attn, w_up_pool,
           w_out, norm_mlp, w_ff1, w_ff2):
    batch, seq, d_model = x.shape
    assert d_model == D_MODEL and seq % TOKENS == 0 and w_in.shape[0] == 1
    n_blocks = seq // TOKENS
    assert n_blocks <= BIAS_ROWS
    bf16 = jnp.bfloat16

    def wT(w):
        return w.T.astype(bf16)

    half = jnp.arange(ROPE_HALF, dtype=jnp.float32)
    inv_freq = ROPE_THETA ** (-half / ROPE_HALF)
    ang = inv_freq[:, None] * jnp.arange(seq).astype(jnp.float32)[None, :]

    whole = pl.BlockSpec(memory_space=pltpu.VMEM)
    x_spec = pl.BlockSpec((None, TOKENS, D_MODEL), lambda b, i: (b, i, 0))
    rope_spec = pl.BlockSpec((ROPE_HALF, TOKENS), lambda b, i: (0, i))

    return pl.pallas_call(
        _layer_kernel,
        grid=(batch, n_blocks),
        in_specs=[x_spec, rope_spec, rope_spec] + [whole] * 12,
        out_specs=x_spec,
        out_shape=jax.ShapeDtypeStruct(x.shape, x.dtype),
        scratch_shapes=[
            pltpu.VMEM((n_blocks, HEADS, MOBA_BLOCK, KQ_DIM), bf16),
            pltpu.VMEM((n_blocks, HEADS, V_ROWS, MOBA_BLOCK), bf16),
            pltpu.VMEM((HEADS, 2 * BIAS_ROWS, KQ_DIM), jnp.float32),
            pltpu.VMEM((POOL_WIDTH, LANES), jnp.float32),
            pltpu.VMEM((HEADS, KQ_DIM, TOKENS), bf16),
            pltpu.VMEM((HEADS, TOKENS), jnp.float32),
            pltpu.VMEM((HEADS, TOKENS), jnp.float32),
            pltpu.VMEM((HEADS, V_ROWS, TOKENS), jnp.float32),
            pltpu.VMEM((HEADS, MOBA_BLOCK, TOKENS), jnp.float32),
            pltpu.VMEM((HEADS, MOBA_BLOCK, TOKENS), bf16),
            pltpu.VMEM((ATTN_WIDTH, TOKENS), bf16),
            pltpu.VMEM((POOL_WIDTH, TOKENS), bf16),
            pltpu.VMEM((D_MODEL, TOKENS), jnp.float32),
            pltpu.VMEM((D_MODEL, TOKENS), jnp.float32),
            pltpu.VMEM((D_MODEL, TOKENS), bf16),
            pltpu.VMEM((D_MODEL, TOKENS), jnp.float32),
            pltpu.VMEM((D_MODEL, TOKENS), bf16),
            pltpu.VMEM((D_FF, TOKENS), bf16),
        ],
        compiler_params=pltpu.CompilerParams(
            dimension_semantics=("arbitrary", "arbitrary"),
            vmem_limit_bytes=VMEM_LIMIT_BYTES),
        name="moba_pool_layer",
    )(x, jnp.cos(ang), jnp.sin(ang),
      _bcast_cols(norm_mix[0], TOKENS), _bcast_cols(norm_mlp[0], TOKENS),
      _bcast_cols(q_norm[0], TOKENS), _bcast_cols(k_norm[0], TOKENS),
      _bcast_cols(pool_scale[0], TOKENS),
      wT(w_in[0]), jnp.swapaxes(w_pool_grp[0], 1, 2).astype(bf16),
      wT(w_up_attn[0]), wT(w_up_pool[0]), wT(w_out[0]), wT(w_ff1[0]), wT(w_ff2[0]))
```

```python
import jax
import jax.numpy as jnp
from jax import lax
from jax.experimental import pallas as pl
from jax.experimental.pallas import tpu as pltpu

D_MODEL = 1024
HEADS = 8
HEAD_DIM = 64
ATTN_WIDTH = HEADS * HEAD_DIM
MOBA_BLOCK = 256
MOBA_TOPK = 3
ROPE_THETA = 500000.0
ROPE_HALF = HEAD_DIM // 8
POOL_WINDOWS = (2, 4, 8, 16)
POOL_GROUP_WIDTH = 128
POOL_WIDTH = len(POOL_WINDOWS) * POOL_GROUP_WIDTH
D_FF = 4 * D_MODEL
RMS_EPS = 1e-6

LANES = 128
TOKENS = MOBA_BLOCK
MASKED = -1e30
KQ_DIM = LANES
BIAS_ROWS = 8
V_ROWS = HEAD_DIM + 16
Q_SCALE = HEAD_DIM ** -0.5 * 1.4426950408889634
FF_CHUNK = 1024
OUT_CHUNK = 512
VMEM_LIMIT_BYTES = 61 * 1024 * 1024

_Q0, _K0, _V0, _U0 = 0, ATTN_WIDTH, 2 * ATTN_WIDTH, 3 * ATTN_WIDTH
_GA0 = 3 * ATTN_WIDTH + POOL_WIDTH
_GP0 = _GA0 + D_MODEL
IN_WIDTH = _GP0 + D_MODEL


def _bdot(a, b):
    return jnp.dot(a, b, preferred_element_type=jnp.float32)


def _wdot2(w_ref, rows, act, cols=slice(None)):
    mid = (rows.start + rows.stop) // 2
    return [_bdot(w_ref[rows.start:mid, cols], act), _bdot(w_ref[mid:rows.stop, cols], act)]


def _halves(rows):
    mid = (rows.start + rows.stop) // 2
    return [slice(rows.start, mid), slice(mid, rows.stop)]


def _sigmoid(x):
    return 1.0 / (1.0 + jnp.exp(-x))


def _rms_scale(t):
    return lax.rsqrt(jnp.mean(t * t, axis=0, keepdims=True) + RMS_EPS)


def _head_norm_rope(t, gain, cos, sin):
    y = t * _rms_scale(t) * gain
    x1 = y[0:ROPE_HALF, :]
    x2 = y[ROPE_HALF:2 * ROPE_HALF, :]
    return jnp.concatenate(
        [x1 * cos - x2 * sin, x2 * cos + x1 * sin, y[2 * ROPE_HALF:, :]], axis=0)


def _shift_tokens(tiles, shift):
    lane = lax.broadcasted_iota(jnp.int32, tiles[0].shape, 1)
    rolled = [pltpu.roll(t, shift, axis=1) for t in tiles]
    out = [rolled[0]]
    for c in range(1, len(tiles)):
        out.append(jnp.where(lane < shift, rolled[c - 1], rolled[c]))
    return out


def _layer_kernel(x_ref, cos_ref, sin_ref, gmix_ref, gmlp_ref, qn_ref, kn_ref, ps_ref,
                  win_ref, wgrp_ref, wupa_ref, wupp_ref, wout_ref, wff1_ref, wff2_ref,
                  o_ref,
                  kbuf, vbuf, kmbuf, uprev, qbuf, m_ref, alpha_ref, acc_ref, s_ref, p_ref,
                  o_sc, op_sc, sga_sc, sgp_sc, mg_sc, x1_sc, h2_sc, r2_sc, f_sc):
    f32, bf16 = jnp.float32, jnp.bfloat16
    n = pl.program_id(0)
    n_tiles = pl.num_programs(0) - 1
    n_blocks = kbuf.shape[0]
    i = lax.rem(jnp.minimum(n, n_tiles - 1), n_blocks)

    @pl.when(n == 0)
    def _no_previous_tile():
        x1_sc[...] = jnp.zeros_like(x1_sc)
        h2_sc[...] = jnp.zeros_like(h2_sc)
        r2_sc[...] = jnp.zeros_like(r2_sc)

    @pl.when(i == 0)
    def _start_of_sequence():
        kmbuf[...] = jnp.zeros_like(kmbuf)
        uprev[...] = jnp.zeros_like(uprev)

    h2T = h2_sc[...]
    for c in range(D_FF // FF_CHUNK):
        chunk = slice(c * FF_CHUNK, (c + 1) * FF_CHUNK)
        for rows, z in zip(_halves(chunk), _wdot2(wff1_ref, chunk, h2T)):
            f = jnp.maximum(z, 0.0)
            f_sc[rows, :] = (f * f).astype(bf16)

    xT = x_ref[...].T
    hT = (xT * _rms_scale(xT) * gmix_ref[...]).astype(bf16)

    def proj2(lo, width):
        return _wdot2(win_ref, slice(lo, lo + width), hT)

    def head_rows(parts, h):
        per_part = HEADS // 2
        hh = h % per_part
        return parts[h // per_part][hh * HEAD_DIM:(hh + 1) * HEAD_DIM, :]

    cos = cos_ref[...]
    sin = sin_ref[...]

    k_parts = proj2(_K0, ATTN_WIDTH)
    q_parts = proj2(_Q0, ATTN_WIDTH)
    v_parts = proj2(_V0, ATTN_WIDTH)
    u_parts = proj2(_U0, POOL_WIDTH)
    all_rows = slice(0, D_MODEL)
    for rows, ga, gp in zip(_halves(all_rows), proj2(_GA0, D_MODEL), proj2(_GP0, D_MODEL)):
        sga_sc[rows, :] = _sigmoid(ga)
        sgp_sc[rows, :] = _sigmoid(gp)
    pad_row = lax.broadcasted_iota(jnp.int32, (KQ_DIM - HEAD_DIM, TOKENS), 0)
    own_block = jnp.where(pad_row == i, 1.0, 0.0)
    ones_row = jnp.where(
        lax.broadcasted_iota(jnp.int32, (V_ROWS - HEAD_DIM, TOKENS), 0) == 0, 1.0, 0.0)
    for h in range(HEADS):
        k_h = _head_norm_rope(head_rows(k_parts, h), kn_ref[...], cos, sin)
        k_tok = jnp.concatenate([k_h, own_block], axis=0).T
        kbuf[i, h] = k_tok.astype(bf16)
        kmean = jnp.sum(k_tok, axis=0, keepdims=True) * (1.0 / MOBA_BLOCK)
        kmbuf[h, pl.ds(i, 1), :] = kmean
        vbuf[i, h, :HEAD_DIM, :] = head_rows(v_parts, h).astype(bf16)
        vbuf[i, h, HEAD_DIM:, :] = ones_row.astype(bf16)

    blk = lax.broadcasted_iota(jnp.int32, (BIAS_ROWS, TOKENS), 0)
    fully_past = blk < i
    for h in range(HEADS):
        q_h = _head_norm_rope(head_rows(q_parts, h), qn_ref[...], cos, sin)
        q_h = (q_h * Q_SCALE).astype(bf16)
        gate = _bdot(kmbuf[h, :, :HEAD_DIM].astype(bf16), q_h)[:BIAS_ROWS, :]
        g = jnp.where(fully_past, gate, -jnp.inf)
        rank = jnp.zeros((BIAS_ROWS, TOKENS), jnp.int32)
        for jp in range(n_blocks - 1):
            g_jp = g[jp:jp + 1, :]
            ahead = jnp.logical_or(g_jp > g, jnp.logical_and(g_jp == g, blk > jp))
            rank = rank + ahead.astype(jnp.int32)
        chosen = jnp.logical_and(rank < MOBA_TOPK, fully_past)
        bias = jnp.where(jnp.logical_or(chosen, blk == i), 0.0, MASKED)
        zeros = jnp.zeros((KQ_DIM - HEAD_DIM - BIAS_ROWS, TOKENS), f32)
        qbuf[h, :HEAD_DIM, :] = q_h
        qbuf[h, HEAD_DIM:, :] = jnp.concatenate([bias, zeros], axis=0).astype(bf16)

    tok = i * TOKENS + lax.broadcasted_iota(jnp.int32, (1, TOKENS), 1)
    groups_per_part = len(POOL_WINDOWS) // 2
    for gi, w in enumerate(POOL_WINDOWS):
        rows = slice(gi * POOL_GROUP_WIDTH, (gi + 1) * POOL_GROUP_WIDTH)
        gg = gi % groups_per_part
        u_g = u_parts[gi // groups_per_part][gg * POOL_GROUP_WIDTH:(gg + 1) * POOL_GROUP_WIDTH, :]
        tiles = [uprev[rows, :]] + [u_g[:, c * LANES:(c + 1) * LANES]
                                    for c in range(TOKENS // LANES)]
        uprev[rows, :] = tiles[-1]
        shift = 1
        while shift < w:
            shifted = _shift_tokens(tiles, shift)
            tiles = [a + b for a, b in zip(tiles, shifted)]
            shift *= 2
        win_sum = jnp.concatenate(tiles[1:], axis=1)
        inv_cnt = 1.0 / jnp.minimum(tok + 1, w).astype(f32)
        pooled = win_sum * inv_cnt - u_g
        y = _bdot(wgrp_ref[gi], pooled.astype(bf16)) * ps_ref[rows, :]
        op_sc[rows, :] = y.astype(bf16)

    key_pos = lax.broadcasted_iota(jnp.int32, (MOBA_BLOCK, TOKENS), 0)
    qry_pos = lax.broadcasted_iota(jnp.int32, (MOBA_BLOCK, TOKENS), 1)
    causal = key_pos <= qry_pos

    def score_pass(j, own):
        for h in range(HEADS):
            s = _bdot(kbuf[j, h], qbuf[h])
            if own:
                s = jnp.where(causal, s, MASKED)
            s_ref[h] = s
            m_new = jnp.max(s, axis=0, keepdims=True)
            if not own:
                m_old = m_ref[h:h + 1, :]
                m_new = jnp.maximum(m_old, m_new)
                alpha_ref[h:h + 1, :] = jnp.exp2(m_old - m_new)
            m_ref[h:h + 1, :] = m_new

    def prob_pass():
        for h in range(HEADS):
            p_ref[h] = jnp.exp2(s_ref[h] - m_ref[h:h + 1, :]).astype(bf16)

    def value_pass(j, own):
        for h in range(HEADS):
            pv = _bdot(vbuf[j, h], p_ref[h])
            acc_ref[h] = pv if own else acc_ref[h] * alpha_ref[h:h + 1, :] + pv

    score_pass(i, True)

    fT = f_sc[...]
    r2_prev = r2_sc[0:1, :]
    for c in range(D_MODEL // OUT_CHUNK):
        chunk = slice(c * OUT_CHUNK, (c + 1) * OUT_CHUNK)
        for rows, y in zip(_halves(chunk), _wdot2(wff2_ref, chunk, fT)):
            o_ref[:, rows] = (x1_sc[rows, :] + y * r2_prev).T

    prob_pass()
    value_pass(i, True)

    def past_block(j, carry):
        score_pass(j, False)
        prob_pass()
        value_pass(j, False)
        return carry

    lax.fori_loop(0, jnp.where(n == n_tiles, 0, i), past_block, 0)

    for h in range(HEADS):
        o_h = acc_ref[h, :HEAD_DIM, :] * (1.0 / acc_ref[h, HEAD_DIM:HEAD_DIM + 1, :])
        o_sc[h * HEAD_DIM:(h + 1) * HEAD_DIM, :] = o_h.astype(bf16)

    upa_parts = _wdot2(wupa_ref, all_rows, o_sc[...])
    upp_parts = _wdot2(wupp_ref, all_rows, op_sc[...])
    for rows, upa, upp in zip(_halves(all_rows), upa_parts, upp_parts):
        mg_sc[rows, :] = (sga_sc[rows, :] * upa + sgp_sc[rows, :] * upp).astype(bf16)
    sumsq = jnp.zeros((1, TOKENS), f32)
    for rows, d in zip(_halves(all_rows), _wdot2(wout_ref, all_rows, mg_sc[...])):
        x1 = xT[rows, :] + d
        x1_sc[rows, :] = x1
        h2_sc[rows, :] = (x1 * gmlp_ref[rows, :]).astype(bf16)
        sumsq = sumsq + jnp.sum(x1 * x1, axis=0, keepdims=True)
    r = lax.rsqrt(sumsq * (1.0 / D_MODEL) + RMS_EPS)
    r2_sc[...] = jnp.broadcast_to(r * r, r2_sc.shape)


def _bcast_cols(v, n):
    return jnp.broadcast_to(v.astype(jnp.float32)[:, None], (v.shape[0], n))


@jax.jit
def kernel(x, norm_mix, w_in, q_norm, k_norm, w_pool_grp, pool_scale, w_up_attn, w_up_pool,
           w_out, norm_mlp, w_ff1, w_ff2):
    batch, seq, d_model = x.shape
    assert d_model == D_MODEL and seq % TOKENS == 0 and w_in.shape[0] == 1
    n_blocks = seq // TOKENS
    assert n_blocks <= BIAS_ROWS
    bf16 = jnp.bfloat16

    def wT(w):
        return w.T.astype(bf16)

    half = jnp.arange(ROPE_HALF, dtype=jnp.float32)
    inv_freq = ROPE_THETA ** (-half / ROPE_HALF)
    ang = inv_freq[:, None] * jnp.arange(seq).astype(jnp.float32)[None, :]

    n_tiles = batch * n_blocks

    def tile(n):
        t = jnp.clip(n, 0, n_tiles - 1)
        return t // n_blocks, t % n_blocks

    whole = pl.BlockSpec(memory_space=pltpu.VMEM)
    x_spec = pl.BlockSpec((None, TOKENS, D_MODEL), lambda n: (*tile(n), 0))
    out_spec = pl.BlockSpec((None, TOKENS, D_MODEL), lambda n: (*tile(n - 1), 0))
    rope_spec = pl.BlockSpec((ROPE_HALF, TOKENS), lambda n: (0, tile(n)[1]))

    return pl.pallas_call(
        _layer_kernel,
        grid=(n_tiles + 1,),
        in_specs=[x_spec, rope_spec, rope_spec] + [whole] * 12,
        out_specs=out_spec,
        out_shape=jax.ShapeDtypeStruct(x.shape, x.dtype),
        scratch_shapes=[
            pltpu.VMEM((n_blocks, HEADS, MOBA_BLOCK, KQ_DIM), bf16),
            pltpu.VMEM((n_blocks, HEADS, V_ROWS, MOBA_BLOCK), bf16),
            pltpu.VMEM((HEADS, 2 * BIAS_ROWS, KQ_DIM), jnp.float32),
            pltpu.VMEM((POOL_WIDTH, LANES), jnp.float32),
            pltpu.VMEM((HEADS, KQ_DIM, TOKENS), bf16),
            pltpu.VMEM((HEADS, TOKENS), jnp.float32),
            pltpu.VMEM((HEADS, TOKENS), jnp.float32),
            pltpu.VMEM((HEADS, V_ROWS, TOKENS), jnp.float32),
            pltpu.VMEM((HEADS, MOBA_BLOCK, TOKENS), jnp.float32),
            pltpu.VMEM((HEADS, MOBA_BLOCK, TOKENS), bf16),
            pltpu.VMEM((ATTN_WIDTH, TOKENS), bf16),
            pltpu.VMEM((POOL_WIDTH, TOKENS), bf16),
            pltpu.VMEM((D_MODEL, TOKENS), jnp.float32),
            pltpu.VMEM((D_MODEL, TOKENS), jnp.float32),
            pltpu.VMEM((D_MODEL, TOKENS), bf16),
            pltpu.VMEM((D_MODEL, TOKENS), jnp.float32),
            pltpu.VMEM((D_MODEL, TOKENS), bf16),
            pltpu.VMEM((8, TOKENS), jnp.float32),
            pltpu.VMEM((D_FF, TOKENS), bf16),
        ],
        compiler_params=pltpu.CompilerParams(
            dimension_semantics=("arbitrary",),
            vmem_limit_bytes=VMEM_LIMIT_BYTES),
        name="moba_pool_layer",
    )(x, jnp.cos(ang), jnp.sin(ang),
      _bcast_cols(norm_mix[0], TOKENS), _bcast_cols(norm_mlp[0], TOKENS),
      _bcast_cols(q_norm[0], TOKENS), _bcast_cols(k_norm[0], TOKENS),
      _bcast_cols(pool_scale[0], TOKENS),
      wT(w_in[0]), jnp.swapaxes(w_pool_grp[0], 1, 2).astype(bf16),
      wT(w_up_attn[0]), wT(w_up_pool[0]), wT(w_out[0]), wT(w_ff1[0]), wT(w_ff2[0]))
```

```python
import jax
import jax.numpy as jnp
from jax import lax
from jax.experimental import pallas as pl
from jax.experimental.pallas import tpu as pltpu

D_MODEL = 1024
HEADS = 8
HEAD_DIM = 64
ATTN_WIDTH = HEADS * HEAD_DIM
MOBA_BLOCK = 256
MOBA_TOPK = 3
ROPE_THETA = 500000.0
ROPE_HALF = HEAD_DIM // 8
POOL_WINDOWS = (2, 4, 8, 16)
POOL_GROUP_WIDTH = 128
POOL_WIDTH = len(POOL_WINDOWS) * POOL_GROUP_WIDTH
D_FF = 4 * D_MODEL
RMS_EPS = 1e-6

LANES = 128
TOKENS = MOBA_BLOCK
MASKED = -1e30
KQ_DIM = LANES
BIAS_ROWS = 8
V_ROWS = HEAD_DIM + 16
Q_SCALE = HEAD_DIM ** -0.5 * 1.4426950408889634
FF_CHUNK = 1024
OUT_CHUNK = 512
VMEM_LIMIT_BYTES = 61 * 1024 * 1024

_Q0, _K0, _V0, _U0 = 0, ATTN_WIDTH, 2 * ATTN_WIDTH, 3 * ATTN_WIDTH
_GA0 = 3 * ATTN_WIDTH + POOL_WIDTH
_GP0 = _GA0 + D_MODEL
IN_WIDTH = _GP0 + D_MODEL


def _bdot(a, b):
    return jnp.dot(a, b, preferred_element_type=jnp.float32)


def _wdot2(w_ref, rows, act, cols=slice(None)):
    mid = (rows.start + rows.stop) // 2
    return [_bdot(w_ref[rows.start:mid, cols], act), _bdot(w_ref[mid:rows.stop, cols], act)]


def _halves(rows):
    mid = (rows.start + rows.stop) // 2
    return [slice(rows.start, mid), slice(mid, rows.stop)]


def _sigmoid(x):
    return 1.0 / (1.0 + jnp.exp(-x))


def _rms_scale(t):
    return lax.rsqrt(jnp.mean(t * t, axis=0, keepdims=True) + RMS_EPS)


def _head_norm_rope(t, gain, cos, sin):
    y = t * _rms_scale(t) * gain
    x1 = y[0:ROPE_HALF, :]
    x2 = y[ROPE_HALF:2 * ROPE_HALF, :]
    return jnp.concatenate(
        [x1 * cos - x2 * sin, x2 * cos + x1 * sin, y[2 * ROPE_HALF:, :]], axis=0)


def _shift_tokens(tiles, shift):
    lane = lax.broadcasted_iota(jnp.int32, tiles[0].shape, 1)
    rolled = [pltpu.roll(t, shift, axis=1) for t in tiles]
    out = [rolled[0]]
    for c in range(1, len(tiles)):
        out.append(jnp.where(lane < shift, rolled[c - 1], rolled[c]))
    return out


def _layer_kernel(x_ref, cos_ref, sin_ref, gmix_ref, gmlp_ref, qn_ref, kn_ref, ps_ref,
                  win_ref, wgrp_ref, wupa_ref, wupp_ref, wout_ref, wff1_ref, wff2_ref,
                  o_ref,
                  kbuf, vbuf, kmbuf, uprev, qbuf, m_ref, alpha_ref, acc_ref, s_ref, p_ref,
                  o_sc, op_sc, sga_sc, sgp_sc, mg_sc, x1_sc, h2_sc, r2_sc, f_sc):
    f32, bf16 = jnp.float32, jnp.bfloat16
    n = pl.program_id(0)
    n_tiles = pl.num_programs(0) - 1
    n_blocks = kbuf.shape[0]
    i = lax.rem(jnp.minimum(n, n_tiles - 1), n_blocks)

    @pl.when(n == 0)
    def _no_previous_tile():
        x1_sc[...] = jnp.zeros_like(x1_sc)
        h2_sc[...] = jnp.zeros_like(h2_sc)
        r2_sc[...] = jnp.zeros_like(r2_sc)

    @pl.when(i == 0)
    def _start_of_sequence():
        kmbuf[...] = jnp.zeros_like(kmbuf)
        uprev[...] = jnp.zeros_like(uprev)

    h2T = h2_sc[...]
    for c in range(D_FF // FF_CHUNK):
        chunk = slice(c * FF_CHUNK, (c + 1) * FF_CHUNK)
        for rows, z in zip(_halves(chunk), _wdot2(wff1_ref, chunk, h2T)):
            f = jnp.maximum(z, 0.0)
            f_sc[rows, :] = (f * f).astype(bf16)

    xT = x_ref[...].T
    hT = (xT * _rms_scale(xT) * gmix_ref[...]).astype(bf16)

    def proj2(lo, width):
        return _wdot2(win_ref, slice(lo, lo + width), hT)

    def head_rows(parts, h):
        per_part = HEADS // 2
        hh = h % per_part
        return parts[h // per_part][hh * HEAD_DIM:(hh + 1) * HEAD_DIM, :]

    cos = cos_ref[...]
    sin = sin_ref[...]

    k_parts = proj2(_K0, ATTN_WIDTH)
    q_parts = proj2(_Q0, ATTN_WIDTH)
    v_parts = proj2(_V0, ATTN_WIDTH)
    u_parts = proj2(_U0, POOL_WIDTH)
    all_rows = slice(0, D_MODEL)
    for rows, ga, gp in zip(_halves(all_rows), proj2(_GA0, D_MODEL), proj2(_GP0, D_MODEL)):
        sga_sc[rows, :] = _sigmoid(ga)
        sgp_sc[rows, :] = _sigmoid(gp)
    pad_row = lax.broadcasted_iota(jnp.int32, (KQ_DIM - HEAD_DIM, TOKENS), 0)
    own_block = jnp.where(pad_row == i, 1.0, 0.0)
    ones_row = jnp.where(
        lax.broadcasted_iota(jnp.int32, (V_ROWS - HEAD_DIM, TOKENS), 0) == 0, 1.0, 0.0)
    for h in range(HEADS):
        k_h = _head_norm_rope(head_rows(k_parts, h), kn_ref[...], cos, sin)
        k_tok = jnp.concatenate([k_h, own_block], axis=0).T
        kbuf[i, h] = k_tok.astype(bf16)
        kmean = jnp.sum(k_tok, axis=0, keepdims=True) * (1.0 / MOBA_BLOCK)
        kmbuf[h, pl.ds(i, 1), :] = kmean
        vbuf[i, h, :HEAD_DIM, :] = head_rows(v_parts, h).astype(bf16)
        vbuf[i, h, HEAD_DIM:, :] = ones_row.astype(bf16)

    blk = lax.broadcasted_iota(jnp.int32, (BIAS_ROWS, TOKENS), 0)
    fully_past = blk < i
    for h in range(HEADS):
        q_h = _head_norm_rope(head_rows(q_parts, h), qn_ref[...], cos, sin)
        q_h = (q_h * Q_SCALE).astype(bf16)
        gate = _bdot(kmbuf[h, :, :HEAD_DIM].astype(bf16), q_h)[:BIAS_ROWS, :]
        g = jnp.where(fully_past, gate, -jnp.inf)
        rank = jnp.zeros((BIAS_ROWS, TOKENS), jnp.int32)
        for jp in range(n_blocks - 1):
            g_jp = g[jp:jp + 1, :]
            ahead = jnp.logical_or(g_jp > g, jnp.logical_and(g_jp == g, blk > jp))
            rank = rank + ahead.astype(jnp.int32)
        chosen = jnp.logical_and(rank < MOBA_TOPK, fully_past)
        bias = jnp.where(jnp.logical_or(chosen, blk == i), 0.0, MASKED)
        zeros = jnp.zeros((KQ_DIM - HEAD_DIM - BIAS_ROWS, TOKENS), f32)
        qbuf[h, :HEAD_DIM, :] = q_h
        qbuf[h, HEAD_DIM:, :] = jnp.concatenate([bias, zeros], axis=0).astype(bf16)

    tok = i * TOKENS + lax.broadcasted_iota(jnp.int32, (1, TOKENS), 1)
    groups_per_part = len(POOL_WINDOWS) // 2
    for gi, w in enumerate(POOL_WINDOWS):
        rows = slice(gi * POOL_GROUP_WIDTH, (gi + 1) * POOL_GROUP_WIDTH)
        gg = gi % groups_per_part
        u_g = u_parts[gi // groups_per_part][gg * POOL_GROUP_WIDTH:(gg + 1) * POOL_GROUP_WIDTH, :]
        tiles = [uprev[rows, :]] + [u_g[:, c * LANES:(c + 1) * LANES]
                                    for c in range(TOKENS // LANES)]
        uprev[rows, :] = tiles[-1]
        shift = 1
        while shift < w:
            shifted = _shift_tokens(tiles, shift)
            tiles = [a + b for a, b in zip(tiles, shifted)]
            shift *= 2
        win_sum = jnp.concatenate(tiles[1:], axis=1)
        inv_cnt = 1.0 / jnp.minimum(tok + 1, w).astype(f32)
        pooled = win_sum * inv_cnt - u_g
        y = _bdot(wgrp_ref[gi], pooled.astype(bf16)) * ps_ref[rows, :]
        op_sc[rows, :] = y.astype(bf16)

    key_pos = lax.broadcasted_iota(jnp.int32, (MOBA_BLOCK, TOKENS), 0)
    qry_pos = lax.broadcasted_iota(jnp.int32, (MOBA_BLOCK, TOKENS), 1)
    causal = key_pos <= qry_pos

    def score_pass(j, slot, own=False):
        for h in range(HEADS):
            s = _bdot(kbuf[j, h], qbuf[h])
            if own:
                s = jnp.where(causal, s, MASKED)
            s_ref[slot, h] = s
            m_new = jnp.max(s, axis=0, keepdims=True)
            if not own:
                m_old = m_ref[1 - slot, h:h + 1, :]
                m_new = jnp.maximum(m_old, m_new)
                alpha_ref[slot, h:h + 1, :] = jnp.exp2(m_old - m_new)
            m_ref[slot, h:h + 1, :] = m_new

    def prob_pass(slot):
        for h in range(HEADS):
            p_ref[slot, h] = jnp.exp2(s_ref[slot, h] - m_ref[slot, h:h + 1, :]).astype(bf16)

    def value_pass(j, slot, own=False):
        for h in range(HEADS):
            pv = _bdot(vbuf[j, h], p_ref[slot, h])
            acc_ref[h] = pv if own else acc_ref[h] * alpha_ref[slot, h:h + 1, :] + pv

    score_pass(i, 1, own=True)

    fT = f_sc[...]
    r2_prev = r2_sc[0:1, :]
    for c in range(D_MODEL // OUT_CHUNK):
        chunk = slice(c * OUT_CHUNK, (c + 1) * OUT_CHUNK)
        for rows, y in zip(_halves(chunk), _wdot2(wff2_ref, chunk, fT)):
            o_ref[:, rows] = (x1_sc[rows, :] + y * r2_prev).T

    prob_pass(1)
    value_pass(i, 1, own=True)

    n_past = jnp.where(n == n_tiles, 0, i)

    def past_pair(jj, carry):
        score_pass(2 * jj, 0)
        score_pass(2 * jj + 1, 1)
        prob_pass(0)
        value_pass(2 * jj, 0)
        prob_pass(1)
        value_pass(2 * jj + 1, 1)
        return carry

    lax.fori_loop(0, n_past // 2, past_pair, 0)

    @pl.when(n_past % 2 == 1)
    def _last_past_block():
        score_pass(n_past - 1, 0)
        prob_pass(0)
        value_pass(n_past - 1, 0)

    for h in range(HEADS):
        o_h = acc_ref[h, :HEAD_DIM, :] * (1.0 / acc_ref[h, HEAD_DIM:HEAD_DIM + 1, :])
        o_sc[h * HEAD_DIM:(h + 1) * HEAD_DIM, :] = o_h.astype(bf16)

    upa_parts = _wdot2(wupa_ref, all_rows, o_sc[...])
    upp_parts = _wdot2(wupp_ref, all_rows, op_sc[...])
    for rows, upa, upp in zip(_halves(all_rows), upa_parts, upp_parts):
        mg_sc[rows, :] = (sga_sc[rows, :] * upa + sgp_sc[rows, :] * upp).astype(bf16)
    sumsq = jnp.zeros((1, TOKENS), f32)
    for rows, d in zip(_halves(all_rows), _wdot2(wout_ref, all_rows, mg_sc[...])):
        x1 = xT[rows, :] + d
        x1_sc[rows, :] = x1
        h2_sc[rows, :] = (x1 * gmlp_ref[rows, :]).astype(bf16)
        sumsq = sumsq + jnp.sum(x1 * x1, axis=0, keepdims=True)
    r = lax.rsqrt(sumsq * (1.0 / D_MODEL) + RMS_EPS)
    r2_sc[...] = jnp.broadcast_to(r * r, r2_sc.shape)


def _bcast_cols(v, n):
    return jnp.broadcast_to(v.astype(jnp.float32)[:, None], (v.shape[0], n))


@jax.jit
def kernel(x, norm_mix, w_in, q_norm, k_norm, w_pool_grp, pool_scale, w_up_attn, w_up_pool,
           w_out, norm_mlp, w_ff1, w_ff2):
    batch, seq, d_model = x.shape
    assert d_model == D_MODEL and seq % TOKENS == 0 and w_in.shape[0] == 1
    n_blocks = seq // TOKENS
    assert n_blocks <= BIAS_ROWS
    bf16 = jnp.bfloat16

    def wT(w):
        return w.T.astype(bf16)

    half = jnp.arange(ROPE_HALF, dtype=jnp.float32)
    inv_freq = ROPE_THETA ** (-half / ROPE_HALF)
    ang = inv_freq[:, None] * jnp.arange(seq).astype(jnp.float32)[None, :]

    n_tiles = batch * n_blocks

    def tile(n):
        t = jnp.clip(n, 0, n_tiles - 1)
        return t // n_blocks, t % n_blocks

    whole = pl.BlockSpec(memory_space=pltpu.VMEM)
    x_spec = pl.BlockSpec((None, TOKENS, D_MODEL), lambda n: (*tile(n), 0))
    out_spec = pl.BlockSpec((None, TOKENS, D_MODEL), lambda n: (*tile(n - 1), 0))
    rope_spec = pl.BlockSpec((ROPE_HALF, TOKENS), lambda n: (0, tile(n)[1]))

    return pl.pallas_call(
        _layer_kernel,
        grid=(n_tiles + 1,),
        in_specs=[x_spec, rope_spec, rope_spec] + [whole] * 12,
        out_specs=out_spec,
        out_shape=jax.ShapeDtypeStruct(x.shape, x.dtype),
        scratch_shapes=[
            pltpu.VMEM((n_blocks, HEADS, MOBA_BLOCK, KQ_DIM), bf16),
            pltpu.VMEM((n_blocks, HEADS, V_ROWS, MOBA_BLOCK), bf16),
            pltpu.VMEM((HEADS, 2 * BIAS_ROWS, KQ_DIM), jnp.float32),
            pltpu.VMEM((POOL_WIDTH, LANES), jnp.float32),
            pltpu.VMEM((HEADS, KQ_DIM, TOKENS), bf16),
            pltpu.VMEM((2, HEADS, TOKENS), jnp.float32),
            pltpu.VMEM((2, HEADS, TOKENS), jnp.float32),
            pltpu.VMEM((HEADS, V_ROWS, TOKENS), jnp.float32),
            pltpu.VMEM((2, HEADS, MOBA_BLOCK, TOKENS), jnp.float32),
            pltpu.VMEM((2, HEADS, MOBA_BLOCK, TOKENS), bf16),
            pltpu.VMEM((ATTN_WIDTH, TOKENS), bf16),
            pltpu.VMEM((POOL_WIDTH, TOKENS), bf16),
            pltpu.VMEM((D_MODEL, TOKENS), jnp.float32),
            pltpu.VMEM((D_MODEL, TOKENS), jnp.float32),
            pltpu.VMEM((D_MODEL, TOKENS), bf16),
            pltpu.VMEM((D_MODEL, TOKENS), jnp.float32),
            pltpu.VMEM((D_MODEL, TOKENS), bf16),
            pltpu.VMEM((8, TOKENS), jnp.float32),
            pltpu.VMEM((D_FF, TOKENS), bf16),
        ],
        compiler_params=pltpu.CompilerParams(
            dimension_semantics=("arbitrary",),
            vmem_limit_bytes=VMEM_LIMIT_BYTES),
        name="moba_pool_layer",
    )(x, jnp.cos(ang), jnp.sin(ang),
      _bcast_cols(norm_mix[0], TOKENS), _bcast_cols(norm_mlp[0], TOKENS),
      _bcast_cols(q_norm[0], TOKENS), _bcast_cols(k_norm[0], TOKENS),
      _bcast_cols(pool_scale[0], TOKENS),
      wT(w_in[0]), jnp.swapaxes(w_pool_grp[0], 1, 2).astype(bf16),
      wT(w_up_attn[0]), wT(w_up_pool[0]), wT(w_out[0]), wT(w_ff1[0]), wT(w_ff2[0]))
```

```python
import jax
import jax.numpy as jnp
from jax import lax
from jax.experimental import pallas as pl
from jax.experimental.pallas import tpu as pltpu

D_MODEL = 1024
HEADS = 8
HEAD_DIM = 64
ATTN_WIDTH = HEADS * HEAD_DIM
MOBA_BLOCK = 256
MOBA_TOPK = 3
ROPE_THETA = 500000.0
ROPE_HALF = HEAD_DIM // 8
POOL_WINDOWS = (2, 4, 8, 16)
POOL_GROUP_WIDTH = 128
POOL_WIDTH = len(POOL_WINDOWS) * POOL_GROUP_WIDTH
D_FF = 4 * D_MODEL
RMS_EPS = 1e-6

LANES = 128
TOKENS = MOBA_BLOCK
MASKED = -1e30
KQ_DIM = LANES
BIAS_ROWS = 8
V_ROWS = HEAD_DIM + 16
Q_SCALE = HEAD_DIM ** -0.5 * 1.4426950408889634
FF_CHUNK = 1024
OUT_CHUNK = 512
VMEM_LIMIT_BYTES = 61 * 1024 * 1024

_Q0, _K0, _V0, _U0 = 0, ATTN_WIDTH, 2 * ATTN_WIDTH, 3 * ATTN_WIDTH
_GA0 = 3 * ATTN_WIDTH + POOL_WIDTH
_GP0 = _GA0 + D_MODEL
IN_WIDTH = _GP0 + D_MODEL


def _bdot(a, b):
    return jnp.dot(a, b, preferred_element_type=jnp.float32)


def _wdot2(w_ref, rows, act, cols=slice(None)):
    mid = (rows.start + rows.stop) // 2
    return [_bdot(w_ref[rows.start:mid, cols], act), _bdot(w_ref[mid:rows.stop, cols], act)]


def _halves(rows):
    mid = (rows.start + rows.stop) // 2
    return [slice(rows.start, mid), slice(mid, rows.stop)]


def _sigmoid(x):
    return 0.5 * jnp.tanh(0.5 * x) + 0.5


def _rms_scale(t):
    return lax.rsqrt(jnp.mean(t * t, axis=0, keepdims=True) + RMS_EPS)


def _head_norm_rope(t, gain, cos, sin):
    y = t * _rms_scale(t) * gain
    x1 = y[0:ROPE_HALF, :]
    x2 = y[ROPE_HALF:2 * ROPE_HALF, :]
    return jnp.concatenate(
        [x1 * cos - x2 * sin, x2 * cos + x1 * sin, y[2 * ROPE_HALF:, :]], axis=0)


def _shift_tokens(tiles, shift):
    lane = lax.broadcasted_iota(jnp.int32, tiles[0].shape, 1)
    rolled = [pltpu.roll(t, shift, axis=1) for t in tiles]
    out = [rolled[0]]
    for c in range(1, len(tiles)):
        out.append(jnp.where(lane < shift, rolled[c - 1], rolled[c]))
    return out


def _layer_kernel(x_ref, cos_ref, sin_ref, gmix_ref, gmlp_ref, qn_ref, kn_ref, ps_ref,
                  win_ref, wgrp_ref, wupa_ref, wupp_ref, wout_ref, wff1_ref, wff2_ref,
                  o_ref,
                  kbuf, vbuf, kmbuf, uprev, qbuf, m_ref, alpha_ref, acc_ref, s_ref, p_ref,
                  o_sc, op_sc, sga_sc, sgp_sc, mg_sc, x1_sc, h2_sc, r2_sc, f_sc):
    f32, bf16 = jnp.float32, jnp.bfloat16
    n = pl.program_id(0)
    n_tiles = pl.num_programs(0) - 1
    n_blocks = kbuf.shape[0]
    i = lax.rem(jnp.minimum(n, n_tiles - 1), n_blocks)

    @pl.when(n == 0)
    def _no_previous_tile():
        x1_sc[...] = jnp.zeros_like(x1_sc)
        h2_sc[...] = jnp.zeros_like(h2_sc)
        r2_sc[...] = jnp.zeros_like(r2_sc)

    @pl.when(i == 0)
    def _start_of_sequence():
        kmbuf[...] = jnp.zeros_like(kmbuf)
        uprev[...] = jnp.zeros_like(uprev)

    h2T = h2_sc[...]
    for c in range(D_FF // FF_CHUNK):
        chunk = slice(c * FF_CHUNK, (c + 1) * FF_CHUNK)
        for rows, z in zip(_halves(chunk), _wdot2(wff1_ref, chunk, h2T)):
            f = jnp.maximum(z, 0.0)
            f_sc[rows, :] = (f * f).astype(bf16)

    xT = x_ref[...].T
    hT = (xT * _rms_scale(xT) * gmix_ref[...]).astype(bf16)

    def proj2(lo, width):
        return _wdot2(win_ref, slice(lo, lo + width), hT)

    def head_rows(parts, h):
        per_part = HEADS // 2
        hh = h % per_part
        return parts[h // per_part][hh * HEAD_DIM:(hh + 1) * HEAD_DIM, :]

    cos = cos_ref[...]
    sin = sin_ref[...]

    k_parts = proj2(_K0, ATTN_WIDTH)
    q_parts = proj2(_Q0, ATTN_WIDTH)
    v_parts = proj2(_V0, ATTN_WIDTH)
    u_parts = proj2(_U0, POOL_WIDTH)
    all_rows = slice(0, D_MODEL)
    for rows, ga, gp in zip(_halves(all_rows), proj2(_GA0, D_MODEL), proj2(_GP0, D_MODEL)):
        sga_sc[rows, :] = _sigmoid(ga)
        sgp_sc[rows, :] = _sigmoid(gp)
    pad_row = lax.broadcasted_iota(jnp.int32, (KQ_DIM - HEAD_DIM, TOKENS), 0)
    own_block = jnp.where(pad_row == i, 1.0, 0.0)
    ones_row = jnp.where(
        lax.broadcasted_iota(jnp.int32, (V_ROWS - HEAD_DIM, TOKENS), 0) == 0, 1.0, 0.0)
    for h in range(HEADS):
        k_h = _head_norm_rope(head_rows(k_parts, h), kn_ref[...], cos, sin)
        k_tok = jnp.concatenate([k_h, own_block], axis=0).T
        kbuf[i, h] = k_tok.astype(bf16)
        kmean = jnp.sum(k_tok, axis=0, keepdims=True) * (1.0 / MOBA_BLOCK)
        kmbuf[h, pl.ds(i, 1), :] = kmean
        vbuf[i, h, :HEAD_DIM, :] = head_rows(v_parts, h).astype(bf16)
        vbuf[i, h, HEAD_DIM:, :] = ones_row.astype(bf16)

    blk = lax.broadcasted_iota(jnp.int32, (BIAS_ROWS, TOKENS), 0)
    fully_past = blk < i
    for h in range(HEADS):
        q_h = _head_norm_rope(head_rows(q_parts, h), qn_ref[...], cos, sin)
        q_h = (q_h * Q_SCALE).astype(bf16)
        gate = _bdot(kmbuf[h, :, :HEAD_DIM].astype(bf16), q_h)[:BIAS_ROWS, :]
        g = jnp.where(fully_past, gate, -jnp.inf)
        rank = jnp.zeros((BIAS_ROWS, TOKENS), jnp.int32)
        for jp in range(n_blocks - 1):
            g_jp = g[jp:jp + 1, :]
            ahead = jnp.logical_or(g_jp > g, jnp.logical_and(g_jp == g, blk > jp))
            rank = rank + ahead.astype(jnp.int32)
        chosen = jnp.logical_and(rank < MOBA_TOPK, fully_past)
        bias = jnp.where(jnp.logical_or(chosen, blk == i), 0.0, MASKED)
        zeros = jnp.zeros((KQ_DIM - HEAD_DIM - BIAS_ROWS, TOKENS), f32)
        qbuf[h, :HEAD_DIM, :] = q_h
        qbuf[h, HEAD_DIM:, :] = jnp.concatenate([bias, zeros], axis=0).astype(bf16)

    tok = i * TOKENS + lax.broadcasted_iota(jnp.int32, (1, TOKENS), 1)
    groups_per_part = len(POOL_WINDOWS) // 2
    for gi, w in enumerate(POOL_WINDOWS):
        rows = slice(gi * POOL_GROUP_WIDTH, (gi + 1) * POOL_GROUP_WIDTH)
        gg = gi % groups_per_part
        u_g = u_parts[gi // groups_per_part][gg * POOL_GROUP_WIDTH:(gg + 1) * POOL_GROUP_WIDTH, :]
        tiles = [uprev[rows, :]] + [u_g[:, c * LANES:(c + 1) * LANES]
                                    for c in range(TOKENS // LANES)]
        uprev[rows, :] = tiles[-1]
        shift = 1
        while shift < w:
            shifted = _shift_tokens(tiles, shift)
            tiles = [a + b for a, b in zip(tiles, shifted)]
            shift *= 2
        win_sum = jnp.concatenate(tiles[1:], axis=1)
        inv_cnt = 1.0 / jnp.minimum(tok + 1, w).astype(f32)
        pooled = win_sum * inv_cnt - u_g
        y = _bdot(wgrp_ref[gi], pooled.astype(bf16)) * ps_ref[rows, :]
        op_sc[rows, :] = y.astype(bf16)

    key_pos = lax.broadcasted_iota(jnp.int32, (MOBA_BLOCK, TOKENS), 0)
    qry_pos = lax.broadcasted_iota(jnp.int32, (MOBA_BLOCK, TOKENS), 1)
    causal = key_pos <= qry_pos

    def score_pass(j, slot, own=False):
        for h in range(HEADS):
            s = _bdot(kbuf[j, h], qbuf[h])
            if own:
                s = jnp.where(causal, s, MASKED)
            s_ref[slot, h] = s
            m_new = jnp.max(s, axis=0, keepdims=True)
            if not own:
                m_old = m_ref[1 - slot, h:h + 1, :]
                m_new = jnp.maximum(m_old, m_new)
                alpha_ref[slot, h:h + 1, :] = jnp.exp2(m_old - m_new)
            m_ref[slot, h:h + 1, :] = m_new

    def prob_pass(slot):
        for h in range(HEADS):
            p_ref[slot, h] = jnp.exp2(s_ref[slot, h] - m_ref[slot, h:h + 1, :]).astype(bf16)

    def value_pass(j, slot, own=False):
        for h in range(HEADS):
            pv = _bdot(vbuf[j, h], p_ref[slot, h])
            acc_ref[h] = pv if own else acc_ref[h] * alpha_ref[slot, h:h + 1, :] + pv

    score_pass(i, 1, own=True)

    fT = f_sc[...]
    r2_prev = r2_sc[0:1, :]
    for c in range(D_MODEL // OUT_CHUNK):
        chunk = slice(c * OUT_CHUNK, (c + 1) * OUT_CHUNK)
        for rows, y in zip(_halves(chunk), _wdot2(wff2_ref, chunk, fT)):
            o_ref[:, rows] = (x1_sc[rows, :] + y * r2_prev).T

    prob_pass(1)
    value_pass(i, 1, own=True)

    n_past = jnp.where(n == n_tiles, 0, i)

    def past_pair(jj, carry):
        score_pass(2 * jj, 0)
        score_pass(2 * jj + 1, 1)
        prob_pass(0)
        value_pass(2 * jj, 0)
        prob_pass(1)
        value_pass(2 * jj + 1, 1)
        return carry

    lax.fori_loop(0, n_past // 2, past_pair, 0)

    @pl.when(n_past % 2 == 1)
    def _last_past_block():
        score_pass(n_past - 1, 0)
        prob_pass(0)
        value_pass(n_past - 1, 0)

    for h in range(HEADS):
        o_h = acc_ref[h, :HEAD_DIM, :] * (1.0 / acc_ref[h, HEAD_DIM:HEAD_DIM + 1, :])
        o_sc[h * HEAD_DIM:(h + 1) * HEAD_DIM, :] = o_h.astype(bf16)

    upa_parts = _wdot2(wupa_ref, all_rows, o_sc[...])
    upp_parts = _wdot2(wupp_ref, all_rows, op_sc[...])
    for rows, upa, upp in zip(_halves(all_rows), upa_parts, upp_parts):
        mg_sc[rows, :] = (sga_sc[rows, :] * upa + sgp_sc[rows, :] * upp).astype(bf16)
    sumsq = jnp.zeros((1, TOKENS), f32)
    for rows, d in zip(_halves(all_rows), _wdot2(wout_ref, all_rows, mg_sc[...])):
        x1 = xT[rows, :] + d
        x1_sc[rows, :] = x1
        h2_sc[rows, :] = (x1 * gmlp_ref[rows, :]).astype(bf16)
        sumsq = sumsq + jnp.sum(x1 * x1, axis=0, keepdims=True)
    r = lax.rsqrt(sumsq * (1.0 / D_MODEL) + RMS_EPS)
    r2_sc[...] = jnp.broadcast_to(r * r, r2_sc.shape)


def _bcast_cols(v, n):
    return jnp.broadcast_to(v.astype(jnp.float32)[:, None], (v.shape[0], n))


@jax.jit
def kernel(x, norm_mix, w_in, q_norm, k_norm, w_pool_grp, pool_scale, w_up_attn, w_up_pool,
           w_out, norm_mlp, w_ff1, w_ff2):
    batch, seq, d_model = x.shape
    assert d_model == D_MODEL and seq % TOKENS == 0 and w_in.shape[0] == 1
    n_blocks = seq // TOKENS
    assert n_blocks <= BIAS_ROWS
    bf16 = jnp.bfloat16

    def wT(w):
        return w.T.astype(bf16)

    half = jnp.arange(ROPE_HALF, dtype=jnp.float32)
    inv_freq = ROPE_THETA ** (-half / ROPE_HALF)
    ang = inv_freq[:, None] * jnp.arange(seq).astype(jnp.float32)[None, :]

    n_tiles = batch * n_blocks

    def tile(n):
        t = jnp.clip(n, 0, n_tiles - 1)
        return t // n_blocks, t % n_blocks

    whole = pl.BlockSpec(memory_space=pltpu.VMEM)
    x_spec = pl.BlockSpec((None, TOKENS, D_MODEL), lambda n: (*tile(n), 0))
    out_spec = pl.BlockSpec((None, TOKENS, D_MODEL), lambda n: (*tile(n - 1), 0))
    rope_spec = pl.BlockSpec((ROPE_HALF, TOKENS), lambda n: (0, tile(n)[1]))

    return pl.pallas_call(
        _layer_kernel,
        grid=(n_tiles + 1,),
        in_specs=[x_spec, rope_spec, rope_spec] + [whole] * 12,
        out_specs=out_spec,
        out_shape=jax.ShapeDtypeStruct(x.shape, x.dtype),
        scratch_shapes=[
            pltpu.VMEM((n_blocks, HEADS, MOBA_BLOCK, KQ_DIM), bf16),
            pltpu.VMEM((n_blocks, HEADS, V_ROWS, MOBA_BLOCK), bf16),
            pltpu.VMEM((HEADS, 2 * BIAS_ROWS, KQ_DIM), jnp.float32),
            pltpu.VMEM((POOL_WIDTH, LANES), jnp.float32),
            pltpu.VMEM((HEADS, KQ_DIM, TOKENS), bf16),
            pltpu.VMEM((2, HEADS, TOKENS), jnp.float32),
            pltpu.VMEM((2, HEADS, TOKENS), jnp.float32),
            pltpu.VMEM((HEADS, V_ROWS, TOKENS), jnp.float32),
            pltpu.VMEM((2, HEADS, MOBA_BLOCK, TOKENS), jnp.float32),
            pltpu.VMEM((2, HEADS, MOBA_BLOCK, TOKENS), bf16),
            pltpu.VMEM((ATTN_WIDTH, TOKENS), bf16),
            pltpu.VMEM((POOL_WIDTH, TOKENS), bf16),
            pltpu.VMEM((D_MODEL, TOKENS), jnp.float32),
            pltpu.VMEM((D_MODEL, TOKENS), jnp.float32),
            pltpu.VMEM((D_MODEL, TOKENS), bf16),
            pltpu.VMEM((D_MODEL, TOKENS), jnp.float32),
            pltpu.VMEM((D_MODEL, TOKENS), bf16),
            pltpu.VMEM((8, TOKENS), jnp.float32),
            pltpu.VMEM((D_FF, TOKENS), bf16),
        ],
        compiler_params=pltpu.CompilerParams(
            dimension_semantics=("arbitrary",),
            vmem_limit_bytes=VMEM_LIMIT_BYTES),
        name="moba_pool_layer",
    )(x, jnp.cos(ang), jnp.sin(ang),
      _bcast_cols(norm_mix[0], TOKENS), _bcast_cols(norm_mlp[0], TOKENS),
      _bcast_cols(q_norm[0], TOKENS), _bcast_cols(k_norm[0], TOKENS),
      _bcast_cols(pool_scale[0], TOKENS),
      wT(w_in[0]), jnp.swapaxes(w_pool_grp[0], 1, 2).astype(bf16),
      wT(w_up_attn[0]), wT(w_up_pool[0]), wT(w_out[0]), wT(w_ff1[0]), wT(w_ff2[0]))
```

```python
import jax
import jax.numpy as jnp
from jax import lax
from jax.experimental import pallas as pl
from jax.experimental.pallas import tpu as pltpu

D_MODEL = 1024
HEADS = 8
HEAD_DIM = 64
ATTN_WIDTH = HEADS * HEAD_DIM
MOBA_BLOCK = 256
MOBA_TOPK = 3
ROPE_THETA = 500000.0
ROPE_HALF = HEAD_DIM // 8
POOL_WINDOWS = (2, 4, 8, 16)
POOL_GROUP_WIDTH = 128
POOL_WIDTH = len(POOL_WINDOWS) * POOL_GROUP_WIDTH
D_FF = 4 * D_MODEL
RMS_EPS = 1e-6

LANES = 128
TOKENS = MOBA_BLOCK
MASKED = -1e30
KQ_DIM = LANES
BIAS_ROWS = 8
V_ROWS = HEAD_DIM + 16
Q_SCALE = HEAD_DIM ** -0.5 * 1.4426950408889634
FF_CHUNK = 1024
OUT_CHUNK = 512
VMEM_LIMIT_BYTES = 61 * 1024 * 1024

_Q0, _K0, _V0, _U0 = 0, ATTN_WIDTH, 2 * ATTN_WIDTH, 3 * ATTN_WIDTH
_GA0 = 3 * ATTN_WIDTH + POOL_WIDTH
_GP0 = _GA0 + D_MODEL
IN_WIDTH = _GP0 + D_MODEL


def _bdot(a, b):
    return jnp.dot(a, b, preferred_element_type=jnp.float32)


def _wdot2(w_ref, rows, act, cols=slice(None)):
    mid = (rows.start + rows.stop) // 2
    return [_bdot(w_ref[rows.start:mid, cols], act), _bdot(w_ref[mid:rows.stop, cols], act)]


def _halves(rows):
    mid = (rows.start + rows.stop) // 2
    return [slice(rows.start, mid), slice(mid, rows.stop)]


def _sigmoid(x):
    return 0.5 * jnp.tanh(0.5 * x) + 0.5


def _rms_scale(t):
    return lax.rsqrt(jnp.mean(t * t, axis=0, keepdims=True) + RMS_EPS)


def _head_norm_rope(t, gain, cos, sin):
    y = t * _rms_scale(t) * gain
    x1 = y[0:ROPE_HALF, :]
    x2 = y[ROPE_HALF:2 * ROPE_HALF, :]
    return jnp.concatenate(
        [x1 * cos - x2 * sin, x2 * cos + x1 * sin, y[2 * ROPE_HALF:, :]], axis=0)


def _shift_tokens(tiles, shift):
    lane = lax.broadcasted_iota(jnp.int32, tiles[0].shape, 1)
    rolled = [pltpu.roll(t, shift, axis=1) for t in tiles]
    out = [rolled[0]]
    for c in range(1, len(tiles)):
        out.append(jnp.where(lane < shift, rolled[c - 1], rolled[c]))
    return out


def _load_weights(pairs, stage, sems):
    side = stage.shape[-1]
    slots_per_row = stage.shape[1]
    n_slots = stage.shape[0] * slots_per_row
    tiles = [(src, dst, k0, n0)
             for src, dst in pairs
             for k0 in range(0, dst.shape[1], side)
             for n0 in range(0, dst.shape[0], side)]

    def slot_of(t):
        slot = t % n_slots
        return slot, stage.at[slot // slots_per_row, slot % slots_per_row]

    def copy(t):
        src, _, k0, n0 = tiles[t]
        slot, buf = slot_of(t)
        return pltpu.make_async_copy(src.at[0, pl.ds(k0, side), pl.ds(n0, side)], buf, sems.at[slot])

    for t in range(min(n_slots, len(tiles))):
        copy(t).start()
    group = 4
    for t0 in range(0, len(tiles), group):
        ts = range(t0, min(t0 + group, len(tiles)))
        for t in ts:
            copy(t).wait()
        for t in ts:
            _, dst, k0, n0 = tiles[t]
            dst[n0:n0 + side, k0:k0 + side] = slot_of(t)[1][...].T.astype(dst.dtype)
        for t in ts:
            if t + n_slots < len(tiles):
                copy(t + n_slots).start()


def _layer_kernel(x_ref, cos_ref, sin_ref, gmix_ref, gmlp_ref, qn_ref, kn_ref, ps_ref, wgrp_ref,
                  win_hbm, wupa_hbm, wupp_hbm, wout_hbm, wff1_hbm, wff2_hbm,
                  o_ref,
                  win_ref, wupa_ref, wupp_ref, wout_ref, wff1_ref, wff2_ref, wsem,
                  kbuf, vbuf, kmbuf, uprev, qbuf, m_ref, alpha_ref, acc_ref, s_ref, p_ref,
                  o_sc, op_sc, sga_sc, sgp_sc, mg_sc, x1_sc, h2_sc, r2_sc, f_sc):
    f32, bf16 = jnp.float32, jnp.bfloat16
    n = pl.program_id(0)
    n_tiles = pl.num_programs(0) - 1
    n_blocks = kbuf.shape[0]
    i = lax.rem(jnp.minimum(n, n_tiles - 1), n_blocks)

    @pl.when(n == 0)
    def _first_step():
        _load_weights([(win_hbm, win_ref), (wupa_hbm, wupa_ref), (wupp_hbm, wupp_ref),
                       (wout_hbm, wout_ref), (wff1_hbm, wff1_ref), (wff2_hbm, wff2_ref)],
                      s_ref, wsem)
        x1_sc[...] = jnp.zeros_like(x1_sc)
        h2_sc[...] = jnp.zeros_like(h2_sc)
        r2_sc[...] = jnp.zeros_like(r2_sc)

    @pl.when(i == 0)
    def _start_of_sequence():
        kmbuf[...] = jnp.zeros_like(kmbuf)
        uprev[...] = jnp.zeros_like(uprev)

    h2T = h2_sc[...]
    for c in range(D_FF // FF_CHUNK):
        chunk = slice(c * FF_CHUNK, (c + 1) * FF_CHUNK)
        for rows, z in zip(_halves(chunk), _wdot2(wff1_ref, chunk, h2T)):
            f = jnp.maximum(z, 0.0)
            f_sc[rows, :] = (f * f).astype(bf16)

    xT = x_ref[...].T
    hT = (xT * _rms_scale(xT) * gmix_ref[...]).astype(bf16)

    def proj2(lo, width):
        return _wdot2(win_ref, slice(lo, lo + width), hT)

    def head_rows(parts, h):
        per_part = HEADS // 2
        hh = h % per_part
        return parts[h // per_part][hh * HEAD_DIM:(hh + 1) * HEAD_DIM, :]

    cos = cos_ref[...]
    sin = sin_ref[...]

    k_parts = proj2(_K0, ATTN_WIDTH)
    q_parts = proj2(_Q0, ATTN_WIDTH)
    v_parts = proj2(_V0, ATTN_WIDTH)
    u_parts = proj2(_U0, POOL_WIDTH)
    all_rows = slice(0, D_MODEL)
    for rows, ga, gp in zip(_halves(all_rows), proj2(_GA0, D_MODEL), proj2(_GP0, D_MODEL)):
        sga_sc[rows, :] = _sigmoid(ga)
        sgp_sc[rows, :] = _sigmoid(gp)
    pad_row = lax.broadcasted_iota(jnp.int32, (KQ_DIM - HEAD_DIM, TOKENS), 0)
    own_block = jnp.where(pad_row == i, 1.0, 0.0)
    ones_row = jnp.where(
        lax.broadcasted_iota(jnp.int32, (V_ROWS - HEAD_DIM, TOKENS), 0) == 0, 1.0, 0.0)
    for h in range(HEADS):
        k_h = _head_norm_rope(head_rows(k_parts, h), kn_ref[...], cos, sin)
        k_tok = jnp.concatenate([k_h, own_block], axis=0).T
        kbuf[i, h] = k_tok.astype(bf16)
        kmean = jnp.sum(k_tok, axis=0, keepdims=True) * (1.0 / MOBA_BLOCK)
        kmbuf[h, pl.ds(i, 1), :] = kmean
        vbuf[i, h, :HEAD_DIM, :] = head_rows(v_parts, h).astype(bf16)
        vbuf[i, h, HEAD_DIM:, :] = ones_row.astype(bf16)

    blk = lax.broadcasted_iota(jnp.int32, (BIAS_ROWS, TOKENS), 0)
    fully_past = blk < i
    for h in range(HEADS):
        q_h = _head_norm_rope(head_rows(q_parts, h), qn_ref[...], cos, sin)
        q_h = (q_h * Q_SCALE).astype(bf16)
        gate = _bdot(kmbuf[h, :, :HEAD_DIM].astype(bf16), q_h)[:BIAS_ROWS, :]
        g = jnp.where(fully_past, gate, -jnp.inf)
        rank = jnp.zeros((BIAS_ROWS, TOKENS), jnp.int32)
        for jp in range(n_blocks - 1):
            g_jp = g[jp:jp + 1, :]
            ahead = jnp.logical_or(g_jp > g, jnp.logical_and(g_jp == g, blk > jp))
            rank = rank + ahead.astype(jnp.int32)
        chosen = jnp.logical_and(rank < MOBA_TOPK, fully_past)
        bias = jnp.where(jnp.logical_or(chosen, blk == i), 0.0, MASKED)
        zeros = jnp.zeros((KQ_DIM - HEAD_DIM - BIAS_ROWS, TOKENS), f32)
        qbuf[h, :HEAD_DIM, :] = q_h
        qbuf[h, HEAD_DIM:, :] = jnp.concatenate([bias, zeros], axis=0).astype(bf16)

    tok = i * TOKENS + lax.broadcasted_iota(jnp.int32, (1, TOKENS), 1)
    groups_per_part = len(POOL_WINDOWS) // 2
    for gi, w in enumerate(POOL_WINDOWS):
        rows = slice(gi * POOL_GROUP_WIDTH, (gi + 1) * POOL_GROUP_WIDTH)
        gg = gi % groups_per_part
        u_g = u_parts[gi // groups_per_part][gg * POOL_GROUP_WIDTH:(gg + 1) * POOL_GROUP_WIDTH, :]
        tiles = [uprev[rows, :]] + [u_g[:, c * LANES:(c + 1) * LANES]
                                    for c in range(TOKENS // LANES)]
        uprev[rows, :] = tiles[-1]
        shift = 1
        while shift < w:
            shifted = _shift_tokens(tiles, shift)
            tiles = [a + b for a, b in zip(tiles, shifted)]
            shift *= 2
        win_sum = jnp.concatenate(tiles[1:], axis=1)
        inv_cnt = 1.0 / jnp.minimum(tok + 1, w).astype(f32)
        pooled = win_sum * inv_cnt - u_g
        y = _bdot(wgrp_ref[gi], pooled.astype(bf16)) * ps_ref[rows, :]
        op_sc[rows, :] = y.astype(bf16)

    key_pos = lax.broadcasted_iota(jnp.int32, (MOBA_BLOCK, TOKENS), 0)
    qry_pos = lax.broadcasted_iota(jnp.int32, (MOBA_BLOCK, TOKENS), 1)
    causal = key_pos <= qry_pos

    def score_pass(j, slot, own=False):
        for h in range(HEADS):
            s = _bdot(kbuf[j, h], qbuf[h])
            if own:
                s = jnp.where(causal, s, MASKED)
            s_ref[slot, h] = s
            m_new = jnp.max(s, axis=0, keepdims=True)
            if not own:
                m_old = m_ref[1 - slot, h:h + 1, :]
                m_new = jnp.maximum(m_old, m_new)
                alpha_ref[slot, h:h + 1, :] = jnp.exp2(m_old - m_new)
            m_ref[slot, h:h + 1, :] = m_new

    def prob_pass(slot):
        for h in range(HEADS):
            p_ref[slot, h] = jnp.exp2(s_ref[slot, h] - m_ref[slot, h:h + 1, :]).astype(bf16)

    def value_pass(j, slot, own=False):
        for h in range(HEADS):
            pv = _bdot(vbuf[j, h], p_ref[slot, h])
            acc_ref[h] = pv if own else acc_ref[h] * alpha_ref[slot, h:h + 1, :] + pv

    score_pass(i, 1, own=True)

    fT = f_sc[...]
    r2_prev = r2_sc[0:1, :]
    for c in range(D_MODEL // OUT_CHUNK):
        chunk = slice(c * OUT_CHUNK, (c + 1) * OUT_CHUNK)
        for rows, y in zip(_halves(chunk), _wdot2(wff2_ref, chunk, fT)):
            o_ref[:, rows] = (x1_sc[rows, :] + y * r2_prev).T

    prob_pass(1)
    value_pass(i, 1, own=True)

    n_past = jnp.where(n == n_tiles, 0, i)

    def past_pair(jj, carry):
        score_pass(2 * jj, 0)
        score_pass(2 * jj + 1, 1)
        prob_pass(0)
        value_pass(2 * jj, 0)
        prob_pass(1)
        value_pass(2 * jj + 1, 1)
        return carry

    lax.fori_loop(0, n_past // 2, past_pair, 0)

    @pl.when(n_past % 2 == 1)
    def _last_past_block():
        score_pass(n_past - 1, 0)
        prob_pass(0)
        value_pass(n_past - 1, 0)

    for h in range(HEADS):
        o_h = acc_ref[h, :HEAD_DIM, :] * (1.0 / acc_ref[h, HEAD_DIM:HEAD_DIM + 1, :])
        o_sc[h * HEAD_DIM:(h + 1) * HEAD_DIM, :] = o_h.astype(bf16)

    upa_parts = _wdot2(wupa_ref, all_rows, o_sc[...])
    upp_parts = _wdot2(wupp_ref, all_rows, op_sc[...])
    for rows, upa, upp in zip(_halves(all_rows), upa_parts, upp_parts):
        mg_sc[rows, :] = (sga_sc[rows, :] * upa + sgp_sc[rows, :] * upp).astype(bf16)
    sumsq = jnp.zeros((1, TOKENS), f32)
    for rows, d in zip(_halves(all_rows), _wdot2(wout_ref, all_rows, mg_sc[...])):
        x1 = xT[rows, :] + d
        x1_sc[rows, :] = x1
        h2_sc[rows, :] = (x1 * gmlp_ref[rows, :]).astype(bf16)
        sumsq = sumsq + jnp.sum(x1 * x1, axis=0, keepdims=True)
    r = lax.rsqrt(sumsq * (1.0 / D_MODEL) + RMS_EPS)
    r2_sc[...] = jnp.broadcast_to(r * r, r2_sc.shape)


def _bcast_cols(v, n):
    return jnp.broadcast_to(v.astype(jnp.float32)[:, None], (v.shape[0], n))


@jax.jit
def kernel(x, norm_mix, w_in, q_norm, k_norm, w_pool_grp, pool_scale, w_up_attn, w_up_pool,
           w_out, norm_mlp, w_ff1, w_ff2):
    batch, seq, d_model = x.shape
    assert d_model == D_MODEL and seq % TOKENS == 0 and w_in.shape[0] == 1
    n_blocks = seq // TOKENS
    assert n_blocks <= BIAS_ROWS
    bf16 = jnp.bfloat16

    half = jnp.arange(ROPE_HALF, dtype=jnp.float32)
    inv_freq = ROPE_THETA ** (-half / ROPE_HALF)
    ang = inv_freq[:, None] * jnp.arange(seq).astype(jnp.float32)[None, :]

    n_tiles = batch * n_blocks

    def tile(n):
        t = jnp.clip(n, 0, n_tiles - 1)
        return t // n_blocks, t % n_blocks

    whole = pl.BlockSpec(memory_space=pltpu.VMEM)
    in_hbm = pl.BlockSpec(memory_space=pl.ANY)
    weights = (w_in, w_up_attn, w_up_pool, w_out, w_ff1, w_ff2)
    x_spec = pl.BlockSpec((None, TOKENS, D_MODEL), lambda n: (*tile(n), 0))
    out_spec = pl.BlockSpec((None, TOKENS, D_MODEL), lambda n: (*tile(n - 1), 0))
    rope_spec = pl.BlockSpec((ROPE_HALF, TOKENS), lambda n: (0, tile(n)[1]))

    return pl.pallas_call(
        _layer_kernel,
        grid=(n_tiles + 1,),
        in_specs=[x_spec, rope_spec, rope_spec] + [whole] * 6 + [in_hbm] * len(weights),
        out_specs=out_spec,
        out_shape=jax.ShapeDtypeStruct(x.shape, x.dtype),
        scratch_shapes=[
            *[pltpu.VMEM((w.shape[2], w.shape[1]), bf16) for w in weights],
            pltpu.SemaphoreType.DMA((2 * HEADS,)),
            pltpu.VMEM((n_blocks, HEADS, MOBA_BLOCK, KQ_DIM), bf16),
            pltpu.VMEM((n_blocks, HEADS, V_ROWS, MOBA_BLOCK), bf16),
            pltpu.VMEM((HEADS, 2 * BIAS_ROWS, KQ_DIM), jnp.float32),
            pltpu.VMEM((POOL_WIDTH, LANES), jnp.float32),
            pltpu.VMEM((HEADS, KQ_DIM, TOKENS), bf16),
            pltpu.VMEM((2, HEADS, TOKENS), jnp.float32),
            pltpu.VMEM((2, HEADS, TOKENS), jnp.float32),
            pltpu.VMEM((HEADS, V_ROWS, TOKENS), jnp.float32),
            pltpu.VMEM((2, HEADS, MOBA_BLOCK, TOKENS), jnp.float32),
            pltpu.VMEM((2, HEADS, MOBA_BLOCK, TOKENS), bf16),
            pltpu.VMEM((ATTN_WIDTH, TOKENS), bf16),
            pltpu.VMEM((POOL_WIDTH, TOKENS), bf16),
            pltpu.VMEM((D_MODEL, TOKENS), jnp.float32),
            pltpu.VMEM((D_MODEL, TOKENS), jnp.float32),
            pltpu.VMEM((D_MODEL, TOKENS), bf16),
            pltpu.VMEM((D_MODEL, TOKENS), jnp.float32),
            pltpu.VMEM((D_MODEL, TOKENS), bf16),
            pltpu.VMEM((8, TOKENS), jnp.float32),
            pltpu.VMEM((D_FF, TOKENS), bf16),
        ],
        compiler_params=pltpu.CompilerParams(
            dimension_semantics=("arbitrary",),
            vmem_limit_bytes=VMEM_LIMIT_BYTES),
        name="moba_pool_layer",
    )(x, jnp.cos(ang), jnp.sin(ang),
      _bcast_cols(norm_mix[0], TOKENS), _bcast_cols(norm_mlp[0], TOKENS),
      _bcast_cols(q_norm[0], TOKENS), _bcast_cols(k_norm[0], TOKENS),
      _bcast_cols(pool_scale[0], TOKENS),
      jnp.swapaxes(w_pool_grp[0], 1, 2).astype(bf16), *weights)
```

```python
import jax
import jax.numpy as jnp
from jax import lax
from jax.experimental import pallas as pl
from jax.experimental.pallas import tpu as pltpu

D_MODEL = 1024
HEADS = 8
HEAD_DIM = 64
ATTN_WIDTH = HEADS * HEAD_DIM
MOBA_BLOCK = 256
MOBA_TOPK = 3
ROPE_THETA = 500000.0
ROPE_HALF = HEAD_DIM // 8
POOL_WINDOWS = (2, 4, 8, 16)
POOL_GROUP_WIDTH = 128
POOL_WIDTH = len(POOL_WINDOWS) * POOL_GROUP_WIDTH
D_FF = 4 * D_MODEL
RMS_EPS = 1e-6

LANES = 128
TOKENS = MOBA_BLOCK
MASKED = -1e30
KQ_DIM = LANES
BIAS_ROWS = 8
V_ROWS = HEAD_DIM + 16
Q_SCALE = HEAD_DIM ** -0.5 * 1.4426950408889634
FF_CHUNK = 1024
OUT_CHUNK = 512
VMEM_LIMIT_BYTES = 61 * 1024 * 1024

_Q0, _K0, _V0, _U0 = 0, ATTN_WIDTH, 2 * ATTN_WIDTH, 3 * ATTN_WIDTH
_GA0 = 3 * ATTN_WIDTH + POOL_WIDTH
_GP0 = _GA0 + D_MODEL
IN_WIDTH = _GP0 + D_MODEL


def _bdot(a, b):
    return jnp.dot(a, b, preferred_element_type=jnp.float32)


def _wdot2(w_ref, rows, act, cols=slice(None)):
    mid = (rows.start + rows.stop) // 2
    return [_bdot(w_ref[rows.start:mid, cols], act), _bdot(w_ref[mid:rows.stop, cols], act)]


def _halves(rows):
    mid = (rows.start + rows.stop) // 2
    return [slice(rows.start, mid), slice(mid, rows.stop)]


def _sigmoid(x):
    return 0.5 * jnp.tanh(0.5 * x) + 0.5


def _rms_scale(t):
    return lax.rsqrt(jnp.mean(t * t, axis=0, keepdims=True) + RMS_EPS)


def _head_norm_rope(t, gain, cos, sin):
    y = t * _rms_scale(t) * gain
    x1 = y[0:ROPE_HALF, :]
    x2 = y[ROPE_HALF:2 * ROPE_HALF, :]
    return jnp.concatenate(
        [x1 * cos - x2 * sin, x2 * cos + x1 * sin, y[2 * ROPE_HALF:, :]], axis=0)


def _shift_tokens(tiles, shift):
    lane = lax.broadcasted_iota(jnp.int32, tiles[0].shape, 1)
    rolled = [pltpu.roll(t, shift, axis=1) for t in tiles]
    out = [rolled[0]]
    for c in range(1, len(tiles)):
        out.append(jnp.where(lane < shift, rolled[c - 1], rolled[c]))
    return out


def _load_weights(pairs, stage, sems):
    side = stage.shape[-1]
    slots_per_row = stage.shape[1]
    n_slots = stage.shape[0] * slots_per_row
    tiles = [(src, dst, k0, n0)
             for src, dst in pairs
             for k0 in range(0, dst.shape[1], side)
             for n0 in range(0, dst.shape[0], side)]

    def slot_of(t):
        slot = t % n_slots
        return slot, stage.at[slot // slots_per_row, slot % slots_per_row]

    def copy(t):
        src, _, k0, n0 = tiles[t]
        slot, buf = slot_of(t)
        return pltpu.make_async_copy(src.at[0, pl.ds(k0, side), pl.ds(n0, side)], buf, sems.at[slot])

    for t in range(min(n_slots, len(tiles))):
        copy(t).start()
    group = 4
    for t0 in range(0, len(tiles), group):
        ts = range(t0, min(t0 + group, len(tiles)))
        for t in ts:
            copy(t).wait()
        for t in ts:
            _, dst, k0, n0 = tiles[t]
            dst[n0:n0 + side, k0:k0 + side] = slot_of(t)[1][...].T.astype(dst.dtype)
        for t in ts:
            if t + n_slots < len(tiles):
                copy(t + n_slots).start()


def _layer_kernel(x_ref, cos_ref, sin_ref, gmix_ref, gmlp_ref, qn_ref, kn_ref, ps_ref, wgrp_ref,
                  win_hbm, wupa_hbm, wupp_hbm, wout_hbm, wff1_hbm, wff2_hbm,
                  o_ref,
                  win_ref, wupa_ref, wupp_ref, wout_ref, wff1_ref, wff2_ref, wsem,
                  kbuf, vbuf, kmbuf, uprev, qbuf, m_ref, alpha_ref, acc_ref, s_ref,
                  o_sc, op_sc, sga_sc, sgp_sc, mg_sc, x1_sc, h2_sc, r2_sc, f_sc):
    f32, bf16 = jnp.float32, jnp.bfloat16
    n = pl.program_id(0)
    n_tiles = pl.num_programs(0) - 1
    n_blocks = kbuf.shape[0]
    i = lax.rem(jnp.minimum(n, n_tiles - 1), n_blocks)

    @pl.when(n == 0)
    def _first_step():
        _load_weights([(win_hbm, win_ref), (wupa_hbm, wupa_ref), (wupp_hbm, wupp_ref),
                       (wout_hbm, wout_ref), (wff1_hbm, wff1_ref), (wff2_hbm, wff2_ref)],
                      s_ref, wsem)
        x1_sc[...] = jnp.zeros_like(x1_sc)
        h2_sc[...] = jnp.zeros_like(h2_sc)
        r2_sc[...] = jnp.zeros_like(r2_sc)

    @pl.when(i == 0)
    def _start_of_sequence():
        kmbuf[...] = jnp.zeros_like(kmbuf)
        uprev[...] = jnp.zeros_like(uprev)

    h2T = h2_sc[...]
    for c in range(D_FF // FF_CHUNK):
        chunk = slice(c * FF_CHUNK, (c + 1) * FF_CHUNK)
        for rows, z in zip(_halves(chunk), _wdot2(wff1_ref, chunk, h2T)):
            f = jnp.maximum(z, 0.0)
            f_sc[rows, :] = (f * f).astype(bf16)

    xT = x_ref[...].T
    hT = (xT * _rms_scale(xT) * gmix_ref[...]).astype(bf16)

    def proj2(lo, width):
        return _wdot2(win_ref, slice(lo, lo + width), hT)

    def head_rows(parts, h):
        per_part = HEADS // 2
        hh = h % per_part
        return parts[h // per_part][hh * HEAD_DIM:(hh + 1) * HEAD_DIM, :]

    cos = cos_ref[...]
    sin = sin_ref[...]

    k_parts = proj2(_K0, ATTN_WIDTH)
    q_parts = proj2(_Q0, ATTN_WIDTH)
    v_parts = proj2(_V0, ATTN_WIDTH)
    u_parts = proj2(_U0, POOL_WIDTH)
    all_rows = slice(0, D_MODEL)
    for rows, ga, gp in zip(_halves(all_rows), proj2(_GA0, D_MODEL), proj2(_GP0, D_MODEL)):
        sga_sc[rows, :] = _sigmoid(ga)
        sgp_sc[rows, :] = _sigmoid(gp)
    pad_row = lax.broadcasted_iota(jnp.int32, (KQ_DIM - HEAD_DIM, TOKENS), 0)
    own_block = jnp.where(pad_row == i, 1.0, 0.0)
    ones_row = jnp.where(
        lax.broadcasted_iota(jnp.int32, (V_ROWS - HEAD_DIM, TOKENS), 0) == 0, 1.0, 0.0)
    for h in range(HEADS):
        k_h = _head_norm_rope(head_rows(k_parts, h), kn_ref[...], cos, sin)
        k_tok = jnp.concatenate([k_h, own_block], axis=0).T
        kbuf[i, h] = k_tok.astype(bf16)
        kmean = jnp.sum(k_tok, axis=0, keepdims=True) * (1.0 / MOBA_BLOCK)
        kmbuf[h, pl.ds(i, 1), :] = kmean
        vbuf[i, h, :HEAD_DIM, :] = head_rows(v_parts, h).astype(bf16)
        vbuf[i, h, HEAD_DIM:, :] = ones_row.astype(bf16)

    blk = lax.broadcasted_iota(jnp.int32, (BIAS_ROWS, TOKENS), 0)
    fully_past = blk < i
    for h in range(HEADS):
        q_h = _head_norm_rope(head_rows(q_parts, h), qn_ref[...], cos, sin)
        q_h = (q_h * Q_SCALE).astype(bf16)
        gate = _bdot(kmbuf[h, :, :HEAD_DIM].astype(bf16), q_h)[:BIAS_ROWS, :]
        g = jnp.where(fully_past, gate, -jnp.inf)
        rank = jnp.zeros((BIAS_ROWS, TOKENS), jnp.int32)
        for jp in range(n_blocks - 1):
            g_jp = g[jp:jp + 1, :]
            ahead = jnp.logical_or(g_jp > g, jnp.logical_and(g_jp == g, blk > jp))
            rank = rank + ahead.astype(jnp.int32)
        chosen = jnp.logical_and(rank < MOBA_TOPK, fully_past)
        bias = jnp.where(jnp.logical_or(chosen, blk == i), 0.0, MASKED)
        zeros = jnp.zeros((KQ_DIM - HEAD_DIM - BIAS_ROWS, TOKENS), f32)
        qbuf[h, :HEAD_DIM, :] = q_h
        qbuf[h, HEAD_DIM:, :] = jnp.concatenate([bias, zeros], axis=0).astype(bf16)

    tok = i * TOKENS + lax.broadcasted_iota(jnp.int32, (1, TOKENS), 1)
    groups_per_part = len(POOL_WINDOWS) // 2
    for gi, w in enumerate(POOL_WINDOWS):
        rows = slice(gi * POOL_GROUP_WIDTH, (gi + 1) * POOL_GROUP_WIDTH)
        gg = gi % groups_per_part
        u_g = u_parts[gi // groups_per_part][gg * POOL_GROUP_WIDTH:(gg + 1) * POOL_GROUP_WIDTH, :]
        tiles = [uprev[rows, :]] + [u_g[:, c * LANES:(c + 1) * LANES]
                                    for c in range(TOKENS // LANES)]
        uprev[rows, :] = tiles[-1]
        shift = 1
        while shift < w:
            shifted = _shift_tokens(tiles, shift)
            tiles = [a + b for a, b in zip(tiles, shifted)]
            shift *= 2
        win_sum = jnp.concatenate(tiles[1:], axis=1)
        inv_cnt = 1.0 / jnp.minimum(tok + 1, w).astype(f32)
        pooled = win_sum * inv_cnt - u_g
        y = _bdot(wgrp_ref[gi], pooled.astype(bf16)) * ps_ref[rows, :]
        op_sc[rows, :] = y.astype(bf16)

    key_pos = lax.broadcasted_iota(jnp.int32, (MOBA_BLOCK, TOKENS), 0)
    qry_pos = lax.broadcasted_iota(jnp.int32, (MOBA_BLOCK, TOKENS), 1)
    causal = key_pos <= qry_pos

    def score_pass(j, slot, own=False):
        for h in range(HEADS):
            s = _bdot(kbuf[j, h], qbuf[h])
            if own:
                s = jnp.where(causal, s, MASKED)
            s_ref[slot, h] = s
            m_new = jnp.max(s, axis=0, keepdims=True)
            if not own:
                m_old = m_ref[1 - slot, h:h + 1, :]
                m_new = jnp.maximum(m_old, m_new)
                alpha_ref[slot, h:h + 1, :] = jnp.exp2(m_old - m_new)
            m_ref[slot, h:h + 1, :] = m_new

    def value_pass(j, slot, own=False):
        for h in range(HEADS):
            p = jnp.exp2(s_ref[slot, h] - m_ref[slot, h:h + 1, :]).astype(bf16)
            pv = _bdot(vbuf[j, h], p)
            acc_ref[h] = pv if own else acc_ref[h] * alpha_ref[slot, h:h + 1, :] + pv

    score_pass(i, 1, own=True)

    fT = f_sc[...]
    r2_prev = r2_sc[0:1, :]
    for c in range(D_MODEL // OUT_CHUNK):
        chunk = slice(c * OUT_CHUNK, (c + 1) * OUT_CHUNK)
        for rows, y in zip(_halves(chunk), _wdot2(wff2_ref, chunk, fT)):
            o_ref[:, rows] = (x1_sc[rows, :] + y * r2_prev).T

    value_pass(i, 1, own=True)

    n_past = jnp.where(n == n_tiles, 0, i)

    def past_pair(jj, carry):
        score_pass(2 * jj, 0)
        score_pass(2 * jj + 1, 1)
        value_pass(2 * jj, 0)
        value_pass(2 * jj + 1, 1)
        return carry

    lax.fori_loop(0, n_past // 2, past_pair, 0)

    @pl.when(n_past % 2 == 1)
    def _last_past_block():
        score_pass(n_past - 1, 0)
        value_pass(n_past - 1, 0)

    for h in range(HEADS):
        o_h = acc_ref[h, :HEAD_DIM, :] * (1.0 / acc_ref[h, HEAD_DIM:HEAD_DIM + 1, :])
        o_sc[h * HEAD_DIM:(h + 1) * HEAD_DIM, :] = o_h.astype(bf16)

    upa_parts = _wdot2(wupa_ref, all_rows, o_sc[...])
    upp_parts = _wdot2(wupp_ref, all_rows, op_sc[...])
    for rows, upa, upp in zip(_halves(all_rows), upa_parts, upp_parts):
        mg_sc[rows, :] = (sga_sc[rows, :] * upa + sgp_sc[rows, :] * upp).astype(bf16)
    sumsq = jnp.zeros((1, TOKENS), f32)
    for rows, d in zip(_halves(all_rows), _wdot2(wout_ref, all_rows, mg_sc[...])):
        x1 = xT[rows, :] + d
        x1_sc[rows, :] = x1
        h2_sc[rows, :] = (x1 * gmlp_ref[rows, :]).astype(bf16)
        sumsq = sumsq + jnp.sum(x1 * x1, axis=0, keepdims=True)
    r = lax.rsqrt(sumsq * (1.0 / D_MODEL) + RMS_EPS)
    r2_sc[...] = jnp.broadcast_to(r * r, r2_sc.shape)


def _bcast_cols(v, n):
    return jnp.broadcast_to(v.astype(jnp.float32)[:, None], (v.shape[0], n))


@jax.jit
def kernel(x, norm_mix, w_in, q_norm, k_norm, w_pool_grp, pool_scale, w_up_attn, w_up_pool,
           w_out, norm_mlp, w_ff1, w_ff2):
    batch, seq, d_model = x.shape
    assert d_model == D_MODEL and seq % TOKENS == 0 and w_in.shape[0] == 1
    n_blocks = seq // TOKENS
    assert n_blocks <= BIAS_ROWS
    bf16 = jnp.bfloat16

    half = jnp.arange(ROPE_HALF, dtype=jnp.float32)
    inv_freq = ROPE_THETA ** (-half / ROPE_HALF)
    ang = inv_freq[:, None] * jnp.arange(seq).astype(jnp.float32)[None, :]

    n_tiles = batch * n_blocks

    def tile(n):
        t = jnp.clip(n, 0, n_tiles - 1)
        return t // n_blocks, t % n_blocks

    whole = pl.BlockSpec(memory_space=pltpu.VMEM)
    in_hbm = pl.BlockSpec(memory_space=pl.ANY)
    weights = (w_in, w_up_attn, w_up_pool, w_out, w_ff1, w_ff2)
    x_spec = pl.BlockSpec((None, TOKENS, D_MODEL), lambda n: (*tile(n), 0))
    out_spec = pl.BlockSpec((None, TOKENS, D_MODEL), lambda n: (*tile(n - 1), 0))
    rope_spec = pl.BlockSpec((ROPE_HALF, TOKENS), lambda n: (0, tile(n)[1]))

    return pl.pallas_call(
        _layer_kernel,
        grid=(n_tiles + 1,),
        in_specs=[x_spec, rope_spec, rope_spec] + [whole] * 6 + [in_hbm] * len(weights),
        out_specs=out_spec,
        out_shape=jax.ShapeDtypeStruct(x.shape, x.dtype),
        scratch_shapes=[
            *[pltpu.VMEM((w.shape[2], w.shape[1]), bf16) for w in weights],
            pltpu.SemaphoreType.DMA((2 * HEADS,)),
            pltpu.VMEM((n_blocks, HEADS, MOBA_BLOCK, KQ_DIM), bf16),
            pltpu.VMEM((n_blocks, HEADS, V_ROWS, MOBA_BLOCK), bf16),
            pltpu.VMEM((HEADS, 2 * BIAS_ROWS, KQ_DIM), jnp.float32),
            pltpu.VMEM((POOL_WIDTH, LANES), jnp.float32),
            pltpu.VMEM((HEADS, KQ_DIM, TOKENS), bf16),
            pltpu.VMEM((2, HEADS, TOKENS), jnp.float32),
            pltpu.VMEM((2, HEADS, TOKENS), jnp.float32),
            pltpu.VMEM((HEADS, V_ROWS, TOKENS), jnp.float32),
            pltpu.VMEM((2, HEADS, MOBA_BLOCK, TOKENS), jnp.float32),
            pltpu.VMEM((ATTN_WIDTH, TOKENS), bf16),
            pltpu.VMEM((POOL_WIDTH, TOKENS), bf16),
            pltpu.VMEM((D_MODEL, TOKENS), jnp.float32),
            pltpu.VMEM((D_MODEL, TOKENS), jnp.float32),
            pltpu.VMEM((D_MODEL, TOKENS), bf16),
            pltpu.VMEM((D_MODEL, TOKENS), jnp.float32),
            pltpu.VMEM((D_MODEL, TOKENS), bf16),
            pltpu.VMEM((8, TOKENS), jnp.float32),
            pltpu.VMEM((D_FF, TOKENS), bf16),
        ],
        compiler_params=pltpu.CompilerParams(
            dimension_semantics=("arbitrary",),
            vmem_limit_bytes=VMEM_LIMIT_BYTES),
        name="moba_pool_layer",
    )(x, jnp.cos(ang), jnp.sin(ang),
      _bcast_cols(norm_mix[0], TOKENS), _bcast_cols(norm_mlp[0], TOKENS),
      _bcast_cols(q_norm[0], TOKENS), _bcast_cols(k_norm[0], TOKENS),
      _bcast_cols(pool_scale[0], TOKENS),
      jnp.swapaxes(w_pool_grp[0], 1, 2).astype(bf16), *weights)
```

```python
import jax
import jax.numpy as jnp
from jax import lax
from jax.experimental import pallas as pl
from jax.experimental.pallas import tpu as pltpu

D_MODEL = 1024
HEADS = 8
HEAD_DIM = 64
ATTN_WIDTH = HEADS * HEAD_DIM
MOBA_BLOCK = 256
MOBA_TOPK = 3
ROPE_THETA = 500000.0
ROPE_HALF = HEAD_DIM // 8
POOL_WINDOWS = (2, 4, 8, 16)
POOL_GROUP_WIDTH = 128
POOL_WIDTH = len(POOL_WINDOWS) * POOL_GROUP_WIDTH
D_FF = 4 * D_MODEL
RMS_EPS = 1e-6

LANES = 128
TOKENS = MOBA_BLOCK
MASKED = -1e30
KQ_DIM = LANES
BIAS_ROWS = 8
V_ROWS = HEAD_DIM + 16
Q_SCALE = HEAD_DIM ** -0.5 * 1.4426950408889634
FF_CHUNK = 1024
OUT_CHUNK = 512
VMEM_LIMIT_BYTES = 61 * 1024 * 1024

_Q0, _K0, _V0, _U0 = 0, ATTN_WIDTH, 2 * ATTN_WIDTH, 3 * ATTN_WIDTH
_GA0 = 3 * ATTN_WIDTH + POOL_WIDTH
_GP0 = _GA0 + D_MODEL
IN_WIDTH = _GP0 + D_MODEL


def _bdot(a, b):
    return jnp.dot(a, b, preferred_element_type=jnp.float32)


def _wdot2(w_ref, rows, act, cols=slice(None)):
    mid = (rows.start + rows.stop) // 2
    return [_bdot(w_ref[rows.start:mid, cols], act), _bdot(w_ref[mid:rows.stop, cols], act)]


def _halves(rows):
    mid = (rows.start + rows.stop) // 2
    return [slice(rows.start, mid), slice(mid, rows.stop)]


def _sigmoid(x):
    return 0.5 * jnp.tanh(0.5 * x) + 0.5


def _rms_scale(t):
    return lax.rsqrt(jnp.mean(t * t, axis=0, keepdims=True) + RMS_EPS)


def _head_norm_rope(t, gain, cos, sin):
    y = t * _rms_scale(t) * gain
    x1 = y[0:ROPE_HALF, :]
    x2 = y[ROPE_HALF:2 * ROPE_HALF, :]
    return jnp.concatenate(
        [x1 * cos - x2 * sin, x2 * cos + x1 * sin, y[2 * ROPE_HALF:, :]], axis=0)


def _shift_tokens(tiles, shift):
    lane = lax.broadcasted_iota(jnp.int32, tiles[0].shape, 1)
    rolled = [pltpu.roll(t, shift, axis=1) for t in tiles]
    out = [rolled[0]]
    for c in range(1, len(tiles)):
        out.append(jnp.where(lane < shift, rolled[c - 1], rolled[c]))
    return out


def _load_weights(pairs, stage, sems):
    side = stage.shape[-1]
    slots_per_row = stage.shape[1]
    n_slots = stage.shape[0] * slots_per_row
    tiles = [(src, dst, k0, n0)
             for src, dst in pairs
             for k0 in range(0, dst.shape[1], side)
             for n0 in range(0, dst.shape[0], side)]

    def slot_of(t):
        slot = t % n_slots
        return slot, stage.at[slot // slots_per_row, slot % slots_per_row]

    def copy(t):
        src, _, k0, n0 = tiles[t]
        slot, buf = slot_of(t)
        return pltpu.make_async_copy(src.at[0, pl.ds(k0, side), pl.ds(n0, side)], buf, sems.at[slot])

    for t in range(min(n_slots, len(tiles))):
        copy(t).start()
    group = 4
    for t0 in range(0, len(tiles), group):
        ts = range(t0, min(t0 + group, len(tiles)))
        for t in ts:
            copy(t).wait()
        for t in ts:
            _, dst, k0, n0 = tiles[t]
            dst[n0:n0 + side, k0:k0 + side] = slot_of(t)[1][...].T.astype(dst.dtype)
        for t in ts:
            if t + n_slots < len(tiles):
                copy(t + n_slots).start()


def _layer_kernel(x_ref, cos_ref, sin_ref, gmix_ref, gmlp_ref, qn_ref, kn_ref, ps_ref, wgrp_ref,
                  win_hbm, wupa_hbm, wupp_hbm, wout_hbm, wff1_hbm, wff2_hbm,
                  o_ref,
                  win_ref, wupa_ref, wupp_ref, wout_ref, wff1_ref, wff2_ref, wsem,
                  kbuf, vbuf, kmbuf, uprev, qbuf, m_ref, alpha_ref, acc_ref, s_ref,
                  o_sc, op_sc, sga_sc, sgp_sc, mg_sc, x1_sc, h2_sc, r2_sc, f_sc):
    f32, bf16 = jnp.float32, jnp.bfloat16
    n = pl.program_id(0)
    n_tiles = pl.num_programs(0) - 1
    n_blocks = kbuf.shape[0]
    i = lax.rem(jnp.minimum(n, n_tiles - 1), n_blocks)

    @pl.when(n == 0)
    def _first_step():
        _load_weights([(win_hbm, win_ref), (wupa_hbm, wupa_ref), (wupp_hbm, wupp_ref),
                       (wout_hbm, wout_ref), (wff1_hbm, wff1_ref), (wff2_hbm, wff2_ref)],
                      s_ref, wsem)
        x1_sc[...] = jnp.zeros_like(x1_sc)
        h2_sc[...] = jnp.zeros_like(h2_sc)
        r2_sc[...] = jnp.zeros_like(r2_sc)

    @pl.when(i == 0)
    def _start_of_sequence():
        kmbuf[...] = jnp.zeros_like(kmbuf)
        uprev[...] = jnp.zeros_like(uprev)

    h2T = h2_sc[...]
    for c in range(D_FF // FF_CHUNK):
        chunk = slice(c * FF_CHUNK, (c + 1) * FF_CHUNK)
        for rows, z in zip(_halves(chunk), _wdot2(wff1_ref, chunk, h2T)):
            f = jnp.maximum(z, 0.0)
            f_sc[rows, :] = (f * f).astype(bf16)

    xT = x_ref[...].T
    hT = (xT * _rms_scale(xT) * gmix_ref[...]).astype(bf16)

    def proj2(lo, width):
        return _wdot2(win_ref, slice(lo, lo + width), hT)

    def head_rows(parts, h):
        per_part = HEADS // 2
        hh = h % per_part
        return parts[h // per_part][hh * HEAD_DIM:(hh + 1) * HEAD_DIM, :]

    cos = cos_ref[...]
    sin = sin_ref[...]

    k_parts = proj2(_K0, ATTN_WIDTH)
    q_parts = proj2(_Q0, ATTN_WIDTH)
    v_parts = proj2(_V0, ATTN_WIDTH)
    u_parts = proj2(_U0, POOL_WIDTH)
    all_rows = slice(0, D_MODEL)
    for rows, ga, gp in zip(_halves(all_rows), proj2(_GA0, D_MODEL), proj2(_GP0, D_MODEL)):
        sga_sc[rows, :] = _sigmoid(ga)
        sgp_sc[rows, :] = _sigmoid(gp)
    pad_row = lax.broadcasted_iota(jnp.int32, (KQ_DIM - HEAD_DIM, TOKENS), 0)
    own_block = jnp.where(pad_row == i, 1.0, 0.0)
    ones_row = jnp.where(
        lax.broadcasted_iota(jnp.int32, (V_ROWS - HEAD_DIM, TOKENS), 0) == 0, 1.0, 0.0)
    for h in range(HEADS):
        k_h = _head_norm_rope(head_rows(k_parts, h), kn_ref[...], cos, sin)
        k_tok = jnp.concatenate([k_h, own_block], axis=0).T
        kbuf[i, h] = k_tok.astype(bf16)
        kmean = jnp.sum(k_tok, axis=0, keepdims=True) * (1.0 / MOBA_BLOCK)
        kmbuf[h, pl.ds(i, 1), :] = kmean
        vbuf[i, h, :HEAD_DIM, :] = head_rows(v_parts, h).astype(bf16)
        vbuf[i, h, HEAD_DIM:, :] = ones_row.astype(bf16)

    blk = lax.broadcasted_iota(jnp.int32, (BIAS_ROWS, TOKENS), 0)
    fully_past = blk < i
    for h in range(HEADS):
        q_h = _head_norm_rope(head_rows(q_parts, h), qn_ref[...], cos, sin)
        q_h = (q_h * Q_SCALE).astype(bf16)
        gate = _bdot(kmbuf[h, :, :HEAD_DIM].astype(bf16), q_h)[:BIAS_ROWS, :]
        g = jnp.where(fully_past, gate, -jnp.inf)
        rank = jnp.zeros((BIAS_ROWS, TOKENS), jnp.int32)
        for jp in range(n_blocks - 1):
            g_jp = g[jp:jp + 1, :]
            ahead = jnp.logical_or(g_jp > g, jnp.logical_and(g_jp == g, blk > jp))
            rank = rank + ahead.astype(jnp.int32)
        chosen = jnp.logical_and(rank < MOBA_TOPK, fully_past)
        bias = jnp.where(jnp.logical_or(chosen, blk == i), 0.0, MASKED)
        zeros = jnp.zeros((KQ_DIM - HEAD_DIM - BIAS_ROWS, TOKENS), f32)
        qbuf[h, :HEAD_DIM, :] = q_h
        qbuf[h, HEAD_DIM:, :] = jnp.concatenate([bias, zeros], axis=0).astype(bf16)

    tok = i * TOKENS + lax.broadcasted_iota(jnp.int32, (1, TOKENS), 1)
    groups_per_part = len(POOL_WINDOWS) // 2
    for gi, w in enumerate(POOL_WINDOWS):
        rows = slice(gi * POOL_GROUP_WIDTH, (gi + 1) * POOL_GROUP_WIDTH)
        gg = gi % groups_per_part
        u_g = u_parts[gi // groups_per_part][gg * POOL_GROUP_WIDTH:(gg + 1) * POOL_GROUP_WIDTH, :]
        tiles = [uprev[rows, :]] + [u_g[:, c * LANES:(c + 1) * LANES]
                                    for c in range(TOKENS // LANES)]
        uprev[rows, :] = tiles[-1]
        shift = 1
        while shift < w:
            shifted = _shift_tokens(tiles, shift)
            tiles = [a + b for a, b in zip(tiles, shifted)]
            shift *= 2
        win_sum = jnp.concatenate(tiles[1:], axis=1)
        inv_cnt = 1.0 / jnp.minimum(tok + 1, w).astype(f32)
        pooled = win_sum * inv_cnt - u_g
        y = _bdot(wgrp_ref[gi], pooled.astype(bf16)) * ps_ref[rows, :]
        op_sc[rows, :] = y.astype(bf16)

    key_pos = lax.broadcasted_iota(jnp.int32, (MOBA_BLOCK, TOKENS), 0)
    qry_pos = lax.broadcasted_iota(jnp.int32, (MOBA_BLOCK, TOKENS), 1)
    causal = key_pos <= qry_pos

    def score_head(j, slot, h, own=False):
        s = _bdot(kbuf[j, h], qbuf[h])
        if own:
            s = jnp.where(causal, s, MASKED)
        s_ref[slot, h] = s
        m_new = jnp.max(s, axis=0, keepdims=True)
        if not own:
            m_old = m_ref[1 - slot, h:h + 1, :]
            m_new = jnp.maximum(m_old, m_new)
            alpha_ref[slot, h:h + 1, :] = jnp.exp2(m_old - m_new)
        m_ref[slot, h:h + 1, :] = m_new

    def value_head(j, slot, h, own=False):
        p = jnp.exp2(s_ref[slot, h] - m_ref[slot, h:h + 1, :]).astype(bf16)
        pv = _bdot(vbuf[j, h], p)
        acc_ref[h] = pv if own else acc_ref[h] * alpha_ref[slot, h:h + 1, :] + pv

    def score_pass(j, slot, own=False):
        for h in range(HEADS):
            score_head(j, slot, h, own)

    def value_pass(j, slot, own=False):
        for h in range(HEADS):
            value_head(j, slot, h, own)

    score_pass(i, 1, own=True)

    fT = f_sc[...]
    r2_prev = r2_sc[0:1, :]
    for c in range(D_MODEL // OUT_CHUNK):
        chunk = slice(c * OUT_CHUNK, (c + 1) * OUT_CHUNK)
        for rows, y in zip(_halves(chunk), _wdot2(wff2_ref, chunk, fT)):
            o_ref[:, rows] = (x1_sc[rows, :] + y * r2_prev).T

    value_pass(i, 1, own=True)

    n_past = jnp.where(n == n_tiles, 0, i)

    def past_pair(jj, carry):
        score_pass(2 * jj, 0)
        for h in range(HEADS):
            score_head(2 * jj + 1, 1, h)
            value_head(2 * jj, 0, h)
        value_pass(2 * jj + 1, 1)
        return carry

    lax.fori_loop(0, n_past // 2, past_pair, 0)

    @pl.when(n_past % 2 == 1)
    def _last_past_block():
        score_pass(n_past - 1, 0)
        value_pass(n_past - 1, 0)

    for h in range(HEADS):
        o_h = acc_ref[h, :HEAD_DIM, :] * (1.0 / acc_ref[h, HEAD_DIM:HEAD_DIM + 1, :])
        o_sc[h * HEAD_DIM:(h + 1) * HEAD_DIM, :] = o_h.astype(bf16)

    upa_parts = _wdot2(wupa_ref, all_rows, o_sc[...])
    upp_parts = _wdot2(wupp_ref, all_rows, op_sc[...])
    for rows, upa, upp in zip(_halves(all_rows), upa_parts, upp_parts):
        mg_sc[rows, :] = (sga_sc[rows, :] * upa + sgp_sc[rows, :] * upp).astype(bf16)
    sumsq = jnp.zeros((1, TOKENS), f32)
    for rows, d in zip(_halves(all_rows), _wdot2(wout_ref, all_rows, mg_sc[...])):
        x1 = xT[rows, :] + d
        x1_sc[rows, :] = x1
        h2_sc[rows, :] = (x1 * gmlp_ref[rows, :]).astype(bf16)
        sumsq = sumsq + jnp.sum(x1 * x1, axis=0, keepdims=True)
    r = lax.rsqrt(sumsq * (1.0 / D_MODEL) + RMS_EPS)
    r2_sc[...] = jnp.broadcast_to(r * r, r2_sc.shape)


def _bcast_cols(v, n):
    return jnp.broadcast_to(v.astype(jnp.float32)[:, None], (v.shape[0], n))


@jax.jit
def kernel(x, norm_mix, w_in, q_norm, k_norm, w_pool_grp, pool_scale, w_up_attn, w_up_pool,
           w_out, norm_mlp, w_ff1, w_ff2):
    batch, seq, d_model = x.shape
    assert d_model == D_MODEL and seq % TOKENS == 0 and w_in.shape[0] == 1
    n_blocks = seq // TOKENS
    assert n_blocks <= BIAS_ROWS
    bf16 = jnp.bfloat16

    half = jnp.arange(ROPE_HALF, dtype=jnp.float32)
    inv_freq = ROPE_THETA ** (-half / ROPE_HALF)
    ang = inv_freq[:, None] * jnp.arange(seq).astype(jnp.float32)[None, :]

    n_tiles = batch * n_blocks

    def tile(n):
        t = jnp.clip(n, 0, n_tiles - 1)
        return t // n_blocks, t % n_blocks

    whole = pl.BlockSpec(memory_space=pltpu.VMEM)
    in_hbm = pl.BlockSpec(memory_space=pl.ANY)
    weights = (w_in, w_up_attn, w_up_pool, w_out, w_ff1, w_ff2)
    x_spec = pl.BlockSpec((None, TOKENS, D_MODEL), lambda n: (*tile(n), 0))
    out_spec = pl.BlockSpec((None, TOKENS, D_MODEL), lambda n: (*tile(n - 1), 0))
    rope_spec = pl.BlockSpec((ROPE_HALF, TOKENS), lambda n: (0, tile(n)[1]))

    return pl.pallas_call(
        _layer_kernel,
        grid=(n_tiles + 1,),
        in_specs=[x_spec, rope_spec, rope_spec] + [whole] * 6 + [in_hbm] * len(weights),
        out_specs=out_spec,
        out_shape=jax.ShapeDtypeStruct(x.shape, x.dtype),
        scratch_shapes=[
            *[pltpu.VMEM((w.shape[2], w.shape[1]), bf16) for w in weights],
            pltpu.SemaphoreType.DMA((2 * HEADS,)),
            pltpu.VMEM((n_blocks, HEADS, MOBA_BLOCK, KQ_DIM), bf16),
            pltpu.VMEM((n_blocks, HEADS, V_ROWS, MOBA_BLOCK), bf16),
            pltpu.VMEM((HEADS, 2 * BIAS_ROWS, KQ_DIM), jnp.float32),
            pltpu.VMEM((POOL_WIDTH, LANES), jnp.float32),
            pltpu.VMEM((HEADS, KQ_DIM, TOKENS), bf16),
            pltpu.VMEM((2, HEADS, TOKENS), jnp.float32),
            pltpu.VMEM((2, HEADS, TOKENS), jnp.float32),
            pltpu.VMEM((HEADS, V_ROWS, TOKENS), jnp.float32),
            pltpu.VMEM((2, HEADS, MOBA_BLOCK, TOKENS), jnp.float32),
            pltpu.VMEM((ATTN_WIDTH, TOKENS), bf16),
            pltpu.VMEM((POOL_WIDTH, TOKENS), bf16),
            pltpu.VMEM((D_MODEL, TOKENS), jnp.float32),
            pltpu.VMEM((D_MODEL, TOKENS), jnp.float32),
            pltpu.VMEM((D_MODEL, TOKENS), bf16),
            pltpu.VMEM((D_MODEL, TOKENS), jnp.float32),
            pltpu.VMEM((D_MODEL, TOKENS), bf16),
            pltpu.VMEM((8, TOKENS), jnp.float32),
            pltpu.VMEM((D_FF, TOKENS), bf16),
        ],
        compiler_params=pltpu.CompilerParams(
            dimension_semantics=("arbitrary",),
            vmem_limit_bytes=VMEM_LIMIT_BYTES),
        name="moba_pool_layer",
    )(x, jnp.cos(ang), jnp.sin(ang),
      _bcast_cols(norm_mix[0], TOKENS), _bcast_cols(norm_mlp[0], TOKENS),
      _bcast_cols(q_norm[0], TOKENS), _bcast_cols(k_norm[0], TOKENS),
      _bcast_cols(pool_scale[0], TOKENS),
      jnp.swapaxes(w_pool_grp[0], 1, 2).astype(bf16), *weights)
```

```python
import jax
import jax.numpy as jnp
from jax import lax
from jax.experimental import pallas as pl
from jax.experimental.pallas import tpu as pltpu

D_MODEL = 1024
HEADS = 8
HEAD_DIM = 64
ATTN_WIDTH = HEADS * HEAD_DIM
MOBA_BLOCK = 256
MOBA_TOPK = 3
ROPE_THETA = 500000.0
ROPE_HALF = HEAD_DIM // 8
POOL_WINDOWS = (2, 4, 8, 16)
POOL_GROUP_WIDTH = 128
POOL_WIDTH = len(POOL_WINDOWS) * POOL_GROUP_WIDTH
D_FF = 4 * D_MODEL
RMS_EPS = 1e-6

LANES = 128
TOKENS = MOBA_BLOCK
MASKED = -1e30
KQ_DIM = LANES
BIAS_ROWS = 8
V_ROWS = HEAD_DIM + 16
Q_SCALE = HEAD_DIM ** -0.5 * 1.4426950408889634
FF_CHUNK = 1024
OUT_CHUNK = 512
VMEM_LIMIT_BYTES = 61 * 1024 * 1024

_Q0, _K0, _V0, _U0 = 0, ATTN_WIDTH, 2 * ATTN_WIDTH, 3 * ATTN_WIDTH
_GA0 = 3 * ATTN_WIDTH + POOL_WIDTH
_GP0 = _GA0 + D_MODEL
IN_WIDTH = _GP0 + D_MODEL


def _bdot(a, b):
    return jnp.dot(a, b, preferred_element_type=jnp.float32)


def _wdot2(w_ref, rows, act, cols=slice(None)):
    mid = (rows.start + rows.stop) // 2
    return [_bdot(w_ref[rows.start:mid, cols], act), _bdot(w_ref[mid:rows.stop, cols], act)]


def _halves(rows):
    mid = (rows.start + rows.stop) // 2
    return [slice(rows.start, mid), slice(mid, rows.stop)]


def _sigmoid(x):
    return 0.5 * jnp.tanh(0.5 * x) + 0.5


def _rms_scale(t):
    return lax.rsqrt(jnp.mean(t * t, axis=0, keepdims=True) + RMS_EPS)


def _head_norm_rope(t, gain, cos, sin):
    y = t * _rms_scale(t) * gain
    x1 = y[0:ROPE_HALF, :]
    x2 = y[ROPE_HALF:2 * ROPE_HALF, :]
    return jnp.concatenate(
        [x1 * cos - x2 * sin, x2 * cos + x1 * sin, y[2 * ROPE_HALF:, :]], axis=0)


def _shift_tokens(tiles, shift):
    lane = lax.broadcasted_iota(jnp.int32, tiles[0].shape, 1)
    rolled = [pltpu.roll(t, shift, axis=1) for t in tiles]
    out = [rolled[0]]
    for c in range(1, len(tiles)):
        out.append(jnp.where(lane < shift, rolled[c - 1], rolled[c]))
    return out


def _load_weights(pairs, stage, sems):
    side = stage.shape[-1]
    slots_per_row = stage.shape[1]
    n_slots = stage.shape[0] * slots_per_row
    tiles = [(src, dst, k0, n0)
             for src, dst in pairs
             for k0 in range(0, dst.shape[1], side)
             for n0 in range(0, dst.shape[0], side)]

    def slot_of(t):
        slot = t % n_slots
        return slot, stage.at[slot // slots_per_row, slot % slots_per_row]

    def copy(t):
        src, _, k0, n0 = tiles[t]
        slot, buf = slot_of(t)
        return pltpu.make_async_copy(src.at[0, pl.ds(k0, side), pl.ds(n0, side)], buf, sems.at[slot])

    for t in range(min(n_slots, len(tiles))):
        copy(t).start()
    group = 4
    for t0 in range(0, len(tiles), group):
        ts = range(t0, min(t0 + group, len(tiles)))
        for t in ts:
            copy(t).wait()
        for t in ts:
            _, dst, k0, n0 = tiles[t]
            dst[n0:n0 + side, k0:k0 + side] = slot_of(t)[1][...].T.astype(dst.dtype)
        for t in ts:
            if t + n_slots < len(tiles):
                copy(t + n_slots).start()


def _layer_kernel(x_ref, cos_ref, sin_ref, gmix_ref, gmlp_ref, qn_ref, kn_ref, ps_ref, wgrp_ref,
                  win_hbm, wupa_hbm, wupp_hbm, wout_hbm, wff1_hbm, wff2_hbm,
                  o_ref,
                  win_ref, wupa_ref, wupp_ref, wout_ref, wff1_ref, wff2_ref, wsem,
                  kbuf, vbuf, kmbuf, uprev, qbuf, m_ref, alpha_ref, acc_ref, s_ref,
                  o_sc, op_sc, sga_sc, sgp_sc, mg_sc, x1_sc, h2_sc, r2_sc, f_sc):
    f32, bf16 = jnp.float32, jnp.bfloat16
    n = pl.program_id(0)
    n_tiles = pl.num_programs(0) - 1
    n_blocks = kbuf.shape[0]
    i = lax.rem(jnp.minimum(n, n_tiles - 1), n_blocks)

    @pl.when(n == 0)
    def _first_step():
        _load_weights([(win_hbm, win_ref), (wupa_hbm, wupa_ref), (wupp_hbm, wupp_ref),
                       (wout_hbm, wout_ref), (wff1_hbm, wff1_ref), (wff2_hbm, wff2_ref)],
                      s_ref, wsem)
        x1_sc[...] = jnp.zeros_like(x1_sc)
        h2_sc[...] = jnp.zeros_like(h2_sc)
        r2_sc[...] = jnp.zeros_like(r2_sc)

    @pl.when(i == 0)
    def _start_of_sequence():
        kmbuf[...] = jnp.zeros_like(kmbuf)
        uprev[...] = jnp.zeros_like(uprev)

    h2T = h2_sc[...]
    for c in range(D_FF // FF_CHUNK):
        chunk = slice(c * FF_CHUNK, (c + 1) * FF_CHUNK)
        for rows, z in zip(_halves(chunk), _wdot2(wff1_ref, chunk, h2T)):
            f = jnp.maximum(z, 0.0)
            f_sc[rows, :] = (f * f).astype(bf16)

    xT = x_ref[...].T
    hT = (xT * _rms_scale(xT) * gmix_ref[...]).astype(bf16)

    def proj2(lo, width):
        return _wdot2(win_ref, slice(lo, lo + width), hT)

    def head_rows(parts, h):
        per_part = HEADS // 2
        hh = h % per_part
        return parts[h // per_part][hh * HEAD_DIM:(hh + 1) * HEAD_DIM, :]

    cos = cos_ref[...]
    sin = sin_ref[...]

    k_parts = proj2(_K0, ATTN_WIDTH)
    q_parts = proj2(_Q0, ATTN_WIDTH)
    v_parts = proj2(_V0, ATTN_WIDTH)
    u_parts = proj2(_U0, POOL_WIDTH)
    all_rows = slice(0, D_MODEL)
    for rows, ga, gp in zip(_halves(all_rows), proj2(_GA0, D_MODEL), proj2(_GP0, D_MODEL)):
        sga_sc[rows, :] = _sigmoid(ga)
        sgp_sc[rows, :] = _sigmoid(gp)
    pad_row = lax.broadcasted_iota(jnp.int32, (KQ_DIM - HEAD_DIM, TOKENS), 0)
    own_block = jnp.where(pad_row == i, 1.0, 0.0)
    ones_row = jnp.where(
        lax.broadcasted_iota(jnp.int32, (V_ROWS - HEAD_DIM, TOKENS), 0) == 0, 1.0, 0.0)
    for h in range(HEADS):
        k_h = _head_norm_rope(head_rows(k_parts, h), kn_ref[...], cos, sin)
        k_tok = jnp.concatenate([k_h, own_block], axis=0).T
        kbuf[i, h] = k_tok.astype(bf16)
        kmean = jnp.sum(k_tok, axis=0, keepdims=True) * (1.0 / MOBA_BLOCK)
        kmbuf[h, pl.ds(i, 1), :] = kmean
        vbuf[i, h, :HEAD_DIM, :] = head_rows(v_parts, h).astype(bf16)
        vbuf[i, h, HEAD_DIM:, :] = ones_row.astype(bf16)

    blk = lax.broadcasted_iota(jnp.int32, (BIAS_ROWS, TOKENS), 0)
    fully_past = blk < i
    for h in range(HEADS):
        q_h = _head_norm_rope(head_rows(q_parts, h), qn_ref[...], cos, sin)
        q_h = (q_h * Q_SCALE).astype(bf16)
        gate = _bdot(kmbuf[h, :, :HEAD_DIM].astype(bf16), q_h)[:BIAS_ROWS, :]
        g = jnp.where(fully_past, gate, -jnp.inf)
        rank = jnp.zeros((BIAS_ROWS, TOKENS), jnp.int32)
        for jp in range(n_blocks - 1):
            g_jp = g[jp:jp + 1, :]
            ahead = jnp.logical_or(g_jp > g, jnp.logical_and(g_jp == g, blk > jp))
            rank = rank + ahead.astype(jnp.int32)
        chosen = jnp.logical_and(rank < MOBA_TOPK, fully_past)
        bias = jnp.where(jnp.logical_or(chosen, blk == i), 0.0, MASKED)
        zeros = jnp.zeros((KQ_DIM - HEAD_DIM - BIAS_ROWS, TOKENS), f32)
        qbuf[h, :HEAD_DIM, :] = q_h
        qbuf[h, HEAD_DIM:, :] = jnp.concatenate([bias, zeros], axis=0).astype(bf16)

    tok = i * TOKENS + lax.broadcasted_iota(jnp.int32, (1, TOKENS), 1)
    groups_per_part = len(POOL_WINDOWS) // 2
    for gi, w in enumerate(POOL_WINDOWS):
        rows = slice(gi * POOL_GROUP_WIDTH, (gi + 1) * POOL_GROUP_WIDTH)
        gg = gi % groups_per_part
        u_g = u_parts[gi // groups_per_part][gg * POOL_GROUP_WIDTH:(gg + 1) * POOL_GROUP_WIDTH, :]
        tiles = [uprev[rows, :]] + [u_g[:, c * LANES:(c + 1) * LANES]
                                    for c in range(TOKENS // LANES)]
        uprev[rows, :] = tiles[-1]
        shift = 1
        while shift < w:
            shifted = _shift_tokens(tiles, shift)
            tiles = [a + b for a, b in zip(tiles, shifted)]
            shift *= 2
        win_sum = jnp.concatenate(tiles[1:], axis=1)
        inv_cnt = 1.0 / jnp.minimum(tok + 1, w).astype(f32)
        pooled = win_sum * inv_cnt - u_g
        y = _bdot(wgrp_ref[gi], pooled.astype(bf16)) * ps_ref[rows, :]
        op_sc[rows, :] = y.astype(bf16)

    key_pos = lax.broadcasted_iota(jnp.int32, (MOBA_BLOCK, TOKENS), 0)
    qry_pos = lax.broadcasted_iota(jnp.int32, (MOBA_BLOCK, TOKENS), 1)
    causal = key_pos <= qry_pos

    def score_head(j, slot, h, own=False):
        s = _bdot(kbuf[j, h], qbuf[h])
        if own:
            s = jnp.where(causal, s, MASKED)
        s_ref[slot, h] = s
        m_new = jnp.max(s, axis=0, keepdims=True)
        if not own:
            m_old = m_ref[1 - slot, h:h + 1, :]
            m_new = jnp.maximum(m_old, m_new)
            alpha_ref[slot, h:h + 1, :] = jnp.exp2(m_old - m_new)
        m_ref[slot, h:h + 1, :] = m_new

    def value_head(j, slot, h, own=False):
        p = jnp.exp2(s_ref[slot, h] - m_ref[slot, h:h + 1, :]).astype(bf16)
        pv = _bdot(vbuf[j, h], p)
        acc_ref[h] = pv if own else acc_ref[h] * alpha_ref[slot, h:h + 1, :] + pv

    def score_pass(j, slot, own=False):
        for h in range(HEADS):
            score_head(j, slot, h, own)

    def value_pass(j, slot, own=False):
        for h in range(HEADS):
            value_head(j, slot, h, own)

    score_pass(i, 1, own=True)

    fT = f_sc[...]
    r2_prev = r2_sc[0:1, :]
    for c in range(D_MODEL // OUT_CHUNK):
        chunk = slice(c * OUT_CHUNK, (c + 1) * OUT_CHUNK)
        for rows, y in zip(_halves(chunk), _wdot2(wff2_ref, chunk, fT)):
            o_ref[:, rows] = (x1_sc[rows, :] + y * r2_prev).T

    value_pass(i, 1, own=True)

    n_past = jnp.where(n == n_tiles, 0, i)

    def visit(blocks):
        score_pass(blocks[0], 0)
        for k in range(1, len(blocks)):
            for h in range(HEADS):
                score_head(blocks[k], k % 2, h)
                value_head(blocks[k - 1], (k - 1) % 2, h)
        value_pass(blocks[-1], (len(blocks) - 1) % 2)

    n_quad = 4
    has_quad = n_past >= n_quad

    @pl.when(has_quad)
    def _first_four():
        visit(list(range(n_quad)))

    first = jnp.where(has_quad, n_quad, 0)

    def past_pair(jj, carry):
        visit([first + 2 * jj, first + 2 * jj + 1])
        return carry

    lax.fori_loop(0, (n_past - first) // 2, past_pair, 0)

    @pl.when(n_past % 2 == 1)
    def _last_past_block():
        score_pass(n_past - 1, 0)
        value_pass(n_past - 1, 0)

    for h in range(HEADS):
        o_h = acc_ref[h, :HEAD_DIM, :] * (1.0 / acc_ref[h, HEAD_DIM:HEAD_DIM + 1, :])
        o_sc[h * HEAD_DIM:(h + 1) * HEAD_DIM, :] = o_h.astype(bf16)

    upa_parts = _wdot2(wupa_ref, all_rows, o_sc[...])
    upp_parts = _wdot2(wupp_ref, all_rows, op_sc[...])
    for rows, upa, upp in zip(_halves(all_rows), upa_parts, upp_parts):
        mg_sc[rows, :] = (sga_sc[rows, :] * upa + sgp_sc[rows, :] * upp).astype(bf16)
    sumsq = jnp.zeros((1, TOKENS), f32)
    for rows, d in zip(_halves(all_rows), _wdot2(wout_ref, all_rows, mg_sc[...])):
        x1 = xT[rows, :] + d
        x1_sc[rows, :] = x1
        h2_sc[rows, :] = (x1 * gmlp_ref[rows, :]).astype(bf16)
        sumsq = sumsq + jnp.sum(x1 * x1, axis=0, keepdims=True)
    r = lax.rsqrt(sumsq * (1.0 / D_MODEL) + RMS_EPS)
    r2_sc[...] = jnp.broadcast_to(r * r, r2_sc.shape)


def _bcast_cols(v, n):
    return jnp.broadcast_to(v.astype(jnp.float32)[:, None], (v.shape[0], n))


@jax.jit
def kernel(x, norm_mix, w_in, q_norm, k_norm, w_pool_grp, pool_scale, w_up_attn, w_up_pool,
           w_out, norm_mlp, w_ff1, w_ff2):
    batch, seq, d_model = x.shape
    assert d_model == D_MODEL and seq % TOKENS == 0 and w_in.shape[0] == 1
    n_blocks = seq // TOKENS
    assert n_blocks <= BIAS_ROWS
    bf16 = jnp.bfloat16

    half = jnp.arange(ROPE_HALF, dtype=jnp.float32)
    inv_freq = ROPE_THETA ** (-half / ROPE_HALF)
    ang = inv_freq[:, None] * jnp.arange(seq).astype(jnp.float32)[None, :]

    n_tiles = batch * n_blocks

    def tile(n):
        t = jnp.clip(n, 0, n_tiles - 1)
        return t // n_blocks, t % n_blocks

    whole = pl.BlockSpec(memory_space=pltpu.VMEM)
    in_hbm = pl.BlockSpec(memory_space=pl.ANY)
    weights = (w_in, w_up_attn, w_up_pool, w_out, w_ff1, w_ff2)
    x_spec = pl.BlockSpec((None, TOKENS, D_MODEL), lambda n: (*tile(n), 0))
    out_spec = pl.BlockSpec((None, TOKENS, D_MODEL), lambda n: (*tile(n - 1), 0))
    rope_spec = pl.BlockSpec((ROPE_HALF, TOKENS), lambda n: (0, tile(n)[1]))

    return pl.pallas_call(
        _layer_kernel,
        grid=(n_tiles + 1,),
        in_specs=[x_spec, rope_spec, rope_spec] + [whole] * 6 + [in_hbm] * len(weights),
        out_specs=out_spec,
        out_shape=jax.ShapeDtypeStruct(x.shape, x.dtype),
        scratch_shapes=[
            *[pltpu.VMEM((w.shape[2], w.shape[1]), bf16) for w in weights],
            pltpu.SemaphoreType.DMA((2 * HEADS,)),
            pltpu.VMEM((n_blocks, HEADS, MOBA_BLOCK, KQ_DIM), bf16),
            pltpu.VMEM((n_blocks, HEADS, V_ROWS, MOBA_BLOCK), bf16),
            pltpu.VMEM((HEADS, 2 * BIAS_ROWS, KQ_DIM), jnp.float32),
            pltpu.VMEM((POOL_WIDTH, LANES), jnp.float32),
            pltpu.VMEM((HEADS, KQ_DIM, TOKENS), bf16),
            pltpu.VMEM((2, HEADS, TOKENS), jnp.float32),
            pltpu.VMEM((2, HEADS, TOKENS), jnp.float32),
            pltpu.VMEM((HEADS, V_ROWS, TOKENS), jnp.float32),
            pltpu.VMEM((2, HEADS, MOBA_BLOCK, TOKENS), jnp.float32),
            pltpu.VMEM((ATTN_WIDTH, TOKENS), bf16),
            pltpu.VMEM((POOL_WIDTH, TOKENS), bf16),
            pltpu.VMEM((D_MODEL, TOKENS), jnp.float32),
            pltpu.VMEM((D_MODEL, TOKENS), jnp.float32),
            pltpu.VMEM((D_MODEL, TOKENS), bf16),
            pltpu.VMEM((D_MODEL, TOKENS), jnp.float32),
            pltpu.VMEM((D_MODEL, TOKENS), bf16),
            pltpu.VMEM((16, TOKENS), jnp.float32),
            pltpu.VMEM((D_FF, TOKENS), bf16),
        ],
        compiler_params=pltpu.CompilerParams(
            dimension_semantics=("arbitrary",),
            vmem_limit_bytes=VMEM_LIMIT_BYTES),
        name="moba_pool_layer",
    )(x, jnp.cos(ang), jnp.sin(ang),
      _bcast_cols(norm_mix[0], TOKENS), _bcast_cols(norm_mlp[0], TOKENS),
      _bcast_cols(q_norm[0], TOKENS), _bcast_cols(k_norm[0], TOKENS),
      _bcast_cols(pool_scale[0], TOKENS),
      jnp.swapaxes(w_pool_grp[0], 1, 2).astype(bf16), *weights)
```

```python
import jax
import jax.numpy as jnp
from jax import lax
from jax.experimental import pallas as pl
from jax.experimental.pallas import tpu as pltpu

D_MODEL = 1024
HEADS = 8
HEAD_DIM = 64
ATTN_WIDTH = HEADS * HEAD_DIM
MOBA_BLOCK = 256
MOBA_TOPK = 3
ROPE_THETA = 500000.0
ROPE_HALF = HEAD_DIM // 8
POOL_WINDOWS = (2, 4, 8, 16)
POOL_GROUP_WIDTH = 128
POOL_WIDTH = len(POOL_WINDOWS) * POOL_GROUP_WIDTH
D_FF = 4 * D_MODEL
RMS_EPS = 1e-6

LANES = 128
TOKENS = MOBA_BLOCK
MASKED = -1e30
KQ_DIM = LANES
BIAS_ROWS = 8
V_ROWS = HEAD_DIM + 16
Q_SCALE = HEAD_DIM ** -0.5 * 1.4426950408889634
FF_CHUNK = 1024
OUT_CHUNK = 512
VMEM_LIMIT_BYTES = 61 * 1024 * 1024

_Q0, _K0, _V0, _U0 = 0, ATTN_WIDTH, 2 * ATTN_WIDTH, 3 * ATTN_WIDTH
_GA0 = 3 * ATTN_WIDTH + POOL_WIDTH
_GP0 = _GA0 + D_MODEL
IN_WIDTH = _GP0 + D_MODEL


def _bdot(a, b):
    return jnp.dot(a, b, preferred_element_type=jnp.float32)


def _wdot2(w_ref, rows, act, cols=slice(None)):
    mid = (rows.start + rows.stop) // 2
    return [_bdot(w_ref[rows.start:mid, cols], act), _bdot(w_ref[mid:rows.stop, cols], act)]


def _halves(rows):
    mid = (rows.start + rows.stop) // 2
    return [slice(rows.start, mid), slice(mid, rows.stop)]


def _sigmoid(x):
    return 0.5 * jnp.tanh(0.5 * x) + 0.5


def _rms_scale(t):
    return lax.rsqrt(jnp.mean(t * t, axis=0, keepdims=True) + RMS_EPS)


def _head_norm_rope(t, gain, cos, sin):
    y = t * _rms_scale(t) * gain
    x1 = y[0:ROPE_HALF, :]
    x2 = y[ROPE_HALF:2 * ROPE_HALF, :]
    return jnp.concatenate(
        [x1 * cos - x2 * sin, x2 * cos + x1 * sin, y[2 * ROPE_HALF:, :]], axis=0)


def _shift_tokens(tiles, shift):
    lane = lax.broadcasted_iota(jnp.int32, tiles[0].shape, 1)
    rolled = [pltpu.roll(t, shift, axis=1) for t in tiles]
    out = [rolled[0]]
    for c in range(1, len(tiles)):
        out.append(jnp.where(lane < shift, rolled[c - 1], rolled[c]))
    return out


def _load_weights(pairs, stage, sems):
    side = stage.shape[-1]
    slots_per_row = stage.shape[1]
    n_slots = stage.shape[0] * slots_per_row
    tiles = [(src, dst, k0, n0)
             for src, dst in pairs
             for k0 in range(0, dst.shape[1], side)
             for n0 in range(0, dst.shape[0], side)]

    def slot_of(t):
        slot = t % n_slots
        return slot, stage.at[slot // slots_per_row, slot % slots_per_row]

    def copy(t):
        src, _, k0, n0 = tiles[t]
        slot, buf = slot_of(t)
        return pltpu.make_async_copy(src.at[0, pl.ds(k0, side), pl.ds(n0, side)], buf, sems.at[slot])

    for t in range(min(n_slots, len(tiles))):
        copy(t).start()
    group = 4
    for t0 in range(0, len(tiles), group):
        ts = range(t0, min(t0 + group, len(tiles)))
        for t in ts:
            copy(t).wait()
        for t in ts:
            _, dst, k0, n0 = tiles[t]
            dst[n0:n0 + side, k0:k0 + side] = slot_of(t)[1][...].T.astype(dst.dtype)
        for t in ts:
            if t + n_slots < len(tiles):
                copy(t + n_slots).start()


def _layer_kernel(x_ref, cos_ref, sin_ref, gmix_ref, gmlp_ref, qn_ref, kn_ref, ps_ref, wgrp_ref,
                  win_hbm, wupa_hbm, wupp_hbm, wout_hbm, wff1_hbm, wff2_hbm,
                  o_ref,
                  win_ref, wupa_ref, wupp_ref, wout_ref, wff1_ref, wff2_ref, wsem,
                  kbuf, vbuf, kmbuf, uprev, qbuf, m_ref, alpha_ref, acc_ref, s_ref,
                  o_sc, op_sc, sga_sc, sgp_sc, mg_sc, x1_sc, h2_sc, f_sc):
    f32, bf16 = jnp.float32, jnp.bfloat16
    n = pl.program_id(0)
    n_tiles = pl.num_programs(0) - 1
    n_blocks = kbuf.shape[0]
    i = lax.rem(jnp.minimum(n, n_tiles - 1), n_blocks)

    @pl.when(n == 0)
    def _first_step():
        _load_weights([(win_hbm, win_ref), (wupa_hbm, wupa_ref), (wupp_hbm, wupp_ref),
                       (wout_hbm, wout_ref), (wff1_hbm, wff1_ref), (wff2_hbm, wff2_ref)],
                      s_ref, wsem)
        x1_sc[...] = jnp.zeros_like(x1_sc)
        h2_sc[...] = jnp.zeros_like(h2_sc)

    @pl.when(i == 0)
    def _start_of_sequence():
        kmbuf[...] = jnp.zeros_like(kmbuf)
        uprev[...] = jnp.zeros_like(uprev)

    h2T = h2_sc[...]
    for c in range(D_FF // FF_CHUNK):
        chunk = slice(c * FF_CHUNK, (c + 1) * FF_CHUNK)
        for rows, z in zip(_halves(chunk), _wdot2(wff1_ref, chunk, h2T)):
            f = jnp.maximum(z, 0.0)
            f_sc[rows, :] = (f * f).astype(bf16)

    xT = x_ref[...].T
    hT = (xT * _rms_scale(xT) * gmix_ref[...]).astype(bf16)

    def proj2(lo, width):
        return _wdot2(win_ref, slice(lo, lo + width), hT)

    def head_rows(parts, h):
        per_part = HEADS // 2
        hh = h % per_part
        return parts[h // per_part][hh * HEAD_DIM:(hh + 1) * HEAD_DIM, :]

    cos = cos_ref[...]
    sin = sin_ref[...]

    k_parts = proj2(_K0, ATTN_WIDTH)
    q_parts = proj2(_Q0, ATTN_WIDTH)
    u_parts = proj2(_U0, POOL_WIDTH)
    all_rows = slice(0, D_MODEL)
    for rows, ga, gp in zip(_halves(all_rows), proj2(_GA0, D_MODEL), proj2(_GP0, D_MODEL)):
        sga_sc[rows, :] = _sigmoid(ga)
        sgp_sc[rows, :] = _sigmoid(gp)
    pad_row = lax.broadcasted_iota(jnp.int32, (KQ_DIM - HEAD_DIM, TOKENS), 0)
    own_block = jnp.where(pad_row == i, 1.0, 0.0)
    ones_row = jnp.where(
        lax.broadcasted_iota(jnp.int32, (V_ROWS - HEAD_DIM, TOKENS), 0) == 0, 1.0, 0.0)
    for h in range(HEADS):
        k_h = _head_norm_rope(head_rows(k_parts, h), kn_ref[...], cos, sin)
        k_tok = jnp.concatenate([k_h, own_block], axis=0).T
        kbuf[i, h] = k_tok.astype(bf16)
        kmean = jnp.sum(k_tok, axis=0, keepdims=True) * (1.0 / MOBA_BLOCK)
        kmbuf[h, pl.ds(i, 1), :] = kmean

    blk = lax.broadcasted_iota(jnp.int32, (BIAS_ROWS, TOKENS), 0)
    fully_past = blk < i
    for h in range(HEADS):
        q_h = _head_norm_rope(head_rows(q_parts, h), qn_ref[...], cos, sin).astype(bf16)
        gate = _bdot(kmbuf[h, :, :HEAD_DIM].astype(bf16), q_h)[:BIAS_ROWS, :]
        g = jnp.where(fully_past, gate, -jnp.inf)
        rank = jnp.zeros((BIAS_ROWS, TOKENS), jnp.int32)
        for jp in range(n_blocks - 1):
            g_jp = g[jp:jp + 1, :]
            ahead = jnp.logical_or(g_jp > g, jnp.logical_and(g_jp == g, blk > jp))
            rank = rank + ahead.astype(jnp.int32)
        chosen = jnp.logical_and(rank < MOBA_TOPK, fully_past)
        bias = jnp.where(jnp.logical_or(chosen, blk == i), 0.0, MASKED)
        zeros = jnp.zeros((KQ_DIM - HEAD_DIM - BIAS_ROWS, TOKENS), f32)
        qbuf[h, :HEAD_DIM, :] = q_h
        qbuf[h, HEAD_DIM:, :] = jnp.concatenate([bias, zeros], axis=0).astype(bf16)

    tok = i * TOKENS + lax.broadcasted_iota(jnp.int32, (1, TOKENS), 1)
    groups_per_part = len(POOL_WINDOWS) // 2
    for gi, w in enumerate(POOL_WINDOWS):
        rows = slice(gi * POOL_GROUP_WIDTH, (gi + 1) * POOL_GROUP_WIDTH)
        gg = gi % groups_per_part
        u_g = u_parts[gi // groups_per_part][gg * POOL_GROUP_WIDTH:(gg + 1) * POOL_GROUP_WIDTH, :]
        tiles = [uprev[rows, :]] + [u_g[:, c * LANES:(c + 1) * LANES]
                                    for c in range(TOKENS // LANES)]
        uprev[rows, :] = tiles[-1]
        shift = 1
        while shift < w:
            shifted = _shift_tokens(tiles, shift)
            tiles = [a + b for a, b in zip(tiles, shifted)]
            shift *= 2
        win_sum = jnp.concatenate(tiles[1:], axis=1)
        inv_cnt = 1.0 / jnp.minimum(tok + 1, w).astype(f32)
        pooled = win_sum * inv_cnt - u_g
        y = _bdot(wgrp_ref[gi], pooled.astype(bf16)) * ps_ref[rows, :]
        op_sc[rows, :] = y.astype(bf16)

    key_pos = lax.broadcasted_iota(jnp.int32, (MOBA_BLOCK, TOKENS), 0)
    qry_pos = lax.broadcasted_iota(jnp.int32, (MOBA_BLOCK, TOKENS), 1)
    causal = key_pos <= qry_pos

    def score_head(j, slot, h, own=False):
        s = _bdot(kbuf[j, h], qbuf[h])
        if own:
            s = jnp.where(causal, s, MASKED)
        s_ref[slot, h] = s
        m_new = jnp.max(s, axis=0, keepdims=True)
        if own:
            alpha_ref[slot, h:h + 1, :] = jnp.ones_like(m_new)
        else:
            m_old = m_ref[1 - slot, h:h + 1, :]
            m_new = jnp.maximum(m_old, m_new)
            alpha_ref[slot, h:h + 1, :] = jnp.exp2(m_old - m_new)
        m_ref[slot, h:h + 1, :] = m_new

    def value_head(j, slot, h):
        p = jnp.exp2(s_ref[slot, h] - m_ref[slot, h:h + 1, :]).astype(bf16)
        pv = _bdot(vbuf[j, h], p)
        acc_ref[h] = acc_ref[h] * alpha_ref[slot, h:h + 1, :] + pv

    own_slot = 1
    acc_ref[...] = jnp.zeros_like(acc_ref)
    for h in range(HEADS):
        score_head(i, own_slot, h, own=True)

    v_parts = proj2(_V0, ATTN_WIDTH)
    for h in range(HEADS):
        vbuf[i, h, :HEAD_DIM, :] = head_rows(v_parts, h).astype(bf16)
        vbuf[i, h, HEAD_DIM:, :] = ones_row.astype(bf16)

    n_past = jnp.where(n == n_tiles, 0, i)

    def before(t):
        return jnp.where(t == 0, i, t - 1)

    def visit(blocks):
        pending = before(blocks[0])
        for k, t in enumerate(blocks):
            for h in range(HEADS):
                score_head(t, k % 2, h)
                value_head(pending, 1 - k % 2, h)
            pending = t

    n_quad = 4
    has_quad = n_past >= n_quad

    @pl.when(has_quad)
    def _first_four():
        visit(list(range(n_quad)))

    first = jnp.where(has_quad, n_quad, 0)

    def past_pair(jj, carry):
        visit([first + 2 * jj, first + 2 * jj + 1])
        return carry

    lax.fori_loop(0, (n_past - first) // 2, past_pair, 0)

    @pl.when(n_past % 2 == 1)
    def _last_past_block():
        visit([n_past - 1])

    fT = f_sc[...]
    x1_prev = x1_sc[...]
    r_prev = _rms_scale(x1_prev)
    r2_prev = r_prev * r_prev

    def mlp_out(c):
        chunk = slice(c * OUT_CHUNK, (c + 1) * OUT_CHUNK)
        for rows, y in zip(_halves(chunk), _wdot2(wff2_ref, chunk, fT)):
            o_ref[:, rows] = (x1_prev[rows, :] + y * r2_prev).T

    upp_parts = _wdot2(wupp_ref, all_rows, op_sc[...])
    last_slot = lax.rem(n_past + 1, 2)
    for h in range(HEADS):
        value_head(before(n_past), last_slot, h)
    n_out_chunks = D_MODEL // OUT_CHUNK
    for c in range(n_out_chunks // 2):
        mlp_out(c)

    for h in range(HEADS):
        o_h = acc_ref[h, :HEAD_DIM, :] * (1.0 / acc_ref[h, HEAD_DIM:HEAD_DIM + 1, :])
        o_sc[h * HEAD_DIM:(h + 1) * HEAD_DIM, :] = o_h.astype(bf16)

    upa_parts = _wdot2(wupa_ref, all_rows, o_sc[...])
    for c in range(n_out_chunks // 2, n_out_chunks):
        mlp_out(c)
    for rows, upa, upp in zip(_halves(all_rows), upa_parts, upp_parts):
        mg_sc[rows, :] = (sga_sc[rows, :] * upa + sgp_sc[rows, :] * upp).astype(bf16)
    for rows, d in zip(_halves(all_rows), _wdot2(wout_ref, all_rows, mg_sc[...])):
        x1 = xT[rows, :] + d
        x1_sc[rows, :] = x1
        h2_sc[rows, :] = (x1 * gmlp_ref[rows, :]).astype(bf16)


def _bcast_cols(v, n):
    return jnp.broadcast_to(v.astype(jnp.float32)[:, None], (v.shape[0], n))


@jax.jit
def kernel(x, norm_mix, w_in, q_norm, k_norm, w_pool_grp, pool_scale, w_up_attn, w_up_pool,
           w_out, norm_mlp, w_ff1, w_ff2):
    batch, seq, d_model = x.shape
    assert d_model == D_MODEL and seq % TOKENS == 0 and w_in.shape[0] == 1
    n_blocks = seq // TOKENS
    assert n_blocks <= BIAS_ROWS
    bf16 = jnp.bfloat16

    half = jnp.arange(ROPE_HALF, dtype=jnp.float32)
    inv_freq = ROPE_THETA ** (-half / ROPE_HALF)
    ang = inv_freq[:, None] * jnp.arange(seq).astype(jnp.float32)[None, :]

    n_tiles = batch * n_blocks

    def tile(n):
        t = jnp.clip(n, 0, n_tiles - 1)
        return t // n_blocks, t % n_blocks

    whole = pl.BlockSpec(memory_space=pltpu.VMEM)
    in_hbm = pl.BlockSpec(memory_space=pl.ANY)
    weights = (w_in, w_up_attn, w_up_pool, w_out, w_ff1, w_ff2)
    x_spec = pl.BlockSpec((None, TOKENS, D_MODEL), lambda n: (*tile(n), 0))
    out_spec = pl.BlockSpec((None, TOKENS, D_MODEL), lambda n: (*tile(n - 1), 0))
    rope_spec = pl.BlockSpec((ROPE_HALF, TOKENS), lambda n: (0, tile(n)[1]))

    return pl.pallas_call(
        _layer_kernel,
        grid=(n_tiles + 1,),
        in_specs=[x_spec, rope_spec, rope_spec] + [whole] * 6 + [in_hbm] * len(weights),
        out_specs=out_spec,
        out_shape=jax.ShapeDtypeStruct(x.shape, x.dtype),
        scratch_shapes=[
            *[pltpu.VMEM((w.shape[2], w.shape[1]), bf16) for w in weights],
            pltpu.SemaphoreType.DMA((2 * HEADS,)),
            pltpu.VMEM((n_blocks, HEADS, MOBA_BLOCK, KQ_DIM), bf16),
            pltpu.VMEM((n_blocks, HEADS, V_ROWS, MOBA_BLOCK), bf16),
            pltpu.VMEM((HEADS, 2 * BIAS_ROWS, KQ_DIM), jnp.float32),
            pltpu.VMEM((POOL_WIDTH, LANES), jnp.float32),
            pltpu.VMEM((HEADS, KQ_DIM, TOKENS), bf16),
            pltpu.VMEM((2, HEADS, TOKENS), jnp.float32),
            pltpu.VMEM((2, HEADS, TOKENS), jnp.float32),
            pltpu.VMEM((HEADS, V_ROWS, TOKENS), jnp.float32),
            pltpu.VMEM((2, HEADS, MOBA_BLOCK, TOKENS), jnp.float32),
            pltpu.VMEM((ATTN_WIDTH, TOKENS), bf16),
            pltpu.VMEM((POOL_WIDTH, TOKENS), bf16),
            pltpu.VMEM((D_MODEL, TOKENS), jnp.float32),
            pltpu.VMEM((D_MODEL, TOKENS), jnp.float32),
            pltpu.VMEM((D_MODEL, TOKENS), bf16),
            pltpu.VMEM((D_MODEL, TOKENS), jnp.float32),
            pltpu.VMEM((D_MODEL, TOKENS), bf16),
            pltpu.VMEM((D_FF, TOKENS), bf16),
        ],
        compiler_params=pltpu.CompilerParams(
            dimension_semantics=("arbitrary",),
            vmem_limit_bytes=VMEM_LIMIT_BYTES),
        name="moba_pool_layer",
    )(x, jnp.cos(ang), jnp.sin(ang),
      _bcast_cols(norm_mix[0], TOKENS), _bcast_cols(norm_mlp[0], TOKENS),
      _bcast_cols(q_norm[0] * Q_SCALE, TOKENS), _bcast_cols(k_norm[0], TOKENS),
      _bcast_cols(pool_scale[0], TOKENS),
      jnp.swapaxes(w_pool_grp[0], 1, 2).astype(bf16), *weights)
```

```python
import jax
import jax.numpy as jnp
from jax import lax
from jax.experimental import pallas as pl
from jax.experimental.pallas import tpu as pltpu

D_MODEL = 1024
HEADS = 8
HEAD_DIM = 64
ATTN_WIDTH = HEADS * HEAD_DIM
MOBA_BLOCK = 256
MOBA_TOPK = 3
ROPE_THETA = 500000.0
ROPE_HALF = HEAD_DIM // 8
POOL_WINDOWS = (2, 4, 8, 16)
POOL_GROUP_WIDTH = 128
POOL_WIDTH = len(POOL_WINDOWS) * POOL_GROUP_WIDTH
D_FF = 4 * D_MODEL
RMS_EPS = 1e-6

LANES = 128
TOKENS = MOBA_BLOCK
MASKED = -1e30
KQ_DIM = LANES
BIAS_ROWS = 8
V_ROWS = HEAD_DIM + 16
Q_SCALE = HEAD_DIM ** -0.5 * 1.4426950408889634
FF_CHUNK = 1024
OUT_CHUNK = 512
VMEM_LIMIT_BYTES = 61 * 1024 * 1024

_Q0, _K0, _V0, _U0 = 0, ATTN_WIDTH, 2 * ATTN_WIDTH, 3 * ATTN_WIDTH
_GA0 = 3 * ATTN_WIDTH + POOL_WIDTH
_GP0 = _GA0 + D_MODEL
IN_WIDTH = _GP0 + D_MODEL


def _bdot(a, b):
    return jnp.dot(a, b, preferred_element_type=jnp.float32)


def _wdot2(w_ref, rows, act, cols=slice(None)):
    mid = (rows.start + rows.stop) // 2
    return [_bdot(w_ref[rows.start:mid, cols], act), _bdot(w_ref[mid:rows.stop, cols], act)]


def _halves(rows):
    mid = (rows.start + rows.stop) // 2
    return [slice(rows.start, mid), slice(mid, rows.stop)]


def _sigmoid(x):
    return 0.5 * jnp.tanh(0.5 * x) + 0.5


def _rms_scale(t):
    return lax.rsqrt(jnp.mean(t * t, axis=0, keepdims=True) + RMS_EPS)


def _head_norm_rope(t, gain, cos, sin):
    y = t * _rms_scale(t) * gain
    x1 = y[0:ROPE_HALF, :]
    x2 = y[ROPE_HALF:2 * ROPE_HALF, :]
    return jnp.concatenate(
        [x1 * cos - x2 * sin, x2 * cos + x1 * sin, y[2 * ROPE_HALF:, :]], axis=0)


def _shift_tokens(tiles, shift):
    lane = lax.broadcasted_iota(jnp.int32, tiles[0].shape, 1)
    rolled = [pltpu.roll(t, shift, axis=1) for t in tiles]
    out = [rolled[0]]
    for c in range(1, len(tiles)):
        out.append(jnp.where(lane < shift, rolled[c - 1], rolled[c]))
    return out


def _load_weights(pairs, stage, sems):
    side = stage.shape[-1]
    slots_per_row = stage.shape[1]
    n_slots = stage.shape[0] * slots_per_row
    tiles = [(src, dst, k0, n0)
             for src, dst in pairs
             for k0 in range(0, dst.shape[1], side)
             for n0 in range(0, dst.shape[0], side)]

    def slot_of(t):
        slot = t % n_slots
        return slot, stage.at[slot // slots_per_row, slot % slots_per_row]

    def copy(t):
        src, _, k0, n0 = tiles[t]
        slot, buf = slot_of(t)
        return pltpu.make_async_copy(src.at[0, pl.ds(k0, side), pl.ds(n0, side)], buf, sems.at[slot])

    for t in range(min(n_slots, len(tiles))):
        copy(t).start()
    group = 4
    for t0 in range(0, len(tiles), group):
        ts = range(t0, min(t0 + group, len(tiles)))
        for t in ts:
            copy(t).wait()
        for t in ts:
            _, dst, k0, n0 = tiles[t]
            dst[n0:n0 + side, k0:k0 + side] = slot_of(t)[1][...].T.astype(dst.dtype)
        for t in ts:
            if t + n_slots < len(tiles):
                copy(t + n_slots).start()


def _layer_kernel(x_ref, cos_ref, sin_ref, gmix_ref, gmlp_ref, qn_ref, kn_ref, ps_ref, wgrp_ref,
                  win_hbm, wupa_hbm, wupp_hbm, wout_hbm, wff1_hbm, wff2_hbm,
                  o_ref,
                  win_ref, wupa_ref, wupp_ref, wout_ref, wff1_ref, wff2_ref, wsem,
                  kbuf, vbuf, kmbuf, uprev, qbuf, m_ref, alpha_ref, acc_ref, s_ref,
                  o_sc, op_sc, sga_sc, sgp_sc, mg_sc, x1_sc, h2_sc, f_sc):
    f32, bf16 = jnp.float32, jnp.bfloat16
    n = pl.program_id(0)
    n_tiles = pl.num_programs(0) - 1
    n_blocks = kbuf.shape[0]
    i = lax.rem(jnp.minimum(n, n_tiles - 1), n_blocks)

    @pl.when(n == 0)
    def _first_step():
        _load_weights([(win_hbm, win_ref), (wupa_hbm, wupa_ref), (wupp_hbm, wupp_ref),
                       (wout_hbm, wout_ref), (wff1_hbm, wff1_ref), (wff2_hbm, wff2_ref)],
                      s_ref, wsem)
        x1_sc[...] = jnp.zeros_like(x1_sc)
        h2_sc[...] = jnp.zeros_like(h2_sc)

    @pl.when(i == 0)
    def _start_of_sequence():
        kmbuf[...] = jnp.zeros_like(kmbuf)
        uprev[...] = jnp.zeros_like(uprev)

    h2T = h2_sc[...]
    for c in range(D_FF // FF_CHUNK):
        chunk = slice(c * FF_CHUNK, (c + 1) * FF_CHUNK)
        for rows, z in zip(_halves(chunk), _wdot2(wff1_ref, chunk, h2T)):
            f = jnp.maximum(z, 0.0)
            f_sc[rows, :] = (f * f).astype(bf16)

    xT = x_ref[...].T
    hT = (xT * _rms_scale(xT) * gmix_ref[...]).astype(bf16)

    def proj2(lo, width):
        return _wdot2(win_ref, slice(lo, lo + width), hT)

    def head_rows(parts, h):
        per_part = HEADS // 2
        hh = h % per_part
        return parts[h // per_part][hh * HEAD_DIM:(hh + 1) * HEAD_DIM, :]

    cos = cos_ref[...]
    sin = sin_ref[...]

    k_parts = proj2(_K0, ATTN_WIDTH)
    q_parts = proj2(_Q0, ATTN_WIDTH)
    u_parts = proj2(_U0, POOL_WIDTH)
    all_rows = slice(0, D_MODEL)
    for rows, ga, gp in zip(_halves(all_rows), proj2(_GA0, D_MODEL), proj2(_GP0, D_MODEL)):
        sga_sc[rows, :] = _sigmoid(ga)
        sgp_sc[rows, :] = _sigmoid(gp)
    pad_row = lax.broadcasted_iota(jnp.int32, (KQ_DIM - HEAD_DIM, TOKENS), 0)
    own_block = jnp.where(pad_row == i, 1.0, 0.0)
    ones_row = jnp.where(
        lax.broadcasted_iota(jnp.int32, (V_ROWS - HEAD_DIM, TOKENS), 0) == 0, 1.0, 0.0)
    for h in range(HEADS):
        k_h = _head_norm_rope(head_rows(k_parts, h), kn_ref[...], cos, sin)
        k_tok = jnp.concatenate([k_h, own_block], axis=0).T
        kbuf[i, h] = k_tok.astype(bf16)
        kmean = jnp.sum(k_tok, axis=0, keepdims=True) * (1.0 / MOBA_BLOCK)
        kmbuf[h, pl.ds(i, 1), :] = kmean

    blk = lax.broadcasted_iota(jnp.int32, (BIAS_ROWS, TOKENS), 0)
    fully_past = blk < i
    for h in range(HEADS):
        q_h = _head_norm_rope(head_rows(q_parts, h), qn_ref[...], cos, sin).astype(bf16)
        gate = _bdot(kmbuf[h, :, :HEAD_DIM].astype(bf16), q_h)[:BIAS_ROWS, :]
        g = jnp.where(fully_past, gate, -jnp.inf)
        rank = jnp.zeros((BIAS_ROWS, TOKENS), jnp.int32)
        for jp in range(n_blocks - 1):
            g_jp = g[jp:jp + 1, :]
            ahead = jnp.logical_or(g_jp > g, jnp.logical_and(g_jp == g, blk > jp))
            rank = rank + ahead.astype(jnp.int32)
        chosen = jnp.logical_and(rank < MOBA_TOPK, fully_past)
        bias = jnp.where(jnp.logical_or(chosen, blk == i), 0.0, MASKED)
        zeros = jnp.zeros((KQ_DIM - HEAD_DIM - BIAS_ROWS, TOKENS), f32)
        qbuf[h, :HEAD_DIM, :] = q_h
        qbuf[h, HEAD_DIM:, :] = jnp.concatenate([bias, zeros], axis=0).astype(bf16)

    tok = i * TOKENS + lax.broadcasted_iota(jnp.int32, (1, TOKENS), 1)
    groups_per_part = len(POOL_WINDOWS) // 2
    for gi, w in enumerate(POOL_WINDOWS):
        rows = slice(gi * POOL_GROUP_WIDTH, (gi + 1) * POOL_GROUP_WIDTH)
        gg = gi % groups_per_part
        u_g = u_parts[gi // groups_per_part][gg * POOL_GROUP_WIDTH:(gg + 1) * POOL_GROUP_WIDTH, :]
        tiles = [uprev[rows, :]] + [u_g[:, c * LANES:(c + 1) * LANES]
                                    for c in range(TOKENS // LANES)]
        uprev[rows, :] = tiles[-1]
        shift = 1
        while shift < w:
            shifted = _shift_tokens(tiles, shift)
            tiles = [a + b for a, b in zip(tiles, shifted)]
            shift *= 2
        win_sum = jnp.concatenate(tiles[1:], axis=1)
        inv_cnt = 1.0 / jnp.minimum(tok + 1, w).astype(f32)
        pooled = win_sum * inv_cnt - u_g
        y = _bdot(wgrp_ref[gi], pooled.astype(bf16)) * ps_ref[rows, :]
        op_sc[rows, :] = y.astype(bf16)

    key_pos = lax.broadcasted_iota(jnp.int32, (MOBA_BLOCK, TOKENS), 0)
    qry_pos = lax.broadcasted_iota(jnp.int32, (MOBA_BLOCK, TOKENS), 1)
    causal = key_pos <= qry_pos

    def score_head(j, slot, h, own=False):
        s = _bdot(kbuf[j, h], qbuf[h])
        if own:
            s = jnp.where(causal, s, MASKED)
        s_ref[slot, h] = s
        m_new = jnp.max(s, axis=0, keepdims=True)
        if own:
            alpha_ref[slot, h:h + 1, :] = jnp.ones_like(m_new)
        else:
            m_old = m_ref[1 - slot, h:h + 1, :]
            m_new = jnp.maximum(m_old, m_new)
            alpha_ref[slot, h:h + 1, :] = jnp.exp2(m_old - m_new)
        m_ref[slot, h:h + 1, :] = m_new

    def value_head(j, slot, h):
        p = jnp.exp2(s_ref[slot, h] - m_ref[slot, h:h + 1, :]).astype(bf16)
        pv = _bdot(vbuf[j, h], p)
        acc_ref[h] = acc_ref[h] * alpha_ref[slot, h:h + 1, :] + pv

    own_slot = 1
    acc_ref[...] = jnp.zeros_like(acc_ref)
    for h in range(HEADS):
        score_head(i, own_slot, h, own=True)

    v_parts = proj2(_V0, ATTN_WIDTH)
    for h in range(HEADS):
        vbuf[i, h, :HEAD_DIM, :] = head_rows(v_parts, h).astype(bf16)
        vbuf[i, h, HEAD_DIM:, :] = ones_row.astype(bf16)

    fT = f_sc[...]
    r_prev = _rms_scale(x1_sc[...])
    r2_prev = r_prev * r_prev
    for c in range(D_MODEL // OUT_CHUNK):
        chunk = slice(c * OUT_CHUNK, (c + 1) * OUT_CHUNK)
        for rows, y in zip(_halves(chunk), _wdot2(wff2_ref, chunk, fT)):
            o_ref[:, rows] = (x1_sc[rows, :] + y * r2_prev).T

    n_past = jnp.where(n == n_tiles, 0, i)

    def before(t):
        return jnp.where(t == 0, i, t - 1)

    def visit(blocks):
        pending = before(blocks[0])
        for k, t in enumerate(blocks):
            for h in range(HEADS):
                score_head(t, k % 2, h)
                value_head(pending, 1 - k % 2, h)
            pending = t

    n_quad = 4
    has_quad = n_past >= n_quad

    @pl.when(has_quad)
    def _first_four():
        visit(list(range(n_quad)))

    first = jnp.where(has_quad, n_quad, 0)

    def past_pair(jj, carry):
        visit([first + 2 * jj, first + 2 * jj + 1])
        return carry

    lax.fori_loop(0, (n_past - first) // 2, past_pair, 0)

    @pl.when(n_past % 2 == 1)
    def _last_past_block():
        visit([n_past - 1])

    upp_parts = _wdot2(wupp_ref, all_rows, op_sc[...])
    last_slot = lax.rem(n_past + 1, 2)
    for h in range(HEADS):
        value_head(before(n_past), last_slot, h)

    for h in range(HEADS):
        o_h = acc_ref[h, :HEAD_DIM, :] * (1.0 / acc_ref[h, HEAD_DIM:HEAD_DIM + 1, :])
        o_sc[h * HEAD_DIM:(h + 1) * HEAD_DIM, :] = o_h.astype(bf16)

    upa_parts = _wdot2(wupa_ref, all_rows, o_sc[...])
    for rows, upa, upp in zip(_halves(all_rows), upa_parts, upp_parts):
        mg_sc[rows, :] = (sga_sc[rows, :] * upa + sgp_sc[rows, :] * upp).astype(bf16)
    for rows, d in zip(_halves(all_rows), _wdot2(wout_ref, all_rows, mg_sc[...])):
        x1 = xT[rows, :] + d
        x1_sc[rows, :] = x1
        h2_sc[rows, :] = (x1 * gmlp_ref[rows, :]).astype(bf16)


def _bcast_cols(v, n):
    return jnp.broadcast_to(v.astype(jnp.float32)[:, None], (v.shape[0], n))


@jax.jit
def kernel(x, norm_mix, w_in, q_norm, k_norm, w_pool_grp, pool_scale, w_up_attn, w_up_pool,
           w_out, norm_mlp, w_ff1, w_ff2):
    batch, seq, d_model = x.shape
    assert d_model == D_MODEL and seq % TOKENS == 0 and w_in.shape[0] == 1
    n_blocks = seq // TOKENS
    assert n_blocks <= BIAS_ROWS
    bf16 = jnp.bfloat16

    half = jnp.arange(ROPE_HALF, dtype=jnp.float32)
    inv_freq = ROPE_THETA ** (-half / ROPE_HALF)
    ang = inv_freq[:, None] * jnp.arange(seq).astype(jnp.float32)[None, :]

    n_tiles = batch * n_blocks

    def tile(n):
        t = jnp.clip(n, 0, n_tiles - 1)
        return t // n_blocks, t % n_blocks

    whole = pl.BlockSpec(memory_space=pltpu.VMEM)
    in_hbm = pl.BlockSpec(memory_space=pl.ANY)
    weights = (w_in, w_up_attn, w_up_pool, w_out, w_ff1, w_ff2)
    x_spec = pl.BlockSpec((None, TOKENS, D_MODEL), lambda n: (*tile(n), 0))
    out_spec = pl.BlockSpec((None, TOKENS, D_MODEL), lambda n: (*tile(n - 1), 0))
    rope_spec = pl.BlockSpec((ROPE_HALF, TOKENS), lambda n: (0, tile(n)[1]))

    return pl.pallas_call(
        _layer_kernel,
        grid=(n_tiles + 1,),
        in_specs=[x_spec, rope_spec, rope_spec] + [whole] * 6 + [in_hbm] * len(weights),
        out_specs=out_spec,
        out_shape=jax.ShapeDtypeStruct(x.shape, x.dtype),
        scratch_shapes=[
            *[pltpu.VMEM((w.shape[2], w.shape[1]), bf16) for w in weights],
            pltpu.SemaphoreType.DMA((2 * HEADS,)),
            pltpu.VMEM((n_blocks, HEADS, MOBA_BLOCK, KQ_DIM), bf16),
            pltpu.VMEM((n_blocks, HEADS, V_ROWS, MOBA_BLOCK), bf16),
            pltpu.VMEM((HEADS, 2 * BIAS_ROWS, KQ_DIM), jnp.float32),
            pltpu.VMEM((POOL_WIDTH, LANES), jnp.float32),
            pltpu.VMEM((HEADS, KQ_DIM, TOKENS), bf16),
            pltpu.VMEM((2, HEADS, TOKENS), jnp.float32),
            pltpu.VMEM((2, HEADS, TOKENS), jnp.float32),
            pltpu.VMEM((HEADS, V_ROWS, TOKENS), jnp.float32),
            pltpu.VMEM((2, HEADS, MOBA_BLOCK, TOKENS), jnp.float32),
            pltpu.VMEM((ATTN_WIDTH, TOKENS), bf16),
            pltpu.VMEM((POOL_WIDTH, TOKENS), bf16),
            pltpu.VMEM((D_MODEL, TOKENS), jnp.float32),
            pltpu.VMEM((D_MODEL, TOKENS), jnp.float32),
            pltpu.VMEM((D_MODEL, TOKENS), bf16),
            pltpu.VMEM((D_MODEL, TOKENS), jnp.float32),
            pltpu.VMEM((D_MODEL, TOKENS), bf16),
            pltpu.VMEM((D_FF, TOKENS), bf16),
        ],
        compiler_params=pltpu.CompilerParams(
            dimension_semantics=("arbitrary",),
            vmem_limit_bytes=VMEM_LIMIT_BYTES),
        name="moba_pool_layer",
    )(x, jnp.cos(ang), jnp.sin(ang),
      _bcast_cols(norm_mix[0], TOKENS), _bcast_cols(norm_mlp[0], TOKENS),
      _bcast_cols(q_norm[0] * Q_SCALE, TOKENS), _bcast_cols(k_norm[0], TOKENS),
      _bcast_cols(pool_scale[0], TOKENS),
      jnp.swapaxes(w_pool_grp[0], 1, 2).astype(bf16), *weights)
```

```python
import jax
import jax.numpy as jnp
from jax import lax
from jax.experimental import pallas as pl
from jax.experimental.pallas import tpu as pltpu

D_MODEL = 1024
HEADS = 8
HEAD_DIM = 64
ATTN_WIDTH = HEADS * HEAD_DIM
MOBA_BLOCK = 256
MOBA_TOPK = 3
ROPE_THETA = 500000.0
ROPE_HALF = HEAD_DIM // 8
POOL_WINDOWS = (2, 4, 8, 16)
POOL_GROUP_WIDTH = 128
POOL_WIDTH = len(POOL_WINDOWS) * POOL_GROUP_WIDTH
D_FF = 4 * D_MODEL
RMS_EPS = 1e-6

LANES = 128
TOKENS = MOBA_BLOCK
MASKED = -1e30
KQ_DIM = LANES
BIAS_ROWS = 8
V_ROWS = HEAD_DIM + 16
Q_SCALE = HEAD_DIM ** -0.5 * 1.4426950408889634
FF_CHUNK = 1024
OUT_CHUNK = 512
VMEM_LIMIT_BYTES = 61 * 1024 * 1024

_Q0, _K0, _V0, _U0 = 0, ATTN_WIDTH, 2 * ATTN_WIDTH, 3 * ATTN_WIDTH
_GA0 = 3 * ATTN_WIDTH + POOL_WIDTH
_GP0 = _GA0 + D_MODEL
IN_WIDTH = _GP0 + D_MODEL
_GAIN_ROWS = (0, D_MODEL, 2 * D_MODEL, 2 * D_MODEL + HEAD_DIM, 2 * D_MODEL + 2 * HEAD_DIM,
              2 * D_MODEL + 2 * HEAD_DIM + POOL_WIDTH)


def _bdot(a, b):
    return jnp.dot(a, b, preferred_element_type=jnp.float32)


def _wdot2(w_ref, rows, act, cols=slice(None)):
    mid = (rows.start + rows.stop) // 2
    return [_bdot(w_ref[rows.start:mid, cols], act), _bdot(w_ref[mid:rows.stop, cols], act)]


def _halves(rows):
    mid = (rows.start + rows.stop) // 2
    return [slice(rows.start, mid), slice(mid, rows.stop)]


def _sigmoid(x):
    return 0.5 * jnp.tanh(0.5 * x) + 0.5


def _rms_scale(t):
    return lax.rsqrt(jnp.mean(t * t, axis=0, keepdims=True) + RMS_EPS)


def _head_norm_rope(t, gain, cos, sin):
    y = t * _rms_scale(t) * gain
    x1 = y[0:ROPE_HALF, :]
    x2 = y[ROPE_HALF:2 * ROPE_HALF, :]
    return jnp.concatenate(
        [x1 * cos - x2 * sin, x2 * cos + x1 * sin, y[2 * ROPE_HALF:, :]], axis=0)


def _shift_tokens(tiles, shift):
    lane = lax.broadcasted_iota(jnp.int32, tiles[0].shape, 1)
    rolled = [pltpu.roll(t, shift, axis=1) for t in tiles]
    out = [rolled[0]]
    for c in range(1, len(tiles)):
        out.append(jnp.where(lane < shift, rolled[c - 1], rolled[c]))
    return out


def _load_weights(pairs, stage, sems):
    side = stage.shape[-1]
    slots_per_row = stage.shape[1]
    n_slots = stage.shape[0] * slots_per_row
    tiles = [(src, dst, k0, n0)
             for src, dst in pairs
             for k0 in range(0, dst.shape[1], side)
             for n0 in range(0, dst.shape[0], side)]

    def slot_of(t):
        slot = t % n_slots
        return slot, stage.at[slot // slots_per_row, slot % slots_per_row]

    def copy(t):
        src, _, k0, n0 = tiles[t]
        slot, buf = slot_of(t)
        return pltpu.make_async_copy(src.at[0, pl.ds(k0, side), pl.ds(n0, side)], buf, sems.at[slot])

    for t in range(min(n_slots, len(tiles))):
        copy(t).start()
    group = 4
    for t0 in range(0, len(tiles), group):
        ts = range(t0, min(t0 + group, len(tiles)))
        for t in ts:
            copy(t).wait()
        for t in ts:
            _, dst, k0, n0 = tiles[t]
            dst[n0:n0 + side, k0:k0 + side] = slot_of(t)[1][...].T.astype(dst.dtype)
        for t in ts:
            if t + n_slots < len(tiles):
                copy(t + n_slots).start()


def _layer_kernel(x_ref, rope_ref, gains_ref, wgrp_ref,
                  win_hbm, wupa_hbm, wupp_hbm, wout_hbm, wff1_hbm, wff2_hbm,
                  o_ref,
                  win_ref, wupa_ref, wupp_ref, wout_ref, wff1_ref, wff2_ref, wsem,
                  kbuf, vbuf, kmbuf, uprev, qbuf, m_ref, alpha_ref, acc_ref, s_ref,
                  o_sc, op_sc, sga_sc, sgp_sc, mg_sc, x1_sc, h2_sc, r2_sc, f_sc):
    f32, bf16 = jnp.float32, jnp.bfloat16
    gmix_ref, gmlp_ref, qn_ref, kn_ref, ps_ref = (
        gains_ref.at[lo:hi] for lo, hi in zip(_GAIN_ROWS[:-1], _GAIN_ROWS[1:]))
    n = pl.program_id(0)
    n_tiles = pl.num_programs(0) - 1
    n_blocks = kbuf.shape[0]
    i = lax.rem(jnp.minimum(n, n_tiles - 1), n_blocks)

    @pl.when(n == 0)
    def _first_step():
        _load_weights([(win_hbm, win_ref), (wupa_hbm, wupa_ref), (wupp_hbm, wupp_ref),
                       (wout_hbm, wout_ref), (wff1_hbm, wff1_ref), (wff2_hbm, wff2_ref)],
                      s_ref, wsem)
        x1_sc[...] = jnp.zeros_like(x1_sc)
        h2_sc[...] = jnp.zeros_like(h2_sc)
        r2_sc[...] = jnp.zeros_like(r2_sc)

    @pl.when(i == 0)
    def _start_of_sequence():
        kmbuf[...] = jnp.zeros_like(kmbuf)
        uprev[...] = jnp.zeros_like(uprev)

    h2T = h2_sc[...]
    for c in range(D_FF // FF_CHUNK):
        chunk = slice(c * FF_CHUNK, (c + 1) * FF_CHUNK)
        for rows, z in zip(_halves(chunk), _wdot2(wff1_ref, chunk, h2T)):
            f = jnp.maximum(z, 0.0)
            f_sc[rows, :] = (f * f).astype(bf16)

    xT = x_ref[...].T
    hT = (xT * _rms_scale(xT) * gmix_ref[...]).astype(bf16)

    def proj2(lo, width):
        return _wdot2(win_ref, slice(lo, lo + width), hT)

    def head_rows(parts, h):
        per_part = HEADS // 2
        hh = h % per_part
        return parts[h // per_part][hh * HEAD_DIM:(hh + 1) * HEAD_DIM, :]

    cos = rope_ref[:ROPE_HALF, :]
    sin = rope_ref[ROPE_HALF:, :]

    k_parts = proj2(_K0, ATTN_WIDTH)
    q_parts = proj2(_Q0, ATTN_WIDTH)
    v_parts = proj2(_V0, ATTN_WIDTH)
    u_parts = proj2(_U0, POOL_WIDTH)
    all_rows = slice(0, D_MODEL)
    for rows, ga, gp in zip(_halves(all_rows), proj2(_GA0, D_MODEL), proj2(_GP0, D_MODEL)):
        sga_sc[rows, :] = _sigmoid(ga)
        sgp_sc[rows, :] = _sigmoid(gp)
    pad_row = lax.broadcasted_iota(jnp.int32, (KQ_DIM - HEAD_DIM, TOKENS), 0)
    own_block = jnp.where(pad_row == i, 1.0, 0.0)
    ones_row = jnp.where(
        lax.broadcasted_iota(jnp.int32, (V_ROWS - HEAD_DIM, TOKENS), 0) == 0, 1.0, 0.0)
    for h in range(HEADS):
        k_h = _head_norm_rope(head_rows(k_parts, h), kn_ref[...], cos, sin)
        k_tok = jnp.concatenate([k_h, own_block], axis=0).T
        kbuf[i, h] = k_tok.astype(bf16)
        kmean = jnp.sum(k_tok, axis=0, keepdims=True) * (1.0 / MOBA_BLOCK)
        kmbuf[h, pl.ds(i, 1), :] = kmean
        vbuf[i, h, :HEAD_DIM, :] = head_rows(v_parts, h).astype(bf16)
        vbuf[i, h, HEAD_DIM:, :] = ones_row.astype(bf16)

    blk = lax.broadcasted_iota(jnp.int32, (BIAS_ROWS, TOKENS), 0)
    fully_past = blk < i
    for h in range(HEADS):
        q_h = _head_norm_rope(head_rows(q_parts, h), qn_ref[...], cos, sin)
        q_h = (q_h * Q_SCALE).astype(bf16)
        gate = _bdot(kmbuf[h, :, :HEAD_DIM].astype(bf16), q_h)[:BIAS_ROWS, :]
        g = jnp.where(fully_past, gate, -jnp.inf)
        rank = jnp.zeros((BIAS_ROWS, TOKENS), jnp.int32)
        for jp in range(n_blocks - 1):
            g_jp = g[jp:jp + 1, :]
            ahead = jnp.logical_or(g_jp > g, jnp.logical_and(g_jp == g, blk > jp))
            rank = rank + ahead.astype(jnp.int32)
        chosen = jnp.logical_and(rank < MOBA_TOPK, fully_past)
        bias = jnp.where(jnp.logical_or(chosen, blk == i), 0.0, MASKED)
        zeros = jnp.zeros((KQ_DIM - HEAD_DIM - BIAS_ROWS, TOKENS), f32)
        qbuf[h, :HEAD_DIM, :] = q_h
        qbuf[h, HEAD_DIM:, :] = jnp.concatenate([bias, zeros], axis=0).astype(bf16)

    tok = i * TOKENS + lax.broadcasted_iota(jnp.int32, (1, TOKENS), 1)
    groups_per_part = len(POOL_WINDOWS) // 2
    for gi, w in enumerate(POOL_WINDOWS):
        rows = slice(gi * POOL_GROUP_WIDTH, (gi + 1) * POOL_GROUP_WIDTH)
        gg = gi % groups_per_part
        u_g = u_parts[gi // groups_per_part][gg * POOL_GROUP_WIDTH:(gg + 1) * POOL_GROUP_WIDTH, :]
        tiles = [uprev[rows, :]] + [u_g[:, c * LANES:(c + 1) * LANES]
                                    for c in range(TOKENS // LANES)]
        uprev[rows, :] = tiles[-1]
        shift = 1
        while shift < w:
            shifted = _shift_tokens(tiles, shift)
            tiles = [a + b for a, b in zip(tiles, shifted)]
            shift *= 2
        win_sum = jnp.concatenate(tiles[1:], axis=1)
        inv_cnt = 1.0 / jnp.minimum(tok + 1, w).astype(f32)
        pooled = win_sum * inv_cnt - u_g
        y = _bdot(wgrp_ref[gi], pooled.astype(bf16)) * ps_ref[rows, :]
        op_sc[rows, :] = y.astype(bf16)

    key_pos = lax.broadcasted_iota(jnp.int32, (MOBA_BLOCK, TOKENS), 0)
    qry_pos = lax.broadcasted_iota(jnp.int32, (MOBA_BLOCK, TOKENS), 1)
    causal = key_pos <= qry_pos

    def score_head(j, slot, h, own=False):
        s = _bdot(kbuf[j, h], qbuf[h])
        if own:
            s = jnp.where(causal, s, MASKED)
        s_ref[slot, h] = s
        m_new = jnp.max(s, axis=0, keepdims=True)
        if not own:
            m_old = m_ref[1 - slot, h:h + 1, :]
            m_new = jnp.maximum(m_old, m_new)
            alpha_ref[slot, h:h + 1, :] = jnp.exp2(m_old - m_new)
        m_ref[slot, h:h + 1, :] = m_new

    def value_head(j, slot, h, own=False):
        p = jnp.exp2(s_ref[slot, h] - m_ref[slot, h:h + 1, :]).astype(bf16)
        pv = _bdot(vbuf[j, h], p)
        acc_ref[h] = pv if own else acc_ref[h] * alpha_ref[slot, h:h + 1, :] + pv

    def score_pass(j, slot, own=False):
        for h in range(HEADS):
            score_head(j, slot, h, own)

    def value_pass(j, slot, own=False):
        for h in range(HEADS):
            value_head(j, slot, h, own)

    score_pass(i, 1, own=True)

    fT = f_sc[...]
    r2_prev = r2_sc[0:1, :]
    for c in range(D_MODEL // OUT_CHUNK):
        chunk = slice(c * OUT_CHUNK, (c + 1) * OUT_CHUNK)
        for rows, y in zip(_halves(chunk), _wdot2(wff2_ref, chunk, fT)):
            o_ref[:, rows] = (x1_sc[rows, :] + y * r2_prev).T

    value_pass(i, 1, own=True)

    n_past = jnp.where(n == n_tiles, 0, i)

    def visit(blocks):
        score_pass(blocks[0], 0)
        for k in range(1, len(blocks)):
            for h in range(HEADS):
                score_head(blocks[k], k % 2, h)
                value_head(blocks[k - 1], (k - 1) % 2, h)
        value_pass(blocks[-1], (len(blocks) - 1) % 2)

    n_quad = 4
    has_quad = n_past >= n_quad

    @pl.when(has_quad)
    def _first_four():
        visit(list(range(n_quad)))

    first = jnp.where(has_quad, n_quad, 0)

    def past_pair(jj, carry):
        visit([first + 2 * jj, first + 2 * jj + 1])
        return carry

    lax.fori_loop(0, (n_past - first) // 2, past_pair, 0)

    @pl.when(n_past % 2 == 1)
    def _last_past_block():
        score_pass(n_past - 1, 0)
        value_pass(n_past - 1, 0)

    for h in range(HEADS):
        o_h = acc_ref[h, :HEAD_DIM, :] * (1.0 / acc_ref[h, HEAD_DIM:HEAD_DIM + 1, :])
        o_sc[h * HEAD_DIM:(h + 1) * HEAD_DIM, :] = o_h.astype(bf16)

    upa_parts = _wdot2(wupa_ref, all_rows, o_sc[...])
    upp_parts = _wdot2(wupp_ref, all_rows, op_sc[...])
    for rows, upa, upp in zip(_halves(all_rows), upa_parts, upp_parts):
        mg_sc[rows, :] = (sga_sc[rows, :] * upa + sgp_sc[rows, :] * upp).astype(bf16)
    sumsq = jnp.zeros((1, TOKENS), f32)
    for rows, d in zip(_halves(all_rows), _wdot2(wout_ref, all_rows, mg_sc[...])):
        x1 = xT[rows, :] + d
        x1_sc[rows, :] = x1
        h2_sc[rows, :] = (x1 * gmlp_ref[rows, :]).astype(bf16)
        sumsq = sumsq + jnp.sum(x1 * x1, axis=0, keepdims=True)
    r = lax.rsqrt(sumsq * (1.0 / D_MODEL) + RMS_EPS)
    r2_sc[...] = jnp.broadcast_to(r * r, r2_sc.shape)


def _bcast_cols(v, n):
    return jnp.broadcast_to(v.astype(jnp.float32)[:, None], (v.shape[0], n))


@jax.jit
def kernel(x, norm_mix, w_in, q_norm, k_norm, w_pool_grp, pool_scale, w_up_attn, w_up_pool,
           w_out, norm_mlp, w_ff1, w_ff2):
    batch, seq, d_model = x.shape
    assert d_model == D_MODEL and seq % TOKENS == 0 and w_in.shape[0] == 1
    n_blocks = seq // TOKENS
    assert n_blocks <= BIAS_ROWS
    bf16 = jnp.bfloat16

    half = jnp.arange(ROPE_HALF, dtype=jnp.float32)
    inv_freq = ROPE_THETA ** (-half / ROPE_HALF)
    ang = inv_freq[:, None] * jnp.arange(seq).astype(jnp.float32)[None, :]

    n_tiles = batch * n_blocks

    def tile(n):
        t = jnp.clip(n, 0, n_tiles - 1)
        return t // n_blocks, t % n_blocks

    whole = pl.BlockSpec(memory_space=pltpu.VMEM)
    in_hbm = pl.BlockSpec(memory_space=pl.ANY)
    weights = (w_in, w_up_attn, w_up_pool, w_out, w_ff1, w_ff2)
    x_spec = pl.BlockSpec((None, TOKENS, D_MODEL), lambda n: (*tile(n), 0))
    out_spec = pl.BlockSpec((None, TOKENS, D_MODEL), lambda n: (*tile(n - 1), 0))
    rope_spec = pl.BlockSpec((2 * ROPE_HALF, TOKENS), lambda n: (0, tile(n)[1]))
    gains = jnp.concatenate([norm_mix[0], norm_mlp[0], q_norm[0], k_norm[0], pool_scale[0]])

    return pl.pallas_call(
        _layer_kernel,
        grid=(n_tiles + 1,),
        in_specs=[x_spec, rope_spec, whole, whole] + [in_hbm] * len(weights),
        out_specs=out_spec,
        out_shape=jax.ShapeDtypeStruct(x.shape, x.dtype),
        scratch_shapes=[
            *[pltpu.VMEM((w.shape[2], w.shape[1]), bf16) for w in weights],
            pltpu.SemaphoreType.DMA((2 * HEADS,)),
            pltpu.VMEM((n_blocks, HEADS, MOBA_BLOCK, KQ_DIM), bf16),
            pltpu.VMEM((n_blocks, HEADS, V_ROWS, MOBA_BLOCK), bf16),
            pltpu.VMEM((HEADS, 2 * BIAS_ROWS, KQ_DIM), jnp.float32),
            pltpu.VMEM((POOL_WIDTH, LANES), jnp.float32),
            pltpu.VMEM((HEADS, KQ_DIM, TOKENS), bf16),
            pltpu.VMEM((2, HEADS, TOKENS), jnp.float32),
            pltpu.VMEM((2, HEADS, TOKENS), jnp.float32),
            pltpu.VMEM((HEADS, V_ROWS, TOKENS), jnp.float32),
            pltpu.VMEM((2, HEADS, MOBA_BLOCK, TOKENS), jnp.float32),
            pltpu.VMEM((ATTN_WIDTH, TOKENS), bf16),
            pltpu.VMEM((POOL_WIDTH, TOKENS), bf16),
            pltpu.VMEM((D_MODEL, TOKENS), jnp.float32),
            pltpu.VMEM((D_MODEL, TOKENS), jnp.float32),
            pltpu.VMEM((D_MODEL, TOKENS), bf16),
            pltpu.VMEM((D_MODEL, TOKENS), jnp.float32),
            pltpu.VMEM((D_MODEL, TOKENS), bf16),
            pltpu.VMEM((16, TOKENS), jnp.float32),
            pltpu.VMEM((D_FF, TOKENS), bf16),
        ],
        compiler_params=pltpu.CompilerParams(
            dimension_semantics=("arbitrary",),
            vmem_limit_bytes=VMEM_LIMIT_BYTES),
        name="moba_pool_layer",
    )(x, jnp.concatenate([jnp.cos(ang), jnp.sin(ang)]), _bcast_cols(gains, TOKENS),
      jnp.swapaxes(w_pool_grp[0], 1, 2).astype(bf16), *weights)
```

```python
import jax
import jax.numpy as jnp
from jax import lax
from jax.experimental import pallas as pl
from jax.experimental.pallas import tpu as pltpu

D_MODEL = 1024
HEADS = 8
HEAD_DIM = 64
ATTN_WIDTH = HEADS * HEAD_DIM
MOBA_BLOCK = 256
MOBA_TOPK = 3
ROPE_THETA = 500000.0
ROPE_HALF = HEAD_DIM // 8
POOL_WINDOWS = (2, 4, 8, 16)
POOL_GROUP_WIDTH = 128
POOL_WIDTH = len(POOL_WINDOWS) * POOL_GROUP_WIDTH
D_FF = 4 * D_MODEL
RMS_EPS = 1e-6

LANES = 128
TOKENS = MOBA_BLOCK
MASKED = -1e30
KQ_DIM = LANES
BIAS_ROWS = 8
V_ROWS = HEAD_DIM + 16
Q_SCALE = HEAD_DIM ** -0.5 * 1.4426950408889634
FF_CHUNK = 1024
OUT_CHUNK = 512
VMEM_LIMIT_BYTES = 61 * 1024 * 1024

_Q0, _K0, _V0, _U0 = 0, ATTN_WIDTH, 2 * ATTN_WIDTH, 3 * ATTN_WIDTH
_GA0 = 3 * ATTN_WIDTH + POOL_WIDTH
_GP0 = _GA0 + D_MODEL
IN_WIDTH = _GP0 + D_MODEL
_GAIN_ROWS = (0, D_MODEL, 2 * D_MODEL, 2 * D_MODEL + HEAD_DIM, 2 * D_MODEL + 2 * HEAD_DIM,
              2 * D_MODEL + 2 * HEAD_DIM + POOL_WIDTH)


def _bdot(a, b):
    return jnp.dot(a, b, preferred_element_type=jnp.float32)


def _wdot2(w_ref, rows, act, cols=slice(None)):
    mid = (rows.start + rows.stop) // 2
    return [_bdot(w_ref[rows.start:mid, cols], act), _bdot(w_ref[mid:rows.stop, cols], act)]


def _halves(rows):
    mid = (rows.start + rows.stop) // 2
    return [slice(rows.start, mid), slice(mid, rows.stop)]


def _sigmoid(x):
    return 0.5 * jnp.tanh(0.5 * x) + 0.5


def _rms_scale(t):
    return lax.rsqrt(jnp.mean(t * t, axis=0, keepdims=True) + RMS_EPS)


def _head_norm_rope(t, gain, cos, sin):
    y = t * _rms_scale(t) * gain
    x1 = y[0:ROPE_HALF, :]
    x2 = y[ROPE_HALF:2 * ROPE_HALF, :]
    return jnp.concatenate(
        [x1 * cos - x2 * sin, x2 * cos + x1 * sin, y[2 * ROPE_HALF:, :]], axis=0)


def _shift_tokens(tiles, shift):
    lane = lax.broadcasted_iota(jnp.int32, tiles[0].shape, 1)
    rolled = [pltpu.roll(t, shift, axis=1) for t in tiles]
    out = [rolled[0]]
    for c in range(1, len(tiles)):
        out.append(jnp.where(lane < shift, rolled[c - 1], rolled[c]))
    return out


def _load_weights(pairs, stage, sems):
    side = stage.shape[-1]
    slots_per_row = stage.shape[1]
    n_slots = stage.shape[0] * slots_per_row
    tiles = [(src, dst, k0, n0)
             for src, dst in pairs
             for k0 in range(0, dst.shape[1], side)
             for n0 in range(0, dst.shape[0], side)]

    def slot_of(t):
        slot = t % n_slots
        return slot, stage.at[slot // slots_per_row, slot % slots_per_row]

    def copy(t):
        src, _, k0, n0 = tiles[t]
        slot, buf = slot_of(t)
        return pltpu.make_async_copy(src.at[0, pl.ds(k0, side), pl.ds(n0, side)], buf, sems.at[slot])

    for t in range(min(n_slots, len(tiles))):
        copy(t).start()
    group = 4
    for t0 in range(0, len(tiles), group):
        ts = range(t0, min(t0 + group, len(tiles)))
        for t in ts:
            copy(t).wait()
        for t in ts:
            _, dst, k0, n0 = tiles[t]
            dst[n0:n0 + side, k0:k0 + side] = slot_of(t)[1][...].T.astype(dst.dtype)
        for t in ts:
            if t + n_slots < len(tiles):
                copy(t + n_slots).start()


def _layer_kernel(x_ref, rope_ref, gains_ref, wgrp_ref,
                  win_hbm, wupa_hbm, wupp_hbm, wout_hbm, wff1_hbm, wff2_hbm,
                  o_ref,
                  win_ref, wupa_ref, wupp_ref, wout_ref, wff1_ref, wff2_ref, wsem,
                  kbuf, vbuf, kmbuf, uprev, qbuf, m_ref, alpha_ref, acc_ref, s_ref,
                  o_sc, op_sc, sga_sc, sgp_sc, mg_sc, x1_sc, h2_sc, r2_sc, f_sc):
    f32, bf16 = jnp.float32, jnp.bfloat16
    gmix_ref, gmlp_ref, qn_ref, kn_ref, ps_ref = (
        gains_ref.at[lo:hi] for lo, hi in zip(_GAIN_ROWS[:-1], _GAIN_ROWS[1:]))
    n = pl.program_id(0)
    n_tiles = pl.num_programs(0) - 1
    n_blocks = kbuf.shape[0]
    i = lax.rem(jnp.minimum(n, n_tiles - 1), n_blocks)

    @pl.when(n == 0)
    def _first_step():
        _load_weights([(win_hbm, win_ref), (wupa_hbm, wupa_ref), (wupp_hbm, wupp_ref),
                       (wout_hbm, wout_ref), (wff1_hbm, wff1_ref), (wff2_hbm, wff2_ref)],
                      s_ref, wsem)
        x1_sc[...] = jnp.zeros_like(x1_sc)
        h2_sc[...] = jnp.zeros_like(h2_sc)
        r2_sc[...] = jnp.zeros_like(r2_sc)

    @pl.when(i == 0)
    def _start_of_sequence():
        kmbuf[...] = jnp.zeros_like(kmbuf)
        uprev[...] = jnp.zeros_like(uprev)

    h2T = h2_sc[...]
    for c in range(D_FF // FF_CHUNK):
        chunk = slice(c * FF_CHUNK, (c + 1) * FF_CHUNK)
        for rows, z in zip(_halves(chunk), _wdot2(wff1_ref, chunk, h2T)):
            f = jnp.maximum(z, 0.0)
            f_sc[rows, :] = (f * f).astype(bf16)

    xT = x_ref[...].T
    hT = (xT * _rms_scale(xT) * gmix_ref[...]).astype(bf16)

    def proj2(lo, width):
        return _wdot2(win_ref, slice(lo, lo + width), hT)

    def head_rows(parts, h):
        per_part = HEADS // 2
        hh = h % per_part
        return parts[h // per_part][hh * HEAD_DIM:(hh + 1) * HEAD_DIM, :]

    cos = rope_ref[:ROPE_HALF, :]
    sin = rope_ref[ROPE_HALF:, :]

    k_parts = proj2(_K0, ATTN_WIDTH)
    q_parts = proj2(_Q0, ATTN_WIDTH)
    v_parts = proj2(_V0, ATTN_WIDTH)
    u_parts = proj2(_U0, POOL_WIDTH)
    all_rows = slice(0, D_MODEL)
    for rows, ga, gp in zip(_halves(all_rows), proj2(_GA0, D_MODEL), proj2(_GP0, D_MODEL)):
        sga_sc[rows, :] = _sigmoid(ga)
        sgp_sc[rows, :] = _sigmoid(gp)
    pad_row = lax.broadcasted_iota(jnp.int32, (KQ_DIM - HEAD_DIM, TOKENS), 0)
    own_block = jnp.where(pad_row == i, 1.0, 0.0)
    ones_row = jnp.where(
        lax.broadcasted_iota(jnp.int32, (V_ROWS - HEAD_DIM, TOKENS), 0) == 0, 1.0, 0.0)
    for h in range(HEADS):
        k_h = _head_norm_rope(head_rows(k_parts, h), kn_ref[...], cos, sin)
        k_tok = jnp.concatenate([k_h, own_block], axis=0).T
        kbuf[i, h] = k_tok.astype(bf16)
        kmean = jnp.sum(k_tok, axis=0, keepdims=True) * (1.0 / MOBA_BLOCK)
        kmbuf[h, pl.ds(i, 1), :] = kmean
        vbuf[i, h, :HEAD_DIM, :] = head_rows(v_parts, h).astype(bf16)
        vbuf[i, h, HEAD_DIM:, :] = ones_row.astype(bf16)

    blk = lax.broadcasted_iota(jnp.int32, (BIAS_ROWS, TOKENS), 0)
    fully_past = blk < i
    for h in range(HEADS):
        q_h = _head_norm_rope(head_rows(q_parts, h), qn_ref[...], cos, sin)
        q_h = (q_h * Q_SCALE).astype(bf16)
        gate = _bdot(kmbuf[h, :, :HEAD_DIM].astype(bf16), q_h)[:BIAS_ROWS, :]
        g = jnp.where(fully_past, gate, -jnp.inf)
        rank = jnp.zeros((BIAS_ROWS, TOKENS), jnp.int32)
        for jp in range(n_blocks - 1):
            g_jp = g[jp:jp + 1, :]
            ahead = jnp.logical_or(g_jp > g, jnp.logical_and(g_jp == g, blk > jp))
            rank = rank + ahead.astype(jnp.int32)
        chosen = jnp.logical_and(rank < MOBA_TOPK, fully_past)
        bias = jnp.where(jnp.logical_or(chosen, blk == i), 0.0, MASKED)
        zeros = jnp.zeros((KQ_DIM - HEAD_DIM - BIAS_ROWS, TOKENS), f32)
        qbuf[h, :HEAD_DIM, :] = q_h
        qbuf[h, HEAD_DIM:, :] = jnp.concatenate([bias, zeros], axis=0).astype(bf16)

    tok = i * TOKENS + lax.broadcasted_iota(jnp.int32, (1, TOKENS), 1)
    groups_per_part = len(POOL_WINDOWS) // 2
    for gi, w in enumerate(POOL_WINDOWS):
        rows = slice(gi * POOL_GROUP_WIDTH, (gi + 1) * POOL_GROUP_WIDTH)
        gg = gi % groups_per_part
        u_g = u_parts[gi // groups_per_part][gg * POOL_GROUP_WIDTH:(gg + 1) * POOL_GROUP_WIDTH, :]
        tiles = [uprev[rows, :]] + [u_g[:, c * LANES:(c + 1) * LANES]
                                    for c in range(TOKENS // LANES)]
        uprev[rows, :] = tiles[-1]
        shift = 1
        while shift < w:
            shifted = _shift_tokens(tiles, shift)
            tiles = [a + b for a, b in zip(tiles, shifted)]
            shift *= 2
        win_sum = jnp.concatenate(tiles[1:], axis=1)
        inv_cnt = 1.0 / jnp.minimum(tok + 1, w).astype(f32)
        pooled = win_sum * inv_cnt - u_g
        y = _bdot(wgrp_ref[gi], pooled.astype(bf16)) * ps_ref[rows, :]
        op_sc[rows, :] = y.astype(bf16)

    key_pos = lax.broadcasted_iota(jnp.int32, (MOBA_BLOCK, TOKENS), 0)
    qry_pos = lax.broadcasted_iota(jnp.int32, (MOBA_BLOCK, TOKENS), 1)
    causal = key_pos <= qry_pos

    def score_head(j, slot, h, own=False):
        s = _bdot(kbuf[j, h], qbuf[h])
        if own:
            s = jnp.where(causal, s, MASKED)
        s_ref[slot, h] = s
        m_new = jnp.max(s, axis=0, keepdims=True)
        if not own:
            m_old = m_ref[1 - slot, h:h + 1, :]
            m_new = jnp.maximum(m_old, m_new)
            alpha_ref[slot, h:h + 1, :] = jnp.exp2(m_old - m_new)
        m_ref[slot, h:h + 1, :] = m_new

    def value_head(j, slot, h, own=False):
        p = jnp.exp2(s_ref[slot, h] - m_ref[slot, h:h + 1, :]).astype(bf16)
        pv = _bdot(vbuf[j, h], p)
        acc_ref[h] = pv if own else acc_ref[h] * alpha_ref[slot, h:h + 1, :] + pv

    def score_pass(j, slot, own=False):
        for h in range(HEADS):
            score_head(j, slot, h, own)

    def value_pass(j, slot, own=False):
        for h in range(HEADS):
            value_head(j, slot, h, own)

    score_pass(i, 1, own=True)

    fT = f_sc[...]
    r2_prev = r2_sc[0:1, :]
    for c in range(D_MODEL // OUT_CHUNK):
        chunk = slice(c * OUT_CHUNK, (c + 1) * OUT_CHUNK)
        for rows, y in zip(_halves(chunk), _wdot2(wff2_ref, chunk, fT)):
            o_ref[:, rows] = (x1_sc[rows, :] + y * r2_prev).T

    value_pass(i, 1, own=True)

    n_past = jnp.where(n == n_tiles, 0, i)

    def visit(blocks):
        score_pass(blocks[0], 0)
        for k in range(1, len(blocks)):
            for h in range(HEADS):
                score_head(blocks[k], k % 2, h)
                value_head(blocks[k - 1], (k - 1) % 2, h)
        value_pass(blocks[-1], (len(blocks) - 1) % 2)

    static_from = 4
    for count in range(static_from, n_blocks):
        pl.when(n_past == count)(lambda count=count: visit(list(range(count))))
    n_loop = jnp.where(n_past >= static_from, 0, n_past)

    def past_pair(jj, carry):
        visit([2 * jj, 2 * jj + 1])
        return carry

    lax.fori_loop(0, n_loop // 2, past_pair, 0)

    @pl.when(n_loop % 2 == 1)
    def _last_past_block():
        visit([n_loop - 1])

    for h in range(HEADS):
        o_h = acc_ref[h, :HEAD_DIM, :] * (1.0 / acc_ref[h, HEAD_DIM:HEAD_DIM + 1, :])
        o_sc[h * HEAD_DIM:(h + 1) * HEAD_DIM, :] = o_h.astype(bf16)

    upa_parts = _wdot2(wupa_ref, all_rows, o_sc[...])
    upp_parts = _wdot2(wupp_ref, all_rows, op_sc[...])
    for rows, upa, upp in zip(_halves(all_rows), upa_parts, upp_parts):
        mg_sc[rows, :] = (sga_sc[rows, :] * upa + sgp_sc[rows, :] * upp).astype(bf16)
    sumsq = jnp.zeros((1, TOKENS), f32)
    for rows, d in zip(_halves(all_rows), _wdot2(wout_ref, all_rows, mg_sc[...])):
        x1 = xT[rows, :] + d
        x1_sc[rows, :] = x1
        h2_sc[rows, :] = (x1 * gmlp_ref[rows, :]).astype(bf16)
        sumsq = sumsq + jnp.sum(x1 * x1, axis=0, keepdims=True)
    r = lax.rsqrt(sumsq * (1.0 / D_MODEL) + RMS_EPS)
    r2_sc[...] = jnp.broadcast_to(r * r, r2_sc.shape)


def _bcast_cols(v, n):
    return jnp.broadcast_to(v.astype(jnp.float32)[:, None], (v.shape[0], n))


@jax.jit
def kernel(x, norm_mix, w_in, q_norm, k_norm, w_pool_grp, pool_scale, w_up_attn, w_up_pool,
           w_out, norm_mlp, w_ff1, w_ff2):
    batch, seq, d_model = x.shape
    assert d_model == D_MODEL and seq % TOKENS == 0 and w_in.shape[0] == 1
    n_blocks = seq // TOKENS
    assert n_blocks <= BIAS_ROWS
    bf16 = jnp.bfloat16

    half = jnp.arange(ROPE_HALF, dtype=jnp.float32)
    inv_freq = ROPE_THETA ** (-half / ROPE_HALF)
    ang = inv_freq[:, None] * jnp.arange(seq).astype(jnp.float32)[None, :]

    n_tiles = batch * n_blocks

    def tile(n):
        t = jnp.clip(n, 0, n_tiles - 1)
        return t // n_blocks, t % n_blocks

    whole = pl.BlockSpec(memory_space=pltpu.VMEM)
    in_hbm = pl.BlockSpec(memory_space=pl.ANY)
    weights = (w_in, w_up_attn, w_up_pool, w_out, w_ff1, w_ff2)
    x_spec = pl.BlockSpec((None, TOKENS, D_MODEL), lambda n: (*tile(n), 0))
    out_spec = pl.BlockSpec((None, TOKENS, D_MODEL), lambda n: (*tile(n - 1), 0))
    rope_spec = pl.BlockSpec((2 * ROPE_HALF, TOKENS), lambda n: (0, tile(n)[1]))
    gains = jnp.concatenate([norm_mix[0], norm_mlp[0], q_norm[0], k_norm[0], pool_scale[0]])

    return pl.pallas_call(
        _layer_kernel,
        grid=(n_tiles + 1,),
        in_specs=[x_spec, rope_spec, whole, whole] + [in_hbm] * len(weights),
        out_specs=out_spec,
        out_shape=jax.ShapeDtypeStruct(x.shape, x.dtype),
        scratch_shapes=[
            *[pltpu.VMEM((w.shape[2], w.shape[1]), bf16) for w in weights],
            pltpu.SemaphoreType.DMA((2 * HEADS,)),
            pltpu.VMEM((n_blocks, HEADS, MOBA_BLOCK, KQ_DIM), bf16),
            pltpu.VMEM((n_blocks, HEADS, V_ROWS, MOBA_BLOCK), bf16),
            pltpu.VMEM((HEADS, 2 * BIAS_ROWS, KQ_DIM), jnp.float32),
            pltpu.VMEM((POOL_WIDTH, LANES), jnp.float32),
            pltpu.VMEM((HEADS, KQ_DIM, TOKENS), bf16),
            pltpu.VMEM((2, HEADS, TOKENS), jnp.float32),
            pltpu.VMEM((2, HEADS, TOKENS), jnp.float32),
            pltpu.VMEM((HEADS, V_ROWS, TOKENS), jnp.float32),
            pltpu.VMEM((2, HEADS, MOBA_BLOCK, TOKENS), jnp.float32),
            pltpu.VMEM((ATTN_WIDTH, TOKENS), bf16),
            pltpu.VMEM((POOL_WIDTH, TOKENS), bf16),
            pltpu.VMEM((D_MODEL, TOKENS), jnp.float32),
            pltpu.VMEM((D_MODEL, TOKENS), jnp.float32),
            pltpu.VMEM((D_MODEL, TOKENS), bf16),
            pltpu.VMEM((D_MODEL, TOKENS), jnp.float32),
            pltpu.VMEM((D_MODEL, TOKENS), bf16),
            pltpu.VMEM((16, TOKENS), jnp.float32),
            pltpu.VMEM((D_FF, TOKENS), bf16),
        ],
        compiler_params=pltpu.CompilerParams(
            dimension_semantics=("arbitrary",),
            vmem_limit_bytes=VMEM_LIMIT_BYTES),
        name="moba_pool_layer",
    )(x, jnp.concatenate([jnp.cos(ang), jnp.sin(ang)]), _bcast_cols(gains, TOKENS),
      jnp.swapaxes(w_pool_grp[0], 1, 2).astype(bf16), *weights)
```

```python
import jax
import jax.numpy as jnp
from jax import lax
from jax.experimental import pallas as pl
from jax.experimental.pallas import tpu as pltpu

D_MODEL = 1024
HEADS = 8
HEAD_DIM = 64
ATTN_WIDTH = HEADS * HEAD_DIM
MOBA_BLOCK = 256
MOBA_TOPK = 3
ROPE_THETA = 500000.0
ROPE_HALF = HEAD_DIM // 8
POOL_WINDOWS = (2, 4, 8, 16)
POOL_GROUP_WIDTH = 128
POOL_WIDTH = len(POOL_WINDOWS) * POOL_GROUP_WIDTH
D_FF = 4 * D_MODEL
RMS_EPS = 1e-6

LANES = 128
TOKENS = MOBA_BLOCK
MASKED = -1e30
KQ_DIM = LANES
BIAS_ROWS = 8
V_ROWS = HEAD_DIM + 16
Q_SCALE = HEAD_DIM ** -0.5 * 1.4426950408889634
FF_CHUNK = 1024
OUT_CHUNK = 512
VMEM_LIMIT_BYTES = 61 * 1024 * 1024

_Q0, _K0, _V0, _U0 = 0, ATTN_WIDTH, 2 * ATTN_WIDTH, 3 * ATTN_WIDTH
_GA0 = 3 * ATTN_WIDTH + POOL_WIDTH
_GP0 = _GA0 + D_MODEL
IN_WIDTH = _GP0 + D_MODEL
_GAIN_ROWS = (0, D_MODEL, 2 * D_MODEL, 2 * D_MODEL + HEAD_DIM, 2 * D_MODEL + 2 * HEAD_DIM,
              2 * D_MODEL + 2 * HEAD_DIM + POOL_WIDTH)


def _bdot(a, b):
    return jnp.dot(a, b, preferred_element_type=jnp.float32)


def _wdot2(w_ref, rows, act, cols=slice(None)):
    mid = (rows.start + rows.stop) // 2
    return [_bdot(w_ref[rows.start:mid, cols], act), _bdot(w_ref[mid:rows.stop, cols], act)]


def _halves(rows):
    mid = (rows.start + rows.stop) // 2
    return [slice(rows.start, mid), slice(mid, rows.stop)]


def _sigmoid(x):
    return 0.5 * jnp.tanh(0.5 * x) + 0.5


def _rms_scale(t):
    return lax.rsqrt(jnp.mean(t * t, axis=0, keepdims=True) + RMS_EPS)


def _head_norm_rope(t, gain, cos, sin):
    y = t * _rms_scale(t) * gain
    x1 = y[0:ROPE_HALF, :]
    x2 = y[ROPE_HALF:2 * ROPE_HALF, :]
    return jnp.concatenate(
        [x1 * cos - x2 * sin, x2 * cos + x1 * sin, y[2 * ROPE_HALF:, :]], axis=0)


def _shift_tokens(tiles, shift):
    lane = lax.broadcasted_iota(jnp.int32, tiles[0].shape, 1)
    rolled = [pltpu.roll(t, shift, axis=1) for t in tiles]
    out = [rolled[0]]
    for c in range(1, len(tiles)):
        out.append(jnp.where(lane < shift, rolled[c - 1], rolled[c]))
    return out


def _load_weights(pairs, stage, sems):
    side = stage.shape[-1]
    slots_per_row = stage.shape[1]
    n_slots = stage.shape[0] * slots_per_row
    tiles = [(src, dst, k0, n0)
             for src, dst in pairs
             for k0 in range(0, dst.shape[1], side)
             for n0 in range(0, dst.shape[0], side)]

    def slot_of(t):
        slot = t % n_slots
        return slot, stage.at[slot // slots_per_row, slot % slots_per_row]

    def copy(t):
        src, _, k0, n0 = tiles[t]
        slot, buf = slot_of(t)
        return pltpu.make_async_copy(src.at[0, pl.ds(k0, side), pl.ds(n0, side)], buf, sems.at[slot])

    for t in range(min(n_slots, len(tiles))):
        copy(t).start()
    group = 4
    for t0 in range(0, len(tiles), group):
        ts = range(t0, min(t0 + group, len(tiles)))
        for t in ts:
            copy(t).wait()
        for t in ts:
            _, dst, k0, n0 = tiles[t]
            dst[n0:n0 + side, k0:k0 + side] = slot_of(t)[1][...].T.astype(dst.dtype)
        for t in ts:
            if t + n_slots < len(tiles):
                copy(t + n_slots).start()


def _layer_kernel(x_ref, rope_ref, gains_ref, wgrp_ref,
                  win_hbm, wupa_hbm, wupp_hbm, wout_hbm, wff1_hbm, wff2_hbm,
                  o_ref,
                  win_ref, wupa_ref, wupp_ref, wout_ref, wff1_ref, wff2_ref, wsem,
                  kbuf, vbuf, kmbuf, uprev, qbuf, m_ref, alpha_ref, acc_ref, s_ref,
                  o_sc, op_sc, sga_sc, sgp_sc, mg_sc, x1_sc, h2_sc, r2_sc, f_sc):
    f32, bf16 = jnp.float32, jnp.bfloat16
    gmix_ref, gmlp_ref, qn_ref, kn_ref, ps_ref = (
        gains_ref.at[lo:hi] for lo, hi in zip(_GAIN_ROWS[:-1], _GAIN_ROWS[1:]))
    n = pl.program_id(0)
    n_tiles = pl.num_programs(0) - 1
    n_blocks = kbuf.shape[0]
    i = lax.rem(jnp.minimum(n, n_tiles - 1), n_blocks)

    @pl.when(n == 0)
    def _first_step():
        _load_weights([(win_hbm, win_ref), (wupa_hbm, wupa_ref), (wupp_hbm, wupp_ref),
                       (wout_hbm, wout_ref), (wff1_hbm, wff1_ref), (wff2_hbm, wff2_ref)],
                      s_ref, wsem)
        x1_sc[...] = jnp.zeros_like(x1_sc)
        h2_sc[...] = jnp.zeros_like(h2_sc)
        r2_sc[...] = jnp.zeros_like(r2_sc)

    @pl.when(i == 0)
    def _start_of_sequence():
        kmbuf[...] = jnp.zeros_like(kmbuf)
        uprev[...] = jnp.zeros_like(uprev)

    h2T = h2_sc[...]
    for c in range(D_FF // FF_CHUNK):
        chunk = slice(c * FF_CHUNK, (c + 1) * FF_CHUNK)
        for rows, z in zip(_halves(chunk), _wdot2(wff1_ref, chunk, h2T)):
            f = jnp.maximum(z, 0.0)
            f_sc[rows, :] = (f * f).astype(bf16)

    xT = x_ref[...].T
    hT = (xT * _rms_scale(xT) * gmix_ref[...]).astype(bf16)

    def proj2(lo, width):
        return _wdot2(win_ref, slice(lo, lo + width), hT)

    def head_rows(parts, h):
        per_part = HEADS // 2
        hh = h % per_part
        return parts[h // per_part][hh * HEAD_DIM:(hh + 1) * HEAD_DIM, :]

    cos = rope_ref[:ROPE_HALF, :]
    sin = rope_ref[ROPE_HALF:, :]

    k_parts = proj2(_K0, ATTN_WIDTH)
    q_parts = proj2(_Q0, ATTN_WIDTH)
    v_parts = proj2(_V0, ATTN_WIDTH)
    u_parts = proj2(_U0, POOL_WIDTH)
    all_rows = slice(0, D_MODEL)
    for rows, ga, gp in zip(_halves(all_rows), proj2(_GA0, D_MODEL), proj2(_GP0, D_MODEL)):
        sga_sc[rows, :] = _sigmoid(ga)
        sgp_sc[rows, :] = _sigmoid(gp)
    pad_row = lax.broadcasted_iota(jnp.int32, (KQ_DIM - HEAD_DIM, TOKENS), 0)
    own_block = jnp.where(pad_row == i, 1.0, 0.0)
    ones_row = jnp.where(
        lax.broadcasted_iota(jnp.int32, (V_ROWS - HEAD_DIM, TOKENS), 0) == 0, 1.0, 0.0)
    for h in range(HEADS):
        k_h = _head_norm_rope(head_rows(k_parts, h), kn_ref[...], cos, sin)
        k_tok = jnp.concatenate([k_h, own_block], axis=0).T
        kbuf[i, h] = k_tok.astype(bf16)
        kmean = jnp.sum(k_tok, axis=0, keepdims=True) * (1.0 / MOBA_BLOCK)
        kmbuf[h, pl.ds(i, 1), :] = kmean
        vbuf[i, h, :HEAD_DIM, :] = head_rows(v_parts, h).astype(bf16)
        vbuf[i, h, HEAD_DIM:, :] = ones_row.astype(bf16)

    blk = lax.broadcasted_iota(jnp.int32, (BIAS_ROWS, TOKENS), 0)
    fully_past = blk < i
    for h in range(HEADS):
        q_h = _head_norm_rope(head_rows(q_parts, h), qn_ref[...], cos, sin)
        q_h = (q_h * Q_SCALE).astype(bf16)
        gate = _bdot(kmbuf[h, :, :HEAD_DIM].astype(bf16), q_h)[:BIAS_ROWS, :]
        g = jnp.where(fully_past, gate, -jnp.inf)
        rank = jnp.zeros((BIAS_ROWS, TOKENS), jnp.int32)
        for jp in range(n_blocks - 1):
            g_jp = g[jp:jp + 1, :]
            ahead = jnp.logical_or(g_jp > g, jnp.logical_and(g_jp == g, blk > jp))
            rank = rank + ahead.astype(jnp.int32)
        chosen = jnp.logical_and(rank < MOBA_TOPK, fully_past)
        bias = jnp.where(jnp.logical_or(chosen, blk == i), 0.0, MASKED)
        zeros = jnp.zeros((KQ_DIM - HEAD_DIM - BIAS_ROWS, TOKENS), f32)
        qbuf[h, :HEAD_DIM, :] = q_h
        qbuf[h, HEAD_DIM:, :] = jnp.concatenate([bias, zeros], axis=0).astype(bf16)

    tok = i * TOKENS + lax.broadcasted_iota(jnp.int32, (1, TOKENS), 1)
    groups_per_part = len(POOL_WINDOWS) // 2
    for gi, w in enumerate(POOL_WINDOWS):
        rows = slice(gi * POOL_GROUP_WIDTH, (gi + 1) * POOL_GROUP_WIDTH)
        gg = gi % groups_per_part
        u_g = u_parts[gi // groups_per_part][gg * POOL_GROUP_WIDTH:(gg + 1) * POOL_GROUP_WIDTH, :]
        tiles = [uprev[rows, :]] + [u_g[:, c * LANES:(c + 1) * LANES]
                                    for c in range(TOKENS // LANES)]
        uprev[rows, :] = tiles[-1]
        shift = 1
        while shift < w:
            shifted = _shift_tokens(tiles, shift)
            tiles = [a + b for a, b in zip(tiles, shifted)]
            shift *= 2
        win_sum = jnp.concatenate(tiles[1:], axis=1)
        inv_cnt = 1.0 / jnp.minimum(tok + 1, w).astype(f32)
        pooled = win_sum * inv_cnt - u_g
        y = _bdot(wgrp_ref[gi], pooled.astype(bf16)) * ps_ref[rows, :]
        op_sc[rows, :] = y.astype(bf16)

    key_pos = lax.broadcasted_iota(jnp.int32, (MOBA_BLOCK, TOKENS), 0)
    qry_pos = lax.broadcasted_iota(jnp.int32, (MOBA_BLOCK, TOKENS), 1)
    causal = key_pos <= qry_pos

    def score_head(j, slot, h, own=False):
        s = _bdot(kbuf[j, h], qbuf[h])
        if own:
            s = jnp.where(causal, s, MASKED)
        s_ref[slot, h] = s
        m_new = jnp.max(s, axis=0, keepdims=True)
        if not own:
            m_old = m_ref[1 - slot, h:h + 1, :]
            m_new = jnp.maximum(m_old, m_new)
            alpha_ref[slot, h:h + 1, :] = jnp.exp2(m_old - m_new)
        m_ref[slot, h:h + 1, :] = m_new

    def value_head(j, slot, h, own=False):
        p = jnp.exp2(s_ref[slot, h] - m_ref[slot, h:h + 1, :]).astype(bf16)
        pv = _bdot(vbuf[j, h], p)
        acc_ref[h] = pv if own else acc_ref[h] * alpha_ref[slot, h:h + 1, :] + pv

    def score_pass(j, slot, own=False):
        for h in range(HEADS):
            score_head(j, slot, h, own)

    def value_pass(j, slot, own=False):
        for h in range(HEADS):
            value_head(j, slot, h, own)

    score_pass(i, 1, own=True)

    fT = f_sc[...]
    r2_prev = r2_sc[0:1, :]
    for c in range(D_MODEL // OUT_CHUNK):
        chunk = slice(c * OUT_CHUNK, (c + 1) * OUT_CHUNK)
        for rows, y in zip(_halves(chunk), _wdot2(wff2_ref, chunk, fT)):
            o_ref[:, rows] = (x1_sc[rows, :] + y * r2_prev).T

    value_pass(i, 1, own=True)

    n_past = jnp.where(n == n_tiles, 0, i)

    def visit(blocks):
        score_pass(blocks[0], 0)
        for k in range(1, len(blocks)):
            for h in range(HEADS):
                score_head(blocks[k], k % 2, h)
                value_head(blocks[k - 1], (k - 1) % 2, h)
        value_pass(blocks[-1], (len(blocks) - 1) % 2)

    for count in range(1, n_blocks):
        pl.when(n_past == count)(lambda count=count: visit(list(range(count))))

    for h in range(HEADS):
        o_h = acc_ref[h, :HEAD_DIM, :] * (1.0 / acc_ref[h, HEAD_DIM:HEAD_DIM + 1, :])
        o_sc[h * HEAD_DIM:(h + 1) * HEAD_DIM, :] = o_h.astype(bf16)

    upa_parts = _wdot2(wupa_ref, all_rows, o_sc[...])
    upp_parts = _wdot2(wupp_ref, all_rows, op_sc[...])
    for rows, upa, upp in zip(_halves(all_rows), upa_parts, upp_parts):
        mg_sc[rows, :] = (sga_sc[rows, :] * upa + sgp_sc[rows, :] * upp).astype(bf16)
    sumsq = jnp.zeros((1, TOKENS), f32)
    for rows, d in zip(_halves(all_rows), _wdot2(wout_ref, all_rows, mg_sc[...])):
        x1 = xT[rows, :] + d
        x1_sc[rows, :] = x1
        h2_sc[rows, :] = (x1 * gmlp_ref[rows, :]).astype(bf16)
        sumsq = sumsq + jnp.sum(x1 * x1, axis=0, keepdims=True)
    r = lax.rsqrt(sumsq * (1.0 / D_MODEL) + RMS_EPS)
    r2_sc[...] = jnp.broadcast_to(r * r, r2_sc.shape)


def _bcast_cols(v, n):
    return jnp.broadcast_to(v.astype(jnp.float32)[:, None], (v.shape[0], n))


@jax.jit
def kernel(x, norm_mix, w_in, q_norm, k_norm, w_pool_grp, pool_scale, w_up_attn, w_up_pool,
           w_out, norm_mlp, w_ff1, w_ff2):
    batch, seq, d_model = x.shape
    assert d_model == D_MODEL and seq % TOKENS == 0 and w_in.shape[0] == 1
    n_blocks = seq // TOKENS
    assert n_blocks <= BIAS_ROWS
    bf16 = jnp.bfloat16

    half = jnp.arange(ROPE_HALF, dtype=jnp.float32)
    inv_freq = ROPE_THETA ** (-half / ROPE_HALF)
    ang = inv_freq[:, None] * jnp.arange(seq).astype(jnp.float32)[None, :]

    n_tiles = batch * n_blocks

    def tile(n):
        t = jnp.clip(n, 0, n_tiles - 1)
        return t // n_blocks, t % n_blocks

    whole = pl.BlockSpec(memory_space=pltpu.VMEM)
    in_hbm = pl.BlockSpec(memory_space=pl.ANY)
    weights = (w_in, w_up_attn, w_up_pool, w_out, w_ff1, w_ff2)
    x_spec = pl.BlockSpec((None, TOKENS, D_MODEL), lambda n: (*tile(n), 0))
    out_spec = pl.BlockSpec((None, TOKENS, D_MODEL), lambda n: (*tile(n - 1), 0))
    rope_spec = pl.BlockSpec((2 * ROPE_HALF, TOKENS), lambda n: (0, tile(n)[1]))
    gains = jnp.concatenate([norm_mix[0], norm_mlp[0], q_norm[0], k_norm[0], pool_scale[0]])

    return pl.pallas_call(
        _layer_kernel,
        grid=(n_tiles + 1,),
        in_specs=[x_spec, rope_spec, whole, whole] + [in_hbm] * len(weights),
        out_specs=out_spec,
        out_shape=jax.ShapeDtypeStruct(x.shape, x.dtype),
        scratch_shapes=[
            *[pltpu.VMEM((w.shape[2], w.shape[1]), bf16) for w in weights],
            pltpu.SemaphoreType.DMA((2 * HEADS,)),
            pltpu.VMEM((n_blocks, HEADS, MOBA_BLOCK, KQ_DIM), bf16),
            pltpu.VMEM((n_blocks, HEADS, V_ROWS, MOBA_BLOCK), bf16),
            pltpu.VMEM((HEADS, 2 * BIAS_ROWS, KQ_DIM), jnp.float32),
            pltpu.VMEM((POOL_WIDTH, LANES), jnp.float32),
            pltpu.VMEM((HEADS, KQ_DIM, TOKENS), bf16),
            pltpu.VMEM((2, HEADS, TOKENS), jnp.float32),
            pltpu.VMEM((2, HEADS, TOKENS), jnp.float32),
            pltpu.VMEM((HEADS, V_ROWS, TOKENS), jnp.float32),
            pltpu.VMEM((2, HEADS, MOBA_BLOCK, TOKENS), jnp.float32),
            pltpu.VMEM((ATTN_WIDTH, TOKENS), bf16),
            pltpu.VMEM((POOL_WIDTH, TOKENS), bf16),
            pltpu.VMEM((D_MODEL, TOKENS), jnp.float32),
            pltpu.VMEM((D_MODEL, TOKENS), jnp.float32),
            pltpu.VMEM((D_MODEL, TOKENS), bf16),
            pltpu.VMEM((D_MODEL, TOKENS), jnp.float32),
            pltpu.VMEM((D_MODEL, TOKENS), bf16),
            pltpu.VMEM((16, TOKENS), jnp.float32),
            pltpu.VMEM((D_FF, TOKENS), bf16),
        ],
        compiler_params=pltpu.CompilerParams(
            dimension_semantics=("arbitrary",),
            vmem_limit_bytes=VMEM_LIMIT_BYTES),
        name="moba_pool_layer",
    )(x, jnp.concatenate([jnp.cos(ang), jnp.sin(ang)]), _bcast_cols(gains, TOKENS),
      jnp.swapaxes(w_pool_grp[0], 1, 2).astype(bf16), *weights)
```

```python
import jax
import jax.numpy as jnp
from jax import lax
from jax.experimental import pallas as pl
from jax.experimental.pallas import tpu as pltpu

D_MODEL = 1024
HEADS = 8
HEAD_DIM = 64
ATTN_WIDTH = HEADS * HEAD_DIM
MOBA_BLOCK = 256
MOBA_TOPK = 3
ROPE_THETA = 500000.0
ROPE_HALF = HEAD_DIM // 8
POOL_WINDOWS = (2, 4, 8, 16)
POOL_GROUP_WIDTH = 128
POOL_WIDTH = len(POOL_WINDOWS) * POOL_GROUP_WIDTH
D_FF = 4 * D_MODEL
RMS_EPS = 1e-6

LANES = 128
TOKENS = MOBA_BLOCK
MASKED = -1e30
KQ_DIM = LANES
BIAS_ROWS = 8
V_ROWS = HEAD_DIM + 16
Q_SCALE = HEAD_DIM ** -0.5 * 1.4426950408889634
FF_CHUNK = 1024
OUT_CHUNK = 512
VMEM_LIMIT_BYTES = 61 * 1024 * 1024

_Q0, _K0, _V0, _U0 = 0, ATTN_WIDTH, 2 * ATTN_WIDTH, 3 * ATTN_WIDTH
_GA0 = 3 * ATTN_WIDTH + POOL_WIDTH
_GP0 = _GA0 + D_MODEL
IN_WIDTH = _GP0 + D_MODEL
_GAIN_ROWS = (0, D_MODEL, 2 * D_MODEL, 2 * D_MODEL + HEAD_DIM, 2 * D_MODEL + 2 * HEAD_DIM,
              2 * D_MODEL + 2 * HEAD_DIM + POOL_WIDTH)


def _bdot(a, b):
    return jnp.dot(a, b, preferred_element_type=jnp.float32)


def _wdot2(w_ref, rows, act, cols=slice(None)):
    mid = (rows.start + rows.stop) // 2
    return [_bdot(w_ref[rows.start:mid, cols], act), _bdot(w_ref[mid:rows.stop, cols], act)]


def _halves(rows):
    mid = (rows.start + rows.stop) // 2
    return [slice(rows.start, mid), slice(mid, rows.stop)]


def _sigmoid(x):
    return 0.5 * jnp.tanh(0.5 * x) + 0.5


def _rms_scale(t):
    return lax.rsqrt(jnp.mean(t * t, axis=0, keepdims=True) + RMS_EPS)


def _head_norm_rope(t, gain, cos, sin):
    y = t * _rms_scale(t) * gain
    x1 = y[0:ROPE_HALF, :]
    x2 = y[ROPE_HALF:2 * ROPE_HALF, :]
    return jnp.concatenate(
        [x1 * cos - x2 * sin, x2 * cos + x1 * sin, y[2 * ROPE_HALF:, :]], axis=0)


def _shift_tokens(tiles, shift):
    lane = lax.broadcasted_iota(jnp.int32, tiles[0].shape, 1)
    rolled = [pltpu.roll(t, shift, axis=1) for t in tiles]
    out = [rolled[0]]
    for c in range(1, len(tiles)):
        out.append(jnp.where(lane < shift, rolled[c - 1], rolled[c]))
    return out


def _load_weights(pairs, stage, sems):
    side = stage.shape[-1]
    slots_per_row = stage.shape[1]
    n_slots = stage.shape[0] * slots_per_row
    tiles = [(src, dst, k0, n0)
             for src, dst in pairs
             for k0 in range(0, dst.shape[1], side)
             for n0 in range(0, dst.shape[0], side)]

    def slot_of(t):
        slot = t % n_slots
        return slot, stage.at[slot // slots_per_row, slot % slots_per_row]

    def copy(t):
        src, _, k0, n0 = tiles[t]
        slot, buf = slot_of(t)
        return pltpu.make_async_copy(src.at[0, pl.ds(k0, side), pl.ds(n0, side)], buf, sems.at[slot])

    for t in range(min(n_slots, len(tiles))):
        copy(t).start()
    group = 4
    for t0 in range(0, len(tiles), group):
        ts = range(t0, min(t0 + group, len(tiles)))
        for t in ts:
            copy(t).wait()
        for t in ts:
            _, dst, k0, n0 = tiles[t]
            dst[n0:n0 + side, k0:k0 + side] = slot_of(t)[1][...].T.astype(dst.dtype)
        for t in ts:
            if t + n_slots < len(tiles):
                copy(t + n_slots).start()


def _layer_kernel(x_ref, rope_ref, gains_ref, wgrp_ref,
                  win_hbm, wupa_hbm, wupp_hbm, wout_hbm, wff1_hbm, wff2_hbm,
                  o_ref,
                  win_ref, wupa_ref, wupp_ref, wout_ref, wff1_ref, wff2_ref, wsem,
                  kbuf, vbuf, kmbuf, uprev, qbuf, m_ref, alpha_ref, acc_ref, s_ref,
                  o_sc, op_sc, sga_sc, sgp_sc, mg_sc, x1_sc, h2_sc, r2_sc, f_sc):
    f32, bf16 = jnp.float32, jnp.bfloat16
    gmix_ref, gmlp_ref, qn_ref, kn_ref, ps_ref = (
        gains_ref.at[lo:hi] for lo, hi in zip(_GAIN_ROWS[:-1], _GAIN_ROWS[1:]))
    n = pl.program_id(0)
    n_tiles = pl.num_programs(0) - 1
    n_blocks = kbuf.shape[0]
    i = lax.rem(jnp.minimum(n, n_tiles - 1), n_blocks)

    @pl.when(n == 0)
    def _first_step():
        _load_weights([(win_hbm, win_ref), (wupa_hbm, wupa_ref), (wupp_hbm, wupp_ref),
                       (wout_hbm, wout_ref), (wff1_hbm, wff1_ref), (wff2_hbm, wff2_ref)],
                      s_ref, wsem)
        x1_sc[...] = jnp.zeros_like(x1_sc)
        h2_sc[...] = jnp.zeros_like(h2_sc)
        r2_sc[...] = jnp.zeros_like(r2_sc)

    @pl.when(i == 0)
    def _start_of_sequence():
        kmbuf[...] = jnp.zeros_like(kmbuf)
        uprev[...] = jnp.zeros_like(uprev)

    h2T = h2_sc[...]
    for c in range(D_FF // FF_CHUNK):
        chunk = slice(c * FF_CHUNK, (c + 1) * FF_CHUNK)
        for rows, z in zip(_halves(chunk), _wdot2(wff1_ref, chunk, h2T)):
            f = jnp.maximum(z, 0.0)
            f_sc[rows, :] = (f * f).astype(bf16)

    xT = x_ref[...].T
    hT = (xT * _rms_scale(xT) * gmix_ref[...]).astype(bf16)

    def proj2(lo, width):
        return _wdot2(win_ref, slice(lo, lo + width), hT)

    def head_rows(parts, h):
        per_part = HEADS // 2
        hh = h % per_part
        return parts[h // per_part][hh * HEAD_DIM:(hh + 1) * HEAD_DIM, :]

    cos = rope_ref[:ROPE_HALF, :]
    sin = rope_ref[ROPE_HALF:, :]

    k_parts = proj2(_K0, ATTN_WIDTH)
    q_parts = proj2(_Q0, ATTN_WIDTH)
    v_parts = proj2(_V0, ATTN_WIDTH)
    u_parts = proj2(_U0, POOL_WIDTH)
    all_rows = slice(0, D_MODEL)
    for rows, ga, gp in zip(_halves(all_rows), proj2(_GA0, D_MODEL), proj2(_GP0, D_MODEL)):
        sga_sc[rows, :] = _sigmoid(ga)
        sgp_sc[rows, :] = _sigmoid(gp)
    pad_row = lax.broadcasted_iota(jnp.int32, (KQ_DIM - HEAD_DIM, TOKENS), 0)
    own_block = jnp.where(pad_row == i, 1.0, 0.0)
    ones_row = jnp.where(
        lax.broadcasted_iota(jnp.int32, (V_ROWS - HEAD_DIM, TOKENS), 0) == 0, 1.0, 0.0)
    for h in range(HEADS):
        k_h = _head_norm_rope(head_rows(k_parts, h), kn_ref[...], cos, sin)
        k_tok = jnp.concatenate([k_h, own_block], axis=0).T
        kbuf[i, h] = k_tok.astype(bf16)
        kmean = jnp.sum(k_tok, axis=0, keepdims=True) * (1.0 / MOBA_BLOCK)
        kmbuf[h, pl.ds(i, 1), :] = kmean
        vbuf[i, h, :HEAD_DIM, :] = head_rows(v_parts, h).astype(bf16)
        vbuf[i, h, HEAD_DIM:, :] = ones_row.astype(bf16)

    blk = lax.broadcasted_iota(jnp.int32, (BIAS_ROWS, TOKENS), 0)
    fully_past = blk < i
    for h in range(HEADS):
        q_h = _head_norm_rope(head_rows(q_parts, h), qn_ref[...], cos, sin)
        q_h = (q_h * Q_SCALE).astype(bf16)
        gate = _bdot(kmbuf[h, :, :HEAD_DIM].astype(bf16), q_h)[:BIAS_ROWS, :]
        g = jnp.where(fully_past, gate, -jnp.inf)
        rank = jnp.zeros((BIAS_ROWS, TOKENS), jnp.int32)
        for jp in range(n_blocks - 1):
            g_jp = g[jp:jp + 1, :]
            ahead = jnp.logical_or(g_jp > g, jnp.logical_and(g_jp == g, blk > jp))
            rank = rank + ahead.astype(jnp.int32)
        chosen = jnp.logical_and(rank < MOBA_TOPK, fully_past)
        bias = jnp.where(jnp.logical_or(chosen, blk == i), 0.0, MASKED)
        zeros = jnp.zeros((KQ_DIM - HEAD_DIM - BIAS_ROWS, TOKENS), f32)
        qbuf[h, :HEAD_DIM, :] = q_h
        qbuf[h, HEAD_DIM:, :] = jnp.concatenate([bias, zeros], axis=0).astype(bf16)

    tok = i * TOKENS + lax.broadcasted_iota(jnp.int32, (1, TOKENS), 1)
    groups_per_part = len(POOL_WINDOWS) // 2
    for gi, w in enumerate(POOL_WINDOWS):
        rows = slice(gi * POOL_GROUP_WIDTH, (gi + 1) * POOL_GROUP_WIDTH)
        gg = gi % groups_per_part
        u_g = u_parts[gi // groups_per_part][gg * POOL_GROUP_WIDTH:(gg + 1) * POOL_GROUP_WIDTH, :]
        tiles = [uprev[rows, :]] + [u_g[:, c * LANES:(c + 1) * LANES]
                                    for c in range(TOKENS // LANES)]
        uprev[rows, :] = tiles[-1]
        shift = 1
        while shift < w:
            shifted = _shift_tokens(tiles, shift)
            tiles = [a + b for a, b in zip(tiles, shifted)]
            shift *= 2
        win_sum = jnp.concatenate(tiles[1:], axis=1)
        inv_cnt = 1.0 / jnp.minimum(tok + 1, w).astype(f32)
        pooled = win_sum * inv_cnt - u_g
        y = _bdot(wgrp_ref[gi], pooled.astype(bf16)) * ps_ref[rows, :]
        op_sc[rows, :] = y.astype(bf16)

    key_pos = lax.broadcasted_iota(jnp.int32, (MOBA_BLOCK, TOKENS), 0)
    qry_pos = lax.broadcasted_iota(jnp.int32, (MOBA_BLOCK, TOKENS), 1)
    causal = key_pos <= qry_pos

    def score_head(j, slot, h, own=False):
        s = _bdot(kbuf[j, h], qbuf[h])
        if own:
            s = jnp.where(causal, s, MASKED)
        s_ref[slot, h] = s
        m_new = jnp.max(s, axis=0, keepdims=True)
        if not own:
            m_old = m_ref[1 - slot, h:h + 1, :]
            m_new = jnp.maximum(m_old, m_new)
            alpha_ref[slot, h:h + 1, :] = jnp.exp2(m_old - m_new)
        m_ref[slot, h:h + 1, :] = m_new

    def value_head(j, slot, h, own=False):
        p = jnp.exp2(s_ref[slot, h] - m_ref[slot, h:h + 1, :]).astype(bf16)
        pv = _bdot(vbuf[j, h], p)
        acc_ref[h] = pv if own else acc_ref[h] * alpha_ref[slot, h:h + 1, :] + pv

    def score_pass(j, slot, own=False):
        for h in range(HEADS):
            score_head(j, slot, h, own)

    def value_pass(j, slot, own=False):
        for h in range(HEADS):
            value_head(j, slot, h, own)

    score_pass(i, 1, own=True)

    fT = f_sc[...]
    r2_prev = r2_sc[0:1, :]
    for c in range(D_MODEL // OUT_CHUNK):
        chunk = slice(c * OUT_CHUNK, (c + 1) * OUT_CHUNK)
        for rows, y in zip(_halves(chunk), _wdot2(wff2_ref, chunk, fT)):
            o_ref[:, rows] = (x1_sc[rows, :] + y * r2_prev).T

    value_pass(i, 1, own=True)

    n_past = jnp.where(n == n_tiles, 0, i)

    def gate_pooling_branch():
        for rows, upp in zip(_halves(all_rows), _wdot2(wupp_ref, all_rows, op_sc[...])):
            sgp_sc[rows, :] = sgp_sc[rows, :] * upp

    def visit(blocks):
        if blocks:
            score_pass(blocks[0], 0)
        for k in range(1, len(blocks)):
            for h in range(HEADS):
                score_head(blocks[k], k % 2, h)
                value_head(blocks[k - 1], (k - 1) % 2, h)
        gate_pooling_branch()
        if blocks:
            value_pass(blocks[-1], (len(blocks) - 1) % 2)

    for count in range(n_blocks):
        pl.when(n_past == count)(lambda count=count: visit(list(range(count))))

    for h in range(HEADS):
        o_h = acc_ref[h, :HEAD_DIM, :] * (1.0 / acc_ref[h, HEAD_DIM:HEAD_DIM + 1, :])
        o_sc[h * HEAD_DIM:(h + 1) * HEAD_DIM, :] = o_h.astype(bf16)

    for rows, upa in zip(_halves(all_rows), _wdot2(wupa_ref, all_rows, o_sc[...])):
        mg_sc[rows, :] = (sga_sc[rows, :] * upa + sgp_sc[rows, :]).astype(bf16)
    sumsq = jnp.zeros((1, TOKENS), f32)
    for rows, d in zip(_halves(all_rows), _wdot2(wout_ref, all_rows, mg_sc[...])):
        x1 = xT[rows, :] + d
        x1_sc[rows, :] = x1
        h2_sc[rows, :] = (x1 * gmlp_ref[rows, :]).astype(bf16)
        sumsq = sumsq + jnp.sum(x1 * x1, axis=0, keepdims=True)
    r = lax.rsqrt(sumsq * (1.0 / D_MODEL) + RMS_EPS)
    r2_sc[...] = jnp.broadcast_to(r * r, r2_sc.shape)


def _bcast_cols(v, n):
    return jnp.broadcast_to(v.astype(jnp.float32)[:, None], (v.shape[0], n))


@jax.jit
def kernel(x, norm_mix, w_in, q_norm, k_norm, w_pool_grp, pool_scale, w_up_attn, w_up_pool,
           w_out, norm_mlp, w_ff1, w_ff2):
    batch, seq, d_model = x.shape
    assert d_model == D_MODEL and seq % TOKENS == 0 and w_in.shape[0] == 1
    n_blocks = seq // TOKENS
    assert n_blocks <= BIAS_ROWS
    bf16 = jnp.bfloat16

    half = jnp.arange(ROPE_HALF, dtype=jnp.float32)
    inv_freq = ROPE_THETA ** (-half / ROPE_HALF)
    ang = inv_freq[:, None] * jnp.arange(seq).astype(jnp.float32)[None, :]

    n_tiles = batch * n_blocks

    def tile(n):
        t = jnp.clip(n, 0, n_tiles - 1)
        return t // n_blocks, t % n_blocks

    whole = pl.BlockSpec(memory_space=pltpu.VMEM)
    in_hbm = pl.BlockSpec(memory_space=pl.ANY)
    weights = (w_in, w_up_attn, w_up_pool, w_out, w_ff1, w_ff2)
    x_spec = pl.BlockSpec((None, TOKENS, D_MODEL), lambda n: (*tile(n), 0))
    out_spec = pl.BlockSpec((None, TOKENS, D_MODEL), lambda n: (*tile(n - 1), 0))
    rope_spec = pl.BlockSpec((2 * ROPE_HALF, TOKENS), lambda n: (0, tile(n)[1]))
    gains = jnp.concatenate([norm_mix[0], norm_mlp[0], q_norm[0], k_norm[0], pool_scale[0]])

    return pl.pallas_call(
        _layer_kernel,
        grid=(n_tiles + 1,),
        in_specs=[x_spec, rope_spec, whole, whole] + [in_hbm] * len(weights),
        out_specs=out_spec,
        out_shape=jax.ShapeDtypeStruct(x.shape, x.dtype),
        scratch_shapes=[
            *[pltpu.VMEM((w.shape[2], w.shape[1]), bf16) for w in weights],
            pltpu.SemaphoreType.DMA((2 * HEADS,)),
            pltpu.VMEM((n_blocks, HEADS, MOBA_BLOCK, KQ_DIM), bf16),
            pltpu.VMEM((n_blocks, HEADS, V_ROWS, MOBA_BLOCK), bf16),
            pltpu.VMEM((HEADS, 2 * BIAS_ROWS, KQ_DIM), jnp.float32),
            pltpu.VMEM((POOL_WIDTH, LANES), jnp.float32),
            pltpu.VMEM((HEADS, KQ_DIM, TOKENS), bf16),
            pltpu.VMEM((2, HEADS, TOKENS), jnp.float32),
            pltpu.VMEM((2, HEADS, TOKENS), jnp.float32),
            pltpu.VMEM((HEADS, V_ROWS, TOKENS), jnp.float32),
            pltpu.VMEM((2, HEADS, MOBA_BLOCK, TOKENS), jnp.float32),
            pltpu.VMEM((ATTN_WIDTH, TOKENS), bf16),
            pltpu.VMEM((POOL_WIDTH, TOKENS), bf16),
            pltpu.VMEM((D_MODEL, TOKENS), jnp.float32),
            pltpu.VMEM((D_MODEL, TOKENS), jnp.float32),
            pltpu.VMEM((D_MODEL, TOKENS), bf16),
            pltpu.VMEM((D_MODEL, TOKENS), jnp.float32),
            pltpu.VMEM((D_MODEL, TOKENS), bf16),
            pltpu.VMEM((16, TOKENS), jnp.float32),
            pltpu.VMEM((D_FF, TOKENS), bf16),
        ],
        compiler_params=pltpu.CompilerParams(
            dimension_semantics=("arbitrary",),
            vmem_limit_bytes=VMEM_LIMIT_BYTES),
        name="moba_pool_layer",
    )(x, jnp.concatenate([jnp.cos(ang), jnp.sin(ang)]), _bcast_cols(gains, TOKENS),
      jnp.swapaxes(w_pool_grp[0], 1, 2).astype(bf16), *weights)
```

```python
import jax
import jax.numpy as jnp
from jax import lax
from jax.experimental import pallas as pl
from jax.experimental.pallas import tpu as pltpu

D_MODEL = 1024
HEADS = 8
HEAD_DIM = 64
ATTN_WIDTH = HEADS * HEAD_DIM
MOBA_BLOCK = 256
MOBA_TOPK = 3
ROPE_THETA = 500000.0
ROPE_HALF = HEAD_DIM // 8
POOL_WINDOWS = (2, 4, 8, 16)
POOL_GROUP_WIDTH = 128
POOL_WIDTH = len(POOL_WINDOWS) * POOL_GROUP_WIDTH
D_FF = 4 * D_MODEL
RMS_EPS = 1e-6

LANES = 128
TOKENS = MOBA_BLOCK
MASKED = -1e30
KQ_DIM = LANES
BIAS_ROWS = 8
V_ROWS = HEAD_DIM + 16
Q_SCALE = HEAD_DIM ** -0.5 * 1.4426950408889634
FF_CHUNK = 1024
OUT_CHUNK = 512
VMEM_LIMIT_BYTES = 61 * 1024 * 1024

_Q0, _K0, _V0, _U0 = 0, ATTN_WIDTH, 2 * ATTN_WIDTH, 3 * ATTN_WIDTH
_GA0 = 3 * ATTN_WIDTH + POOL_WIDTH
_GP0 = _GA0 + D_MODEL
IN_WIDTH = _GP0 + D_MODEL
_GAIN_ROWS = (0, D_MODEL, 2 * D_MODEL, 2 * D_MODEL + HEAD_DIM, 2 * D_MODEL + 2 * HEAD_DIM,
              2 * D_MODEL + 2 * HEAD_DIM + POOL_WIDTH)


def _bdot(a, b):
    return jnp.dot(a, b, preferred_element_type=jnp.float32)


def _wdot2(w_ref, rows, act, cols=slice(None)):
    mid = (rows.start + rows.stop) // 2
    return [_bdot(w_ref[rows.start:mid, cols], act), _bdot(w_ref[mid:rows.stop, cols], act)]


def _halves(rows):
    mid = (rows.start + rows.stop) // 2
    return [slice(rows.start, mid), slice(mid, rows.stop)]


def _sigmoid(x):
    return 0.5 * jnp.tanh(0.5 * x) + 0.5


def _rms_scale(t):
    return lax.rsqrt(jnp.mean(t * t, axis=0, keepdims=True) + RMS_EPS)


def _head_norm_rope(t, gain, cos, sin):
    y = t * _rms_scale(t) * gain
    x1 = y[0:ROPE_HALF, :]
    x2 = y[ROPE_HALF:2 * ROPE_HALF, :]
    return jnp.concatenate(
        [x1 * cos - x2 * sin, x2 * cos + x1 * sin, y[2 * ROPE_HALF:, :]], axis=0)


def _shift_tokens(tiles, shift):
    lane = lax.broadcasted_iota(jnp.int32, tiles[0].shape, 1)
    rolled = [pltpu.roll(t, shift, axis=1) for t in tiles]
    out = [rolled[0]]
    for c in range(1, len(tiles)):
        out.append(jnp.where(lane < shift, rolled[c - 1], rolled[c]))
    return out


def _load_weights(pairs, stage, sems):
    side = stage.shape[-1]
    slots_per_row = stage.shape[1]
    n_slots = stage.shape[0] * slots_per_row
    tiles = [(src, dst, k0, n0)
             for src, dst in pairs
             for k0 in range(0, dst.shape[1], side)
             for n0 in range(0, dst.shape[0], side)]

    def slot_of(t):
        slot = t % n_slots
        return slot, stage.at[slot // slots_per_row, slot % slots_per_row]

    def copy(t):
        src, _, k0, n0 = tiles[t]
        slot, buf = slot_of(t)
        return pltpu.make_async_copy(src.at[0, pl.ds(k0, side), pl.ds(n0, side)], buf, sems.at[slot])

    for t in range(min(n_slots, len(tiles))):
        copy(t).start()
    group = 4
    for t0 in range(0, len(tiles), group):
        ts = range(t0, min(t0 + group, len(tiles)))
        for t in ts:
            copy(t).wait()
        for t in ts:
            _, dst, k0, n0 = tiles[t]
            dst[n0:n0 + side, k0:k0 + side] = slot_of(t)[1][...].T.astype(dst.dtype)
        for t in ts:
            if t + n_slots < len(tiles):
                copy(t + n_slots).start()


def _layer_kernel(x_ref, rope_ref, gains_ref, wgrp_in,
                  win_hbm, wupa_hbm, wupp_hbm, wout_hbm, wff1_hbm, wff2_hbm,
                  o_ref,
                  win_ref, wupa_ref, wupp_ref, wout_ref, wff1_ref, wff2_ref, wgrp_ref, wsem,
                  kbuf, vbuf, kmbuf, uprev, qbuf, m_ref, alpha_ref, acc_ref, s_ref,
                  o_sc, op_sc, sga_sc, sgp_sc, mg_sc, x1_sc, h2_sc, f_sc):
    f32, bf16 = jnp.float32, jnp.bfloat16
    gmix_ref, gmlp_ref, qn_ref, kn_ref, ps_ref = (
        gains_ref.at[lo:hi] for lo, hi in zip(_GAIN_ROWS[:-1], _GAIN_ROWS[1:]))
    n = pl.program_id(0)
    n_tiles = pl.num_programs(0) - 1
    n_blocks = kbuf.shape[0]
    i = lax.rem(jnp.minimum(n, n_tiles - 1), n_blocks)

    @pl.when(n == 0)
    def _first_step():
        _load_weights([(win_hbm, win_ref), (wupa_hbm, wupa_ref), (wupp_hbm, wupp_ref),
                       (wout_hbm, wout_ref), (wff1_hbm, wff1_ref), (wff2_hbm, wff2_ref)],
                      s_ref, wsem)
        for g in range(len(POOL_WINDOWS)):
            wgrp_ref[g] = wgrp_in[0, g].T.astype(bf16)
        x1_sc[...] = jnp.zeros_like(x1_sc)
        h2_sc[...] = jnp.zeros_like(h2_sc)

    @pl.when(i == 0)
    def _start_of_sequence():
        kmbuf[...] = jnp.zeros_like(kmbuf)
        uprev[...] = jnp.zeros_like(uprev)

    h2T = h2_sc[...]
    for c in range(D_FF // FF_CHUNK):
        chunk = slice(c * FF_CHUNK, (c + 1) * FF_CHUNK)
        for rows, z in zip(_halves(chunk), _wdot2(wff1_ref, chunk, h2T)):
            f = jnp.maximum(z, 0.0)
            f_sc[rows, :] = (f * f).astype(bf16)

    xT = x_ref[...].T
    hT = (xT * _rms_scale(xT) * gmix_ref[...]).astype(bf16)

    def proj2(lo, width):
        return _wdot2(win_ref, slice(lo, lo + width), hT)

    def head_rows(parts, h):
        per_part = HEADS // 2
        hh = h % per_part
        return parts[h // per_part][hh * HEAD_DIM:(hh + 1) * HEAD_DIM, :]

    cos = rope_ref[:ROPE_HALF, :]
    sin = rope_ref[ROPE_HALF:, :]

    k_parts = proj2(_K0, ATTN_WIDTH)
    q_parts = proj2(_Q0, ATTN_WIDTH)
    v_parts = proj2(_V0, ATTN_WIDTH)
    u_parts = proj2(_U0, POOL_WIDTH)
    all_rows = slice(0, D_MODEL)
    for rows, ga, gp in zip(_halves(all_rows), proj2(_GA0, D_MODEL), proj2(_GP0, D_MODEL)):
        sga_sc[rows, :] = _sigmoid(ga)
        sgp_sc[rows, :] = _sigmoid(gp)
    pad_row = lax.broadcasted_iota(jnp.int32, (KQ_DIM - HEAD_DIM, TOKENS), 0)
    own_block = jnp.where(pad_row == i, 1.0, 0.0)
    ones_row = jnp.where(
        lax.broadcasted_iota(jnp.int32, (V_ROWS - HEAD_DIM, TOKENS), 0) == 0, 1.0, 0.0)
    for h in range(HEADS):
        k_h = _head_norm_rope(head_rows(k_parts, h), kn_ref[...], cos, sin)
        k_tok = jnp.concatenate([k_h, own_block], axis=0).T
        kbuf[i, h] = k_tok.astype(bf16)
        kmean = jnp.sum(k_tok, axis=0, keepdims=True) * (1.0 / MOBA_BLOCK)
        kmbuf[h, pl.ds(i, 1), :] = kmean
        vbuf[i, h, :HEAD_DIM, :] = head_rows(v_parts, h).astype(bf16)
        vbuf[i, h, HEAD_DIM:, :] = ones_row.astype(bf16)

    blk = lax.broadcasted_iota(jnp.int32, (BIAS_ROWS, TOKENS), 0)
    fully_past = blk < i
    for h in range(HEADS):
        q_h = _head_norm_rope(head_rows(q_parts, h), qn_ref[...], cos, sin)
        q_h = (q_h * Q_SCALE).astype(bf16)
        gate = _bdot(kmbuf[h, :, :HEAD_DIM].astype(bf16), q_h)[:BIAS_ROWS, :]
        g = jnp.where(fully_past, gate, -jnp.inf)
        rank = jnp.zeros((BIAS_ROWS, TOKENS), jnp.int32)
        for jp in range(n_blocks - 1):
            g_jp = g[jp:jp + 1, :]
            ahead = jnp.logical_or(g_jp > g, jnp.logical_and(g_jp == g, blk > jp))
            rank = rank + ahead.astype(jnp.int32)
        chosen = jnp.logical_and(rank < MOBA_TOPK, fully_past)
        bias = jnp.where(jnp.logical_or(chosen, blk == i), 0.0, MASKED)
        zeros = jnp.zeros((KQ_DIM - HEAD_DIM - BIAS_ROWS, TOKENS), f32)
        qbuf[h, :HEAD_DIM, :] = q_h
        qbuf[h, HEAD_DIM:, :] = jnp.concatenate([bias, zeros], axis=0).astype(bf16)

    tok = i * TOKENS + lax.broadcasted_iota(jnp.int32, (1, TOKENS), 1)
    groups_per_part = len(POOL_WINDOWS) // 2
    for gi, w in enumerate(POOL_WINDOWS):
        rows = slice(gi * POOL_GROUP_WIDTH, (gi + 1) * POOL_GROUP_WIDTH)
        gg = gi % groups_per_part
        u_g = u_parts[gi // groups_per_part][gg * POOL_GROUP_WIDTH:(gg + 1) * POOL_GROUP_WIDTH, :]
        tiles = [uprev[rows, :]] + [u_g[:, c * LANES:(c + 1) * LANES]
                                    for c in range(TOKENS // LANES)]
        uprev[rows, :] = tiles[-1]
        shift = 1
        while shift < w:
            shifted = _shift_tokens(tiles, shift)
            tiles = [a + b for a, b in zip(tiles, shifted)]
            shift *= 2
        win_sum = jnp.concatenate(tiles[1:], axis=1)
        inv_cnt = 1.0 / jnp.minimum(tok + 1, w).astype(f32)
        pooled = win_sum * inv_cnt - u_g
        y = _bdot(wgrp_ref[gi], pooled.astype(bf16)) * ps_ref[rows, :]
        op_sc[rows, :] = y.astype(bf16)

    key_pos = lax.broadcasted_iota(jnp.int32, (MOBA_BLOCK, TOKENS), 0)
    qry_pos = lax.broadcasted_iota(jnp.int32, (MOBA_BLOCK, TOKENS), 1)
    causal = key_pos <= qry_pos

    def score_head(j, slot, h, own=False):
        s = _bdot(kbuf[j, h], qbuf[h])
        if own:
            s = jnp.where(causal, s, MASKED)
        s_ref[slot, h] = s
        m_new = jnp.max(s, axis=0, keepdims=True)
        if not own:
            m_old = m_ref[1 - slot, h:h + 1, :]
            m_new = jnp.maximum(m_old, m_new)
            alpha_ref[slot, h:h + 1, :] = jnp.exp2(m_old - m_new)
        m_ref[slot, h:h + 1, :] = m_new

    def value_head(j, slot, h, own=False):
        p = jnp.exp2(s_ref[slot, h] - m_ref[slot, h:h + 1, :]).astype(bf16)
        pv = _bdot(vbuf[j, h], p)
        acc_ref[h] = pv if own else acc_ref[h] * alpha_ref[slot, h:h + 1, :] + pv

    def score_pass(j, slot, own=False):
        for h in range(HEADS):
            score_head(j, slot, h, own)

    def value_pass(j, slot, own=False):
        for h in range(HEADS):
            value_head(j, slot, h, own)

    score_pass(i, 1, own=True)

    fT = f_sc[...]
    r_prev = _rms_scale(x1_sc[...])
    r2_prev = r_prev * r_prev
    for c in range(D_MODEL // OUT_CHUNK):
        chunk = slice(c * OUT_CHUNK, (c + 1) * OUT_CHUNK)
        for rows, y in zip(_halves(chunk), _wdot2(wff2_ref, chunk, fT)):
            o_ref[:, rows] = (x1_sc[rows, :] + y * r2_prev).T

    value_pass(i, 1, own=True)

    n_past = jnp.where(n == n_tiles, 0, i)

    def gate_pooling_branch():
        for rows, upp in zip(_halves(all_rows), _wdot2(wupp_ref, all_rows, op_sc[...])):
            sgp_sc[rows, :] = sgp_sc[rows, :] * upp

    def visit(blocks):
        if blocks:
            score_pass(blocks[0], 0)
        for k in range(1, len(blocks)):
            for h in range(HEADS):
                score_head(blocks[k], k % 2, h)
                value_head(blocks[k - 1], (k - 1) % 2, h)
        gate_pooling_branch()
        if blocks:
            value_pass(blocks[-1], (len(blocks) - 1) % 2)

    for count in range(n_blocks):
        pl.when(n_past == count)(lambda count=count: visit(list(range(count))))

    for h in range(HEADS):
        o_h = acc_ref[h, :HEAD_DIM, :] * (1.0 / acc_ref[h, HEAD_DIM:HEAD_DIM + 1, :])
        o_sc[h * HEAD_DIM:(h + 1) * HEAD_DIM, :] = o_h.astype(bf16)

    for rows, upa in zip(_halves(all_rows), _wdot2(wupa_ref, all_rows, o_sc[...])):
        mg_sc[rows, :] = (sga_sc[rows, :] * upa + sgp_sc[rows, :]).astype(bf16)
    for rows, d in zip(_halves(all_rows), _wdot2(wout_ref, all_rows, mg_sc[...])):
        x1 = xT[rows, :] + d
        x1_sc[rows, :] = x1
        h2_sc[rows, :] = (x1 * gmlp_ref[rows, :]).astype(bf16)


def _bcast_cols(v, n):
    return jnp.broadcast_to(v.astype(jnp.float32)[:, None], (v.shape[0], n))


@jax.jit
def kernel(x, norm_mix, w_in, q_norm, k_norm, w_pool_grp, pool_scale, w_up_attn, w_up_pool,
           w_out, norm_mlp, w_ff1, w_ff2):
    batch, seq, d_model = x.shape
    assert d_model == D_MODEL and seq % TOKENS == 0 and w_in.shape[0] == 1
    n_blocks = seq // TOKENS
    assert n_blocks <= BIAS_ROWS
    bf16 = jnp.bfloat16

    half = jnp.arange(ROPE_HALF, dtype=jnp.float32)
    inv_freq = ROPE_THETA ** (-half / ROPE_HALF)
    ang = inv_freq[:, None] * jnp.arange(seq).astype(jnp.float32)[None, :]

    n_tiles = batch * n_blocks

    def tile(n):
        t = jnp.clip(n, 0, n_tiles - 1)
        return t // n_blocks, t % n_blocks

    whole = pl.BlockSpec(memory_space=pltpu.VMEM)
    in_hbm = pl.BlockSpec(memory_space=pl.ANY)
    weights = (w_in, w_up_attn, w_up_pool, w_out, w_ff1, w_ff2)
    x_spec = pl.BlockSpec((None, TOKENS, D_MODEL), lambda n: (*tile(n), 0))
    out_spec = pl.BlockSpec((None, TOKENS, D_MODEL), lambda n: (*tile(n - 1), 0))
    rope_spec = pl.BlockSpec((2 * ROPE_HALF, TOKENS), lambda n: (0, tile(n)[1]))
    gains = jnp.concatenate([norm_mix[0], norm_mlp[0], q_norm[0], k_norm[0], pool_scale[0]])

    return pl.pallas_call(
        _layer_kernel,
        grid=(n_tiles + 1,),
        in_specs=[x_spec, rope_spec, whole, whole] + [in_hbm] * len(weights),
        out_specs=out_spec,
        out_shape=jax.ShapeDtypeStruct(x.shape, x.dtype),
        scratch_shapes=[
            *[pltpu.VMEM((w.shape[2], w.shape[1]), bf16) for w in weights],
            pltpu.VMEM(w_pool_grp.shape[1:], bf16),
            pltpu.SemaphoreType.DMA((2 * HEADS,)),
            pltpu.VMEM((n_blocks, HEADS, MOBA_BLOCK, KQ_DIM), bf16),
            pltpu.VMEM((n_blocks, HEADS, V_ROWS, MOBA_BLOCK), bf16),
            pltpu.VMEM((HEADS, 2 * BIAS_ROWS, KQ_DIM), jnp.float32),
            pltpu.VMEM((POOL_WIDTH, LANES), jnp.float32),
            pltpu.VMEM((HEADS, KQ_DIM, TOKENS), bf16),
            pltpu.VMEM((2, HEADS, TOKENS), jnp.float32),
            pltpu.VMEM((2, HEADS, TOKENS), jnp.float32),
            pltpu.VMEM((HEADS, V_ROWS, TOKENS), jnp.float32),
            pltpu.VMEM((2, HEADS, MOBA_BLOCK, TOKENS), jnp.float32),
            pltpu.VMEM((ATTN_WIDTH, TOKENS), bf16),
            pltpu.VMEM((POOL_WIDTH, TOKENS), bf16),
            pltpu.VMEM((D_MODEL, TOKENS), jnp.float32),
            pltpu.VMEM((D_MODEL, TOKENS), jnp.float32),
            pltpu.VMEM((D_MODEL, TOKENS), bf16),
            pltpu.VMEM((D_MODEL, TOKENS), jnp.float32),
            pltpu.VMEM((D_MODEL, TOKENS), bf16),
            pltpu.VMEM((D_FF, TOKENS), bf16),
        ],
        compiler_params=pltpu.CompilerParams(
            dimension_semantics=("arbitrary",),
            vmem_limit_bytes=VMEM_LIMIT_BYTES),
        name="moba_pool_layer",
    )(x, jnp.concatenate([jnp.cos(ang), jnp.sin(ang)]), _bcast_cols(gains, TOKENS),
      w_pool_grp, *weights)
```

```python
import jax
import jax.numpy as jnp
from jax import lax
from jax.experimental import pallas as pl
from jax.experimental.pallas import tpu as pltpu

D_MODEL = 1024
HEADS = 8
HEAD_DIM = 64
ATTN_WIDTH = HEADS * HEAD_DIM
MOBA_BLOCK = 256
MOBA_TOPK = 3
ROPE_THETA = 500000.0
ROPE_HALF = HEAD_DIM // 8
POOL_WINDOWS = (2, 4, 8, 16)
POOL_GROUP_WIDTH = 128
POOL_WIDTH = len(POOL_WINDOWS) * POOL_GROUP_WIDTH
D_FF = 4 * D_MODEL
RMS_EPS = 1e-6

LANES = 128
TOKENS = MOBA_BLOCK
MASKED = -1e30
KQ_DIM = LANES
BIAS_ROWS = 8
V_ROWS = HEAD_DIM + 16
Q_SCALE = HEAD_DIM ** -0.5 * 1.4426950408889634
FF_CHUNK = 1024
OUT_CHUNK = 512
VMEM_LIMIT_BYTES = 61 * 1024 * 1024

_Q0, _K0, _V0, _U0 = 0, ATTN_WIDTH, 2 * ATTN_WIDTH, 3 * ATTN_WIDTH
_GA0 = 3 * ATTN_WIDTH + POOL_WIDTH
_GP0 = _GA0 + D_MODEL
IN_WIDTH = _GP0 + D_MODEL
_GAIN_ROWS = (0, D_MODEL, 2 * D_MODEL, 2 * D_MODEL + HEAD_DIM, 2 * D_MODEL + 2 * HEAD_DIM,
              2 * D_MODEL + 2 * HEAD_DIM + POOL_WIDTH)


def _bdot(a, b):
    return jnp.dot(a, b, preferred_element_type=jnp.float32)


def _wdot2(w_ref, rows, act, cols=slice(None)):
    mid = (rows.start + rows.stop) // 2
    return [_bdot(w_ref[rows.start:mid, cols], act), _bdot(w_ref[mid:rows.stop, cols], act)]


def _halves(rows):
    mid = (rows.start + rows.stop) // 2
    return [slice(rows.start, mid), slice(mid, rows.stop)]


def _sigmoid(x):
    return 0.5 * jnp.tanh(0.5 * x) + 0.5


def _rms_scale(t):
    return lax.rsqrt(jnp.mean(t * t, axis=0, keepdims=True) + RMS_EPS)


def _head_norm_rope(t, gain, cos, sin):
    y = t * _rms_scale(t) * gain
    x1 = y[0:ROPE_HALF, :]
    x2 = y[ROPE_HALF:2 * ROPE_HALF, :]
    return jnp.concatenate(
        [x1 * cos - x2 * sin, x2 * cos + x1 * sin, y[2 * ROPE_HALF:, :]], axis=0)


def _shift_tokens(tiles, shift):
    lane = lax.broadcasted_iota(jnp.int32, tiles[0].shape, 1)
    rolled = [pltpu.roll(t, shift, axis=1) for t in tiles]
    out = [rolled[0]]
    for c in range(1, len(tiles)):
        out.append(jnp.where(lane < shift, rolled[c - 1], rolled[c]))
    return out


def _load_weights(pairs, stage, sems):
    side = stage.shape[-1]
    slots_per_row = stage.shape[1]
    n_slots = stage.shape[0] * slots_per_row
    tiles = [(src, dst, k0, n0)
             for src, dst in pairs
             for k0 in range(0, dst.shape[1], side)
             for n0 in range(0, dst.shape[0], side)]

    def slot_of(t):
        slot = t % n_slots
        return slot, stage.at[slot // slots_per_row, slot % slots_per_row]

    def copy(t):
        src, _, k0, n0 = tiles[t]
        slot, buf = slot_of(t)
        return pltpu.make_async_copy(src.at[0, pl.ds(k0, side), pl.ds(n0, side)], buf, sems.at[slot])

    for t in range(min(n_slots, len(tiles))):
        copy(t).start()
    group = 4
    for t0 in range(0, len(tiles), group):
        ts = range(t0, min(t0 + group, len(tiles)))
        for t in ts:
            copy(t).wait()
        for t in ts:
            _, dst, k0, n0 = tiles[t]
            dst[n0:n0 + side, k0:k0 + side] = slot_of(t)[1][...].T.astype(dst.dtype)
        for t in ts:
            if t + n_slots < len(tiles):
                copy(t + n_slots).start()


def _layer_kernel(x_ref, rope_ref, gains_ref, wgrp_in,
                  win_hbm, wupa_hbm, wupp_hbm, wout_hbm, wff1_hbm, wff2_hbm,
                  o_ref,
                  win_ref, wupa_ref, wupp_ref, wout_ref, wff1_ref, wff2_ref, wgrp_ref, wsem,
                  kbuf, vbuf, kmbuf, uprev, qbuf, m_ref, alpha_ref, acc_ref, s_ref,
                  o_sc, op_sc, sga_sc, sgp_sc, mg_sc, x1_sc, h2_sc, r2_sc, f_sc):
    f32, bf16 = jnp.float32, jnp.bfloat16
    gmix_ref, gmlp_ref, qn_ref, kn_ref, ps_ref = (
        gains_ref.at[lo:hi] for lo, hi in zip(_GAIN_ROWS[:-1], _GAIN_ROWS[1:]))
    n = pl.program_id(0)
    n_tiles = pl.num_programs(0) - 1
    n_blocks = kbuf.shape[0]
    i = lax.rem(jnp.minimum(n, n_tiles - 1), n_blocks)

    @pl.when(n == 0)
    def _first_step():
        _load_weights([(win_hbm, win_ref), (wupa_hbm, wupa_ref), (wupp_hbm, wupp_ref),
                       (wout_hbm, wout_ref), (wff1_hbm, wff1_ref), (wff2_hbm, wff2_ref)],
                      s_ref, wsem)
        for g in range(len(POOL_WINDOWS)):
            wgrp_ref[g] = wgrp_in[0, g].T.astype(bf16)
        x1_sc[...] = jnp.zeros_like(x1_sc)
        h2_sc[...] = jnp.zeros_like(h2_sc)
        r2_sc[...] = jnp.zeros_like(r2_sc)

    @pl.when(i == 0)
    def _start_of_sequence():
        kmbuf[...] = jnp.zeros_like(kmbuf)
        uprev[...] = jnp.zeros_like(uprev)

    h2T = h2_sc[...]
    for c in range(D_FF // FF_CHUNK):
        chunk = slice(c * FF_CHUNK, (c + 1) * FF_CHUNK)
        for rows, z in zip(_halves(chunk), _wdot2(wff1_ref, chunk, h2T)):
            f = jnp.maximum(z, 0.0)
            f_sc[rows, :] = (f * f).astype(bf16)

    xT = x_ref[...].T
    hT = (xT * _rms_scale(xT) * gmix_ref[...]).astype(bf16)

    def proj2(lo, width):
        return _wdot2(win_ref, slice(lo, lo + width), hT)

    def head_rows(parts, h):
        per_part = HEADS // 2
        hh = h % per_part
        return parts[h // per_part][hh * HEAD_DIM:(hh + 1) * HEAD_DIM, :]

    cos = rope_ref[:ROPE_HALF, :]
    sin = rope_ref[ROPE_HALF:, :]

    k_parts = proj2(_K0, ATTN_WIDTH)
    q_parts = proj2(_Q0, ATTN_WIDTH)
    v_parts = proj2(_V0, ATTN_WIDTH)
    u_parts = proj2(_U0, POOL_WIDTH)
    all_rows = slice(0, D_MODEL)
    for rows, ga, gp in zip(_halves(all_rows), proj2(_GA0, D_MODEL), proj2(_GP0, D_MODEL)):
        sga_sc[rows, :] = _sigmoid(ga)
        sgp_sc[rows, :] = _sigmoid(gp)
    pad_row = lax.broadcasted_iota(jnp.int32, (KQ_DIM - HEAD_DIM, TOKENS), 0)
    own_block = jnp.where(pad_row == i, 1.0, 0.0)
    ones_row = jnp.where(
        lax.broadcasted_iota(jnp.int32, (V_ROWS - HEAD_DIM, TOKENS), 0) == 0, 1.0, 0.0)
    for h in range(HEADS):
        k_h = _head_norm_rope(head_rows(k_parts, h), kn_ref[...], cos, sin)
        k_tok = jnp.concatenate([k_h, own_block], axis=0).T
        kbuf[i, h] = k_tok.astype(bf16)
        kmean = jnp.sum(k_tok, axis=0, keepdims=True) * (1.0 / MOBA_BLOCK)
        kmbuf[h, pl.ds(i, 1), :] = kmean
        vbuf[i, h, :HEAD_DIM, :] = head_rows(v_parts, h).astype(bf16)
        vbuf[i, h, HEAD_DIM:, :] = ones_row.astype(bf16)

    blk = lax.broadcasted_iota(jnp.int32, (BIAS_ROWS, TOKENS), 0)
    fully_past = blk < i
    for h in range(HEADS):
        q_h = _head_norm_rope(head_rows(q_parts, h), qn_ref[...], cos, sin)
        q_h = (q_h * Q_SCALE).astype(bf16)
        gate = _bdot(kmbuf[h, :, :HEAD_DIM].astype(bf16), q_h)[:BIAS_ROWS, :]
        g = jnp.where(fully_past, gate, -jnp.inf)
        rank = jnp.zeros((BIAS_ROWS, TOKENS), jnp.int32)
        for jp in range(n_blocks - 1):
            g_jp = g[jp:jp + 1, :]
            ahead = jnp.logical_or(g_jp > g, jnp.logical_and(g_jp == g, blk > jp))
            rank = rank + ahead.astype(jnp.int32)
        chosen = jnp.logical_and(rank < MOBA_TOPK, fully_past)
        bias = jnp.where(jnp.logical_or(chosen, blk == i), 0.0, MASKED)
        zeros = jnp.zeros((KQ_DIM - HEAD_DIM - BIAS_ROWS, TOKENS), f32)
        qbuf[h, :HEAD_DIM, :] = q_h
        qbuf[h, HEAD_DIM:, :] = jnp.concatenate([bias, zeros], axis=0).astype(bf16)

    tok = i * TOKENS + lax.broadcasted_iota(jnp.int32, (1, TOKENS), 1)
    groups_per_part = len(POOL_WINDOWS) // 2
    for gi, w in enumerate(POOL_WINDOWS):
        rows = slice(gi * POOL_GROUP_WIDTH, (gi + 1) * POOL_GROUP_WIDTH)
        gg = gi % groups_per_part
        u_g = u_parts[gi // groups_per_part][gg * POOL_GROUP_WIDTH:(gg + 1) * POOL_GROUP_WIDTH, :]
        tiles = [uprev[rows, :]] + [u_g[:, c * LANES:(c + 1) * LANES]
                                    for c in range(TOKENS // LANES)]
        uprev[rows, :] = tiles[-1]
        shift = 1
        while shift < w:
            shifted = _shift_tokens(tiles, shift)
            tiles = [a + b for a, b in zip(tiles, shifted)]
            shift *= 2
        win_sum = jnp.concatenate(tiles[1:], axis=1)
        inv_cnt = 1.0 / jnp.minimum(tok + 1, w).astype(f32)
        pooled = win_sum * inv_cnt - u_g
        y = _bdot(wgrp_ref[gi], pooled.astype(bf16)) * ps_ref[rows, :]
        op_sc[rows, :] = y.astype(bf16)

    key_pos = lax.broadcasted_iota(jnp.int32, (MOBA_BLOCK, TOKENS), 0)
    qry_pos = lax.broadcasted_iota(jnp.int32, (MOBA_BLOCK, TOKENS), 1)
    causal = key_pos <= qry_pos

    def score_head(j, slot, h, own=False):
        s = _bdot(kbuf[j, h], qbuf[h])
        if own:
            s = jnp.where(causal, s, MASKED)
        s_ref[slot, h] = s
        m_new = jnp.max(s, axis=0, keepdims=True)
        if not own:
            m_old = m_ref[1 - slot, h:h + 1, :]
            m_new = jnp.maximum(m_old, m_new)
            alpha_ref[slot, h:h + 1, :] = jnp.exp2(m_old - m_new)
        m_ref[slot, h:h + 1, :] = m_new

    def value_head(j, slot, h, own=False):
        p = jnp.exp2(s_ref[slot, h] - m_ref[slot, h:h + 1, :]).astype(bf16)
        pv = _bdot(vbuf[j, h], p)
        acc_ref[h] = pv if own else acc_ref[h] * alpha_ref[slot, h:h + 1, :] + pv

    def score_pass(j, slot, own=False):
        for h in range(HEADS):
            score_head(j, slot, h, own)

    def value_pass(j, slot, own=False):
        for h in range(HEADS):
            value_head(j, slot, h, own)

    score_pass(i, 1, own=True)

    fT = f_sc[...]
    r2_prev = r2_sc[0:1, :]
    for c in range(D_MODEL // OUT_CHUNK):
        chunk = slice(c * OUT_CHUNK, (c + 1) * OUT_CHUNK)
        for rows, y in zip(_halves(chunk), _wdot2(wff2_ref, chunk, fT)):
            o_ref[:, rows] = (x1_sc[rows, :] + y * r2_prev).T

    value_pass(i, 1, own=True)

    n_past = jnp.where(n == n_tiles, 0, i)

    def gate_pooling_branch():
        for rows, upp in zip(_halves(all_rows), _wdot2(wupp_ref, all_rows, op_sc[...])):
            sgp_sc[rows, :] = sgp_sc[rows, :] * upp

    def visit(blocks):
        if blocks:
            score_pass(blocks[0], 0)
        for k in range(1, len(blocks)):
            for h in range(HEADS):
                score_head(blocks[k], k % 2, h)
                value_head(blocks[k - 1], (k - 1) % 2, h)
        gate_pooling_branch()
        if blocks:
            value_pass(blocks[-1], (len(blocks) - 1) % 2)

    for count in range(n_blocks):
        pl.when(n_past == count)(lambda count=count: visit(list(range(count))))

    for h in range(HEADS):
        o_h = acc_ref[h, :HEAD_DIM, :] * (1.0 / acc_ref[h, HEAD_DIM:HEAD_DIM + 1, :])
        o_sc[h * HEAD_DIM:(h + 1) * HEAD_DIM, :] = o_h.astype(bf16)

    for rows, upa in zip(_halves(all_rows), _wdot2(wupa_ref, all_rows, o_sc[...])):
        mg_sc[rows, :] = (sga_sc[rows, :] * upa + sgp_sc[rows, :]).astype(bf16)
    sumsq = jnp.zeros((1, TOKENS), f32)
    for rows, d in zip(_halves(all_rows), _wdot2(wout_ref, all_rows, mg_sc[...])):
        x1 = xT[rows, :] + d
        x1_sc[rows, :] = x1
        h2_sc[rows, :] = (x1 * gmlp_ref[rows, :]).astype(bf16)
        sumsq = sumsq + jnp.sum(x1 * x1, axis=0, keepdims=True)
    r = lax.rsqrt(sumsq * (1.0 / D_MODEL) + RMS_EPS)
    r2_sc[...] = jnp.broadcast_to(r * r, r2_sc.shape)


def _bcast_cols(v, n):
    return jnp.broadcast_to(v.astype(jnp.float32)[:, None], (v.shape[0], n))


@jax.jit
def kernel(x, norm_mix, w_in, q_norm, k_norm, w_pool_grp, pool_scale, w_up_attn, w_up_pool,
           w_out, norm_mlp, w_ff1, w_ff2):
    batch, seq, d_model = x.shape
    assert d_model == D_MODEL and seq % TOKENS == 0 and w_in.shape[0] == 1
    n_blocks = seq // TOKENS
    assert n_blocks <= BIAS_ROWS
    bf16 = jnp.bfloat16

    half = jnp.arange(ROPE_HALF, dtype=jnp.float32)
    inv_freq = ROPE_THETA ** (-half / ROPE_HALF)
    ang = inv_freq[:, None] * jnp.arange(seq).astype(jnp.float32)[None, :]

    n_tiles = batch * n_blocks

    def tile(n):
        t = jnp.clip(n, 0, n_tiles - 1)
        return t // n_blocks, t % n_blocks

    whole = pl.BlockSpec(memory_space=pltpu.VMEM)
    in_hbm = pl.BlockSpec(memory_space=pl.ANY)
    weights = (w_in, w_up_attn, w_up_pool, w_out, w_ff1, w_ff2)
    x_spec = pl.BlockSpec((None, TOKENS, D_MODEL), lambda n: (*tile(n), 0))
    out_spec = pl.BlockSpec((None, TOKENS, D_MODEL), lambda n: (*tile(n - 1), 0))
    rope_spec = pl.BlockSpec((2 * ROPE_HALF, TOKENS), lambda n: (0, tile(n)[1]))
    gains = jnp.concatenate([norm_mix[0], norm_mlp[0], q_norm[0], k_norm[0], pool_scale[0]])

    return pl.pallas_call(
        _layer_kernel,
        grid=(n_tiles + 1,),
        in_specs=[x_spec, rope_spec, whole, whole] + [in_hbm] * len(weights),
        out_specs=out_spec,
        out_shape=jax.ShapeDtypeStruct(x.shape, x.dtype),
        scratch_shapes=[
            *[pltpu.VMEM((w.shape[2], w.shape[1]), bf16) for w in weights],
            pltpu.VMEM(w_pool_grp.shape[1:], bf16),
            pltpu.SemaphoreType.DMA((2 * HEADS,)),
            pltpu.VMEM((n_blocks, HEADS, MOBA_BLOCK, KQ_DIM), bf16),
            pltpu.VMEM((n_blocks, HEADS, V_ROWS, MOBA_BLOCK), bf16),
            pltpu.VMEM((HEADS, 2 * BIAS_ROWS, KQ_DIM), jnp.float32),
            pltpu.VMEM((POOL_WIDTH, LANES), jnp.float32),
            pltpu.VMEM((HEADS, KQ_DIM, TOKENS), bf16),
            pltpu.VMEM((2, HEADS, TOKENS), jnp.float32),
            pltpu.VMEM((2, HEADS, TOKENS), jnp.float32),
            pltpu.VMEM((HEADS, V_ROWS, TOKENS), jnp.float32),
            pltpu.VMEM((2, HEADS, MOBA_BLOCK, TOKENS), jnp.float32),
            pltpu.VMEM((ATTN_WIDTH, TOKENS), bf16),
            pltpu.VMEM((POOL_WIDTH, TOKENS), bf16),
            pltpu.VMEM((D_MODEL, TOKENS), jnp.float32),
            pltpu.VMEM((D_MODEL, TOKENS), jnp.float32),
            pltpu.VMEM((D_MODEL, TOKENS), bf16),
            pltpu.VMEM((D_MODEL, TOKENS), jnp.float32),
            pltpu.VMEM((D_MODEL, TOKENS), bf16),
            pltpu.VMEM((16, TOKENS), jnp.float32),
            pltpu.VMEM((D_FF, TOKENS), bf16),
        ],
        compiler_params=pltpu.CompilerParams(
            dimension_semantics=("arbitrary",),
            vmem_limit_bytes=VMEM_LIMIT_BYTES),
        name="moba_pool_layer",
    )(x, jnp.concatenate([jnp.cos(ang), jnp.sin(ang)]), _bcast_cols(gains, TOKENS),
      w_pool_grp, *weights)
```

```python
import jax
import jax.numpy as jnp
from jax import lax
from jax.experimental import pallas as pl
from jax.experimental.pallas import tpu as pltpu

D_MODEL = 1024
HEADS = 8
HEAD_DIM = 64
ATTN_WIDTH = HEADS * HEAD_DIM
MOBA_BLOCK = 256
MOBA_TOPK = 3
ROPE_THETA = 500000.0
ROPE_HALF = HEAD_DIM // 8
POOL_WINDOWS = (2, 4, 8, 16)
POOL_GROUP_WIDTH = 128
POOL_WIDTH = len(POOL_WINDOWS) * POOL_GROUP_WIDTH
D_FF = 4 * D_MODEL
RMS_EPS = 1e-6

LANES = 128
TOKENS = MOBA_BLOCK
MASKED = -1e30
KQ_DIM = LANES
BIAS_ROWS = 8
V_ROWS = HEAD_DIM + 16
Q_SCALE = HEAD_DIM ** -0.5 * 1.4426950408889634
FF_CHUNK = 1024
OUT_CHUNK = 512
VMEM_LIMIT_BYTES = 61 * 1024 * 1024

_Q0, _K0, _V0, _U0 = 0, ATTN_WIDTH, 2 * ATTN_WIDTH, 3 * ATTN_WIDTH
_GA0 = 3 * ATTN_WIDTH + POOL_WIDTH
_GP0 = _GA0 + D_MODEL
IN_WIDTH = _GP0 + D_MODEL
_GAIN_ROWS = (0, D_MODEL, 2 * D_MODEL, 2 * D_MODEL + HEAD_DIM, 2 * D_MODEL + 2 * HEAD_DIM,
              2 * D_MODEL + 2 * HEAD_DIM + POOL_WIDTH)


def _bdot(a, b):
    return jnp.dot(a, b, preferred_element_type=jnp.float32)


def _wdot2(w_ref, rows, act, cols=slice(None)):
    mid = (rows.start + rows.stop) // 2
    return [_bdot(w_ref[rows.start:mid, cols], act), _bdot(w_ref[mid:rows.stop, cols], act)]


def _halves(rows):
    mid = (rows.start + rows.stop) // 2
    return [slice(rows.start, mid), slice(mid, rows.stop)]


def _sigmoid(x):
    return 0.5 * jnp.tanh(0.5 * x) + 0.5


def _rms_scale(t):
    return lax.rsqrt(jnp.mean(t * t, axis=0, keepdims=True) + RMS_EPS)


def _head_norm_rope(t, gain, cos, sin):
    y = t * _rms_scale(t) * gain
    x1 = y[0:ROPE_HALF, :]
    x2 = y[ROPE_HALF:2 * ROPE_HALF, :]
    return jnp.concatenate(
        [x1 * cos - x2 * sin, x2 * cos + x1 * sin, y[2 * ROPE_HALF:, :]], axis=0)


def _shift_tokens(tiles, shift):
    lane = lax.broadcasted_iota(jnp.int32, tiles[0].shape, 1)
    rolled = [pltpu.roll(t, shift, axis=1) for t in tiles]
    out = [rolled[0]]
    for c in range(1, len(tiles)):
        out.append(jnp.where(lane < shift, rolled[c - 1], rolled[c]))
    return out


def _load_weights(pairs, stage, sems):
    side = stage.shape[-1]
    slots_per_row = stage.shape[1]
    n_slots = stage.shape[0] * slots_per_row
    tiles = [(src, dst, k0, n0)
             for src, dst in pairs
             for k0 in range(0, dst.shape[1], side)
             for n0 in range(0, dst.shape[0], side)]

    def slot_of(t):
        slot = t % n_slots
        return slot, stage.at[slot // slots_per_row, slot % slots_per_row]

    def copy(t):
        src, _, k0, n0 = tiles[t]
        slot, buf = slot_of(t)
        return pltpu.make_async_copy(src.at[0, pl.ds(k0, side), pl.ds(n0, side)], buf, sems.at[slot])

    for t in range(min(n_slots, len(tiles))):
        copy(t).start()
    group = 4
    for t0 in range(0, len(tiles), group):
        ts = range(t0, min(t0 + group, len(tiles)))
        for t in ts:
            copy(t).wait()
        for t in ts:
            _, dst, k0, n0 = tiles[t]
            dst[n0:n0 + side, k0:k0 + side] = slot_of(t)[1][...].T.astype(dst.dtype)
        for t in ts:
            if t + n_slots < len(tiles):
                copy(t + n_slots).start()


def _layer_kernel(x_ref, rope_ref, gains_ref, wgrp_in,
                  win_hbm, wupa_hbm, wupp_hbm, wout_hbm, wff1_hbm, wff2_hbm,
                  o_ref,
                  win_ref, wupa_ref, wupp_ref, wout_ref, wff1_ref, wff2_ref, wgrp_ref, wsem,
                  kbuf, vbuf, kmbuf, uprev, qbuf, m_ref, alpha_ref, acc_ref, s_ref,
                  o_sc, op_sc, sga_sc, sgp_sc, mg_sc, x1_sc, h2_sc, r2_sc, f_sc):
    f32, bf16 = jnp.float32, jnp.bfloat16
    gmix_ref, gmlp_ref, qn_ref, kn_ref, ps_ref = (
        gains_ref.at[lo:hi] for lo, hi in zip(_GAIN_ROWS[:-1], _GAIN_ROWS[1:]))
    n = pl.program_id(0)
    n_tiles = pl.num_programs(0) - 1
    n_blocks = kbuf.shape[0]
    i = lax.rem(jnp.minimum(n, n_tiles - 1), n_blocks)

    @pl.when(n == 0)
    def _first_step():
        _load_weights([(win_hbm, win_ref), (wupa_hbm, wupa_ref), (wupp_hbm, wupp_ref),
                       (wout_hbm, wout_ref), (wff1_hbm, wff1_ref), (wff2_hbm, wff2_ref)],
                      s_ref, wsem)
        for g in range(len(POOL_WINDOWS)):
            wgrp_ref[g] = wgrp_in[0, g].T.astype(bf16)
        x1_sc[...] = jnp.zeros_like(x1_sc)
        h2_sc[...] = jnp.zeros_like(h2_sc)
        r2_sc[...] = jnp.zeros_like(r2_sc)

    @pl.when(i == 0)
    def _start_of_sequence():
        kmbuf[...] = jnp.zeros_like(kmbuf)
        uprev[...] = jnp.zeros_like(uprev)

    h2T = h2_sc[...]
    for c in range(D_FF // FF_CHUNK):
        chunk = slice(c * FF_CHUNK, (c + 1) * FF_CHUNK)
        for rows, z in zip(_halves(chunk), _wdot2(wff1_ref, chunk, h2T)):
            f = jnp.maximum(z, 0.0)
            f_sc[rows, :] = (f * f).astype(bf16)

    xT = x_ref[...].T
    hT = (xT * _rms_scale(xT) * gmix_ref[...]).astype(bf16)

    def proj2(lo, width):
        return _wdot2(win_ref, slice(lo, lo + width), hT)

    def head_rows(parts, h):
        per_part = HEADS // 2
        hh = h % per_part
        return parts[h // per_part][hh * HEAD_DIM:(hh + 1) * HEAD_DIM, :]

    cos = rope_ref[:ROPE_HALF, :]
    sin = rope_ref[ROPE_HALF:, :]

    k_parts = proj2(_K0, ATTN_WIDTH)
    q_parts = proj2(_Q0, ATTN_WIDTH)
    v_parts = proj2(_V0, ATTN_WIDTH)
    u_parts = proj2(_U0, POOL_WIDTH)
    all_rows = slice(0, D_MODEL)
    for rows, ga, gp in zip(_halves(all_rows), proj2(_GA0, D_MODEL), proj2(_GP0, D_MODEL)):
        sga_sc[rows, :] = _sigmoid(ga)
        sgp_sc[rows, :] = _sigmoid(gp)
    pad_row = lax.broadcasted_iota(jnp.int32, (KQ_DIM - HEAD_DIM, TOKENS), 0)
    own_block = jnp.where(pad_row == i, 1.0, 0.0)
    ones_row = jnp.where(
        lax.broadcasted_iota(jnp.int32, (V_ROWS - HEAD_DIM, TOKENS), 0) == 0, 1.0, 0.0)
    for h in range(HEADS):
        k_h = _head_norm_rope(head_rows(k_parts, h), kn_ref[...], cos, sin)
        k_tok = jnp.concatenate([k_h, own_block], axis=0).T
        kbuf[i, h] = k_tok.astype(bf16)
        kmean = jnp.sum(k_tok, axis=0, keepdims=True) * (1.0 / MOBA_BLOCK)
        kmbuf[h, pl.ds(i, 1), :] = kmean
        vbuf[i, h, :HEAD_DIM, :] = head_rows(v_parts, h).astype(bf16)
        vbuf[i, h, HEAD_DIM:, :] = ones_row.astype(bf16)

    blk = lax.broadcasted_iota(jnp.int32, (BIAS_ROWS, TOKENS), 0)
    fully_past = blk < i
    for h in range(HEADS):
        q_h = _head_norm_rope(head_rows(q_parts, h), qn_ref[...], cos, sin)
        q_h = (q_h * Q_SCALE).astype(bf16)
        gate = _bdot(kmbuf[h, :, :HEAD_DIM].astype(bf16), q_h)[:BIAS_ROWS, :]
        g = jnp.where(fully_past, gate, -jnp.inf)
        rank = jnp.zeros((BIAS_ROWS, TOKENS), jnp.int32)
        for jp in range(n_blocks - 1):
            g_jp = g[jp:jp + 1, :]
            ahead = jnp.logical_or(g_jp > g, jnp.logical_and(g_jp == g, blk > jp))
            rank = rank + ahead.astype(jnp.int32)
        chosen = jnp.logical_and(rank < MOBA_TOPK, fully_past)
        bias = jnp.where(jnp.logical_or(chosen, blk == i), 0.0, MASKED)
        zeros = jnp.zeros((KQ_DIM - HEAD_DIM - BIAS_ROWS, TOKENS), f32)
        qbuf[h, :HEAD_DIM, :] = q_h
        qbuf[h, HEAD_DIM:, :] = jnp.concatenate([bias, zeros], axis=0).astype(bf16)

    tok = i * TOKENS + lax.broadcasted_iota(jnp.int32, (1, TOKENS), 1)
    groups_per_part = len(POOL_WINDOWS) // 2
    for gi, w in enumerate(POOL_WINDOWS):
        rows = slice(gi * POOL_GROUP_WIDTH, (gi + 1) * POOL_GROUP_WIDTH)
        gg = gi % groups_per_part
        u_g = u_parts[gi // groups_per_part][gg * POOL_GROUP_WIDTH:(gg + 1) * POOL_GROUP_WIDTH, :]
        tiles = [uprev[rows, :]] + [u_g[:, c * LANES:(c + 1) * LANES]
                                    for c in range(TOKENS // LANES)]
        uprev[rows, :] = tiles[-1]
        shift = 1
        while shift < w:
            shifted = _shift_tokens(tiles, shift)
            tiles = [a + b for a, b in zip(tiles, shifted)]
            shift *= 2
        win_sum = jnp.concatenate(tiles[1:], axis=1)
        inv_cnt = 1.0 / jnp.minimum(tok + 1, w).astype(f32)
        pooled = win_sum * inv_cnt - u_g
        y = _bdot(wgrp_ref[gi], pooled.astype(bf16)) * ps_ref[rows, :]
        op_sc[rows, :] = y.astype(bf16)

    key_pos = lax.broadcasted_iota(jnp.int32, (MOBA_BLOCK, TOKENS), 0)
    qry_pos = lax.broadcasted_iota(jnp.int32, (MOBA_BLOCK, TOKENS), 1)
    causal = key_pos <= qry_pos

    def score_head(j, slot, h, own=False):
        s = _bdot(kbuf[j, h], qbuf[h])
        if own:
            s = jnp.where(causal, s, MASKED)
        s_ref[slot, h] = s
        m_new = jnp.max(s, axis=0, keepdims=True)
        if not own:
            m_old = m_ref[1 - slot, h:h + 1, :]
            m_new = jnp.maximum(m_old, m_new)
            alpha_ref[slot, h:h + 1, :] = jnp.exp2(m_old - m_new)
        m_ref[slot, h:h + 1, :] = m_new

    def value_head(j, slot, h, own=False):
        p = jnp.exp2(s_ref[slot, h] - m_ref[slot, h:h + 1, :]).astype(bf16)
        pv = _bdot(vbuf[j, h], p)
        acc_ref[h] = pv if own else acc_ref[h] * alpha_ref[slot, h:h + 1, :] + pv

    def score_pass(j, slot, own=False):
        for h in range(HEADS):
            score_head(j, slot, h, own)

    def value_pass(j, slot, own=False):
        for h in range(HEADS):
            value_head(j, slot, h, own)

    score_pass(i, 1, own=True)

    n_past = jnp.where(n == n_tiles, 0, i)

    def gate_pooling_branch():
        for rows, upp in zip(_halves(all_rows), _wdot2(wupp_ref, all_rows, op_sc[...])):
            sgp_sc[rows, :] = sgp_sc[rows, :] * upp

    def visit(blocks):
        pending = (i, 1, True)
        for k, t in enumerate(blocks):
            for h in range(HEADS):
                score_head(t, k % 2, h)
                value_head(pending[0], pending[1], h, pending[2])
            pending = (t, k % 2, False)
        gate_pooling_branch()
        value_pass(*pending)

    for count in range(n_blocks):
        pl.when(n_past == count)(lambda count=count: visit(list(range(count))))

    fT = f_sc[...]
    x1_prev = x1_sc[...]
    r2_prev = r2_sc[0:1, :]

    def mlp_out(c):
        chunk = slice(c * OUT_CHUNK, (c + 1) * OUT_CHUNK)
        for rows, y in zip(_halves(chunk), _wdot2(wff2_ref, chunk, fT)):
            o_ref[:, rows] = (x1_prev[rows, :] + y * r2_prev).T

    n_out_chunks = D_MODEL // OUT_CHUNK
    for c in range(n_out_chunks // 2):
        mlp_out(c)

    for h in range(HEADS):
        o_h = acc_ref[h, :HEAD_DIM, :] * (1.0 / acc_ref[h, HEAD_DIM:HEAD_DIM + 1, :])
        o_sc[h * HEAD_DIM:(h + 1) * HEAD_DIM, :] = o_h.astype(bf16)

    upa_parts = _wdot2(wupa_ref, all_rows, o_sc[...])
    for c in range(n_out_chunks // 2, n_out_chunks):
        mlp_out(c)
    for rows, upa in zip(_halves(all_rows), upa_parts):
        mg_sc[rows, :] = (sga_sc[rows, :] * upa + sgp_sc[rows, :]).astype(bf16)
    sumsq = jnp.zeros((1, TOKENS), f32)
    for rows, d in zip(_halves(all_rows), _wdot2(wout_ref, all_rows, mg_sc[...])):
        x1 = xT[rows, :] + d
        x1_sc[rows, :] = x1
        h2_sc[rows, :] = (x1 * gmlp_ref[rows, :]).astype(bf16)
        sumsq = sumsq + jnp.sum(x1 * x1, axis=0, keepdims=True)
    r = lax.rsqrt(sumsq * (1.0 / D_MODEL) + RMS_EPS)
    r2_sc[...] = jnp.broadcast_to(r * r, r2_sc.shape)


def _bcast_cols(v, n):
    return jnp.broadcast_to(v.astype(jnp.float32)[:, None], (v.shape[0], n))


@jax.jit
def kernel(x, norm_mix, w_in, q_norm, k_norm, w_pool_grp, pool_scale, w_up_attn, w_up_pool,
           w_out, norm_mlp, w_ff1, w_ff2):
    batch, seq, d_model = x.shape
    assert d_model == D_MODEL and seq % TOKENS == 0 and w_in.shape[0] == 1
    n_blocks = seq // TOKENS
    assert n_blocks <= BIAS_ROWS
    bf16 = jnp.bfloat16

    half = jnp.arange(ROPE_HALF, dtype=jnp.float32)
    inv_freq = ROPE_THETA ** (-half / ROPE_HALF)
    ang = inv_freq[:, None] * jnp.arange(seq).astype(jnp.float32)[None, :]

    n_tiles = batch * n_blocks

    def tile(n):
        t = jnp.clip(n, 0, n_tiles - 1)
        return t // n_blocks, t % n_blocks

    whole = pl.BlockSpec(memory_space=pltpu.VMEM)
    in_hbm = pl.BlockSpec(memory_space=pl.ANY)
    weights = (w_in, w_up_attn, w_up_pool, w_out, w_ff1, w_ff2)
    x_spec = pl.BlockSpec((None, TOKENS, D_MODEL), lambda n: (*tile(n), 0))
    out_spec = pl.BlockSpec((None, TOKENS, D_MODEL), lambda n: (*tile(n - 1), 0))
    rope_spec = pl.BlockSpec((2 * ROPE_HALF, TOKENS), lambda n: (0, tile(n)[1]))
    gains = jnp.concatenate([norm_mix[0], norm_mlp[0], q_norm[0], k_norm[0], pool_scale[0]])

    return pl.pallas_call(
        _layer_kernel,
        grid=(n_tiles + 1,),
        in_specs=[x_spec, rope_spec, whole, whole] + [in_hbm] * len(weights),
        out_specs=out_spec,
        out_shape=jax.ShapeDtypeStruct(x.shape, x.dtype),
        scratch_shapes=[
            *[pltpu.VMEM((w.shape[2], w.shape[1]), bf16) for w in weights],
            pltpu.VMEM(w_pool_grp.shape[1:], bf16),
            pltpu.SemaphoreType.DMA((2 * HEADS,)),
            pltpu.VMEM((n_blocks, HEADS, MOBA_BLOCK, KQ_DIM), bf16),
            pltpu.VMEM((n_blocks, HEADS, V_ROWS, MOBA_BLOCK), bf16),
            pltpu.VMEM((HEADS, 2 * BIAS_ROWS, KQ_DIM), jnp.float32),
            pltpu.VMEM((POOL_WIDTH, LANES), jnp.float32),
            pltpu.VMEM((HEADS, KQ_DIM, TOKENS), bf16),
            pltpu.VMEM((2, HEADS, TOKENS), jnp.float32),
            pltpu.VMEM((2, HEADS, TOKENS), jnp.float32),
            pltpu.VMEM((HEADS, V_ROWS, TOKENS), jnp.float32),
            pltpu.VMEM((2, HEADS, MOBA_BLOCK, TOKENS), jnp.float32),
            pltpu.VMEM((ATTN_WIDTH, TOKENS), bf16),
            pltpu.VMEM((POOL_WIDTH, TOKENS), bf16),
            pltpu.VMEM((D_MODEL, TOKENS), jnp.float32),
            pltpu.VMEM((D_MODEL, TOKENS), jnp.float32),
            pltpu.VMEM((D_MODEL, TOKENS), bf16),
            pltpu.VMEM((D_MODEL, TOKENS), jnp.float32),
            pltpu.VMEM((D_MODEL, TOKENS), bf16),
            pltpu.VMEM((16, TOKENS), jnp.float32),
            pltpu.VMEM((D_FF, TOKENS), bf16),
        ],
        compiler_params=pltpu.CompilerParams(
            dimension_semantics=("arbitrary",),
            vmem_limit_bytes=VMEM_LIMIT_BYTES),
        name="moba_pool_layer",
    )(x, jnp.concatenate([jnp.cos(ang), jnp.sin(ang)]), _bcast_cols(gains, TOKENS),
      w_pool_grp, *weights)
```

```python
import jax
import jax.numpy as jnp
from jax import lax
from jax.experimental import pallas as pl
from jax.experimental.pallas import tpu as pltpu

D_MODEL = 1024
HEADS = 8
HEAD_DIM = 64
ATTN_WIDTH = HEADS * HEAD_DIM
MOBA_BLOCK = 256
MOBA_TOPK = 3
ROPE_THETA = 500000.0
ROPE_HALF = HEAD_DIM // 8
POOL_WINDOWS = (2, 4, 8, 16)
POOL_GROUP_WIDTH = 128
POOL_WIDTH = len(POOL_WINDOWS) * POOL_GROUP_WIDTH
D_FF = 4 * D_MODEL
RMS_EPS = 1e-6

LANES = 128
TOKENS = MOBA_BLOCK
MASKED = -1e30
KQ_DIM = LANES
BIAS_ROWS = 8
V_ROWS = HEAD_DIM + 16
Q_SCALE = HEAD_DIM ** -0.5 * 1.4426950408889634
FF_CHUNK = 1024
OUT_CHUNK = 512
VMEM_LIMIT_BYTES = 61 * 1024 * 1024

_Q0, _K0, _V0, _U0 = 0, ATTN_WIDTH, 2 * ATTN_WIDTH, 3 * ATTN_WIDTH
_GA0 = 3 * ATTN_WIDTH + POOL_WIDTH
_GP0 = _GA0 + D_MODEL
IN_WIDTH = _GP0 + D_MODEL
_GAIN_ROWS = (0, D_MODEL, 2 * D_MODEL, 2 * D_MODEL + HEAD_DIM, 2 * D_MODEL + 2 * HEAD_DIM,
              2 * D_MODEL + 2 * HEAD_DIM + POOL_WIDTH)


def _bdot(a, b):
    return jnp.dot(a, b, preferred_element_type=jnp.float32)


def _wdot2(w_ref, rows, act, cols=slice(None)):
    mid = (rows.start + rows.stop) // 2
    return [_bdot(w_ref[rows.start:mid, cols], act), _bdot(w_ref[mid:rows.stop, cols], act)]


def _halves(rows):
    mid = (rows.start + rows.stop) // 2
    return [slice(rows.start, mid), slice(mid, rows.stop)]


def _sigmoid(x):
    return 0.5 * jnp.tanh(0.5 * x) + 0.5


def _rms_scale(t):
    return lax.rsqrt(jnp.mean(t * t, axis=0, keepdims=True) + RMS_EPS)


def _head_norm_rope(t, gain, cos, sin):
    y = t * _rms_scale(t) * gain
    x1 = y[0:ROPE_HALF, :]
    x2 = y[ROPE_HALF:2 * ROPE_HALF, :]
    return jnp.concatenate(
        [x1 * cos - x2 * sin, x2 * cos + x1 * sin, y[2 * ROPE_HALF:, :]], axis=0)


def _shift_tokens(tiles, shift):
    lane = lax.broadcasted_iota(jnp.int32, tiles[0].shape, 1)
    rolled = [pltpu.roll(t, shift, axis=1) for t in tiles]
    out = [rolled[0]]
    for c in range(1, len(tiles)):
        out.append(jnp.where(lane < shift, rolled[c - 1], rolled[c]))
    return out


def _load_weights(pairs, stage, sems):
    side = stage.shape[-1]
    slots_per_row = stage.shape[1]
    n_slots = stage.shape[0] * slots_per_row
    tiles = [(src, dst, k0, n0)
             for src, dst in pairs
             for k0 in range(0, dst.shape[1], side)
             for n0 in range(0, dst.shape[0], side)]

    def slot_of(t):
        slot = t % n_slots
        return slot, stage.at[slot // slots_per_row, slot % slots_per_row]

    def copy(t):
        src, _, k0, n0 = tiles[t]
        slot, buf = slot_of(t)
        return pltpu.make_async_copy(src.at[0, pl.ds(k0, side), pl.ds(n0, side)], buf, sems.at[slot])

    for t in range(min(n_slots, len(tiles))):
        copy(t).start()
    group = 4
    for t0 in range(0, len(tiles), group):
        ts = range(t0, min(t0 + group, len(tiles)))
        for t in ts:
            copy(t).wait()
        for t in ts:
            _, dst, k0, n0 = tiles[t]
            dst[n0:n0 + side, k0:k0 + side] = slot_of(t)[1][...].T.astype(dst.dtype)
        for t in ts:
            if t + n_slots < len(tiles):
                copy(t + n_slots).start()


def _layer_kernel(x_ref, rope_ref, gains_ref, wgrp_in,
                  win_hbm, wupa_hbm, wupp_hbm, wout_hbm, wff1_hbm, wff2_hbm,
                  o_ref,
                  win_ref, wupa_ref, wupp_ref, wout_ref, wff1_ref, wff2_ref, wgrp_ref, wsem,
                  kbuf, vbuf, kmbuf, uprev, qbuf, m_ref, alpha_ref, acc_ref, s_ref,
                  o_sc, op_sc, sga_sc, sgp_sc, mg_sc, x1_sc, h2_sc, r2_sc, f_sc, xT_sc):
    f32, bf16 = jnp.float32, jnp.bfloat16
    gmix_ref, gmlp_ref, qn_ref, kn_ref, ps_ref = (
        gains_ref.at[lo:hi] for lo, hi in zip(_GAIN_ROWS[:-1], _GAIN_ROWS[1:]))
    n = pl.program_id(0)
    n_tiles = pl.num_programs(0) - 1
    n_blocks = kbuf.shape[0]
    i = lax.rem(jnp.minimum(n, n_tiles - 1), n_blocks)

    @pl.when(n == 0)
    def _first_step():
        _load_weights([(win_hbm, win_ref), (wupa_hbm, wupa_ref), (wupp_hbm, wupp_ref),
                       (wout_hbm, wout_ref), (wff1_hbm, wff1_ref), (wff2_hbm, wff2_ref)],
                      s_ref, wsem)
        for g in range(len(POOL_WINDOWS)):
            wgrp_ref[g] = wgrp_in[0, g].T.astype(bf16)
        x1_sc[...] = jnp.zeros_like(x1_sc)
        h2_sc[...] = jnp.zeros_like(h2_sc)
        r2_sc[...] = jnp.zeros_like(r2_sc)

    @pl.when(i == 0)
    def _start_of_sequence():
        kmbuf[...] = jnp.zeros_like(kmbuf)
        uprev[...] = jnp.zeros_like(uprev)

    def score_head(j, slot, h, own=False):
        s = _bdot(kbuf[j, h], qbuf[h])
        if own:
            key_pos = lax.broadcasted_iota(jnp.int32, (MOBA_BLOCK, TOKENS), 0)
            qry_pos = lax.broadcasted_iota(jnp.int32, (MOBA_BLOCK, TOKENS), 1)
            s = jnp.where(key_pos <= qry_pos, s, MASKED)
        s_ref[slot, h] = s
        m_new = jnp.max(s, axis=0, keepdims=True)
        if not own:
            m_old = m_ref[1 - slot, h:h + 1, :]
            m_new = jnp.maximum(m_old, m_new)
            alpha_ref[slot, h:h + 1, :] = jnp.exp2(m_old - m_new)
        m_ref[slot, h:h + 1, :] = m_new

    def value_head(j, slot, h, own=False):
        p = jnp.exp2(s_ref[slot, h] - m_ref[slot, h:h + 1, :]).astype(bf16)
        pv = _bdot(vbuf[j, h], p)
        acc_ref[h] = pv if own else acc_ref[h] * alpha_ref[slot, h:h + 1, :] + pv

    def score_pass(j, slot, own=False):
        for h in range(HEADS):
            score_head(j, slot, h, own)

    def value_pass(j, slot, own=False):
        for h in range(HEADS):
            value_head(j, slot, h, own)

    all_rows = slice(0, D_MODEL)
    mixing = n < n_tiles

    def mlp_hidden():
        h2T = h2_sc[...]
        for c in range(D_FF // FF_CHUNK):
            chunk = slice(c * FF_CHUNK, (c + 1) * FF_CHUNK)
            for rows, z in zip(_halves(chunk), _wdot2(wff1_ref, chunk, h2T)):
                f = jnp.maximum(z, 0.0)
                f_sc[rows, :] = (f * f).astype(bf16)

    def mix_front():
        xT = x_ref[...].T
        xT_sc[...] = xT
        hT = (xT * _rms_scale(xT) * gmix_ref[...]).astype(bf16)

        def proj2(lo, width):
            return _wdot2(win_ref, slice(lo, lo + width), hT)

        def head_rows(parts, h):
            per_part = HEADS // 2
            hh = h % per_part
            return parts[h // per_part][hh * HEAD_DIM:(hh + 1) * HEAD_DIM, :]

        cos = rope_ref[:ROPE_HALF, :]
        sin = rope_ref[ROPE_HALF:, :]

        k_parts = proj2(_K0, ATTN_WIDTH)
        q_parts = proj2(_Q0, ATTN_WIDTH)
        v_parts = proj2(_V0, ATTN_WIDTH)
        u_parts = proj2(_U0, POOL_WIDTH)
        for rows, ga, gp in zip(_halves(all_rows), proj2(_GA0, D_MODEL), proj2(_GP0, D_MODEL)):
            sga_sc[rows, :] = _sigmoid(ga)
            sgp_sc[rows, :] = _sigmoid(gp)
        pad_row = lax.broadcasted_iota(jnp.int32, (KQ_DIM - HEAD_DIM, TOKENS), 0)
        own_block = jnp.where(pad_row == i, 1.0, 0.0)
        ones_row = jnp.where(
            lax.broadcasted_iota(jnp.int32, (V_ROWS - HEAD_DIM, TOKENS), 0) == 0, 1.0, 0.0)
        for h in range(HEADS):
            k_h = _head_norm_rope(head_rows(k_parts, h), kn_ref[...], cos, sin)
            k_tok = jnp.concatenate([k_h, own_block], axis=0).T
            kbuf[i, h] = k_tok.astype(bf16)
            kmean = jnp.sum(k_tok, axis=0, keepdims=True) * (1.0 / MOBA_BLOCK)
            kmbuf[h, pl.ds(i, 1), :] = kmean
            vbuf[i, h, :HEAD_DIM, :] = head_rows(v_parts, h).astype(bf16)
            vbuf[i, h, HEAD_DIM:, :] = ones_row.astype(bf16)

        blk = lax.broadcasted_iota(jnp.int32, (BIAS_ROWS, TOKENS), 0)
        fully_past = blk < i
        for h in range(HEADS):
            q_h = _head_norm_rope(head_rows(q_parts, h), qn_ref[...], cos, sin)
            q_h = (q_h * Q_SCALE).astype(bf16)
            gate = _bdot(kmbuf[h, :, :HEAD_DIM].astype(bf16), q_h)[:BIAS_ROWS, :]
            g = jnp.where(fully_past, gate, -jnp.inf)
            rank = jnp.zeros((BIAS_ROWS, TOKENS), jnp.int32)
            for jp in range(n_blocks - 1):
                g_jp = g[jp:jp + 1, :]
                ahead = jnp.logical_or(g_jp > g, jnp.logical_and(g_jp == g, blk > jp))
                rank = rank + ahead.astype(jnp.int32)
            chosen = jnp.logical_and(rank < MOBA_TOPK, fully_past)
            bias = jnp.where(jnp.logical_or(chosen, blk == i), 0.0, MASKED)
            zeros = jnp.zeros((KQ_DIM - HEAD_DIM - BIAS_ROWS, TOKENS), f32)
            qbuf[h, :HEAD_DIM, :] = q_h
            qbuf[h, HEAD_DIM:, :] = jnp.concatenate([bias, zeros], axis=0).astype(bf16)

        tok = i * TOKENS + lax.broadcasted_iota(jnp.int32, (1, TOKENS), 1)
        groups_per_part = len(POOL_WINDOWS) // 2
        for gi, w in enumerate(POOL_WINDOWS):
            rows = slice(gi * POOL_GROUP_WIDTH, (gi + 1) * POOL_GROUP_WIDTH)
            gg = gi % groups_per_part
            u_g = u_parts[gi // groups_per_part][gg * POOL_GROUP_WIDTH:(gg + 1) * POOL_GROUP_WIDTH, :]
            tiles = [uprev[rows, :]] + [u_g[:, c * LANES:(c + 1) * LANES]
                                        for c in range(TOKENS // LANES)]
            uprev[rows, :] = tiles[-1]
            shift = 1
            while shift < w:
                shifted = _shift_tokens(tiles, shift)
                tiles = [a + b for a, b in zip(tiles, shifted)]
                shift *= 2
            win_sum = jnp.concatenate(tiles[1:], axis=1)
            inv_cnt = 1.0 / jnp.minimum(tok + 1, w).astype(f32)
            pooled = win_sum * inv_cnt - u_g
            y = _bdot(wgrp_ref[gi], pooled.astype(bf16)) * ps_ref[rows, :]
            op_sc[rows, :] = y.astype(bf16)

        score_pass(i, 1, own=True)

    @pl.when(mixing)
    def _mix_front_with_previous_mlp():
        mlp_hidden()
        mix_front()

    @pl.when(jnp.logical_not(mixing))
    def _only_previous_mlp():
        mlp_hidden()

    n_past = i

    def gate_pooling_branch():
        for rows, upp in zip(_halves(all_rows), _wdot2(wupp_ref, all_rows, op_sc[...])):
            sgp_sc[rows, :] = sgp_sc[rows, :] * upp

    def visit(blocks):
        pending = (i, 1, True)
        for k, t in enumerate(blocks):
            for h in range(HEADS):
                score_head(t, k % 2, h)
                value_head(pending[0], pending[1], h, pending[2])
            pending = (t, k % 2, False)
        gate_pooling_branch()
        value_pass(*pending)

    for count in range(n_blocks):
        pl.when(jnp.logical_and(mixing, n_past == count))(
            lambda count=count: visit(list(range(count))))

    n_out_chunks = D_MODEL // OUT_CHUNK

    def mlp_out(c):
        chunk = slice(c * OUT_CHUNK, (c + 1) * OUT_CHUNK)
        r2_prev = r2_sc[0:1, :]
        for rows, y in zip(_halves(chunk), _wdot2(wff2_ref, chunk, f_sc[...])):
            o_ref[:, rows] = (x1_sc[rows, :] + y * r2_prev).T

    @pl.when(mixing)
    def _mix_back_with_previous_mlp():
        for c in range(n_out_chunks // 2):
            mlp_out(c)

        for h in range(HEADS):
            o_h = acc_ref[h, :HEAD_DIM, :] * (1.0 / acc_ref[h, HEAD_DIM:HEAD_DIM + 1, :])
            o_sc[h * HEAD_DIM:(h + 1) * HEAD_DIM, :] = o_h.astype(bf16)

        upa_parts = _wdot2(wupa_ref, all_rows, o_sc[...])
        for c in range(n_out_chunks // 2, n_out_chunks):
            mlp_out(c)
        for rows, upa in zip(_halves(all_rows), upa_parts):
            mg_sc[rows, :] = (sga_sc[rows, :] * upa + sgp_sc[rows, :]).astype(bf16)
        sumsq = jnp.zeros((1, TOKENS), f32)
        for rows, d in zip(_halves(all_rows), _wdot2(wout_ref, all_rows, mg_sc[...])):
            x1 = xT_sc[rows, :] + d
            x1_sc[rows, :] = x1
            h2_sc[rows, :] = (x1 * gmlp_ref[rows, :]).astype(bf16)
            sumsq = sumsq + jnp.sum(x1 * x1, axis=0, keepdims=True)
        r = lax.rsqrt(sumsq * (1.0 / D_MODEL) + RMS_EPS)
        r2_sc[...] = jnp.broadcast_to(r * r, r2_sc.shape)

    @pl.when(jnp.logical_not(mixing))
    def _only_previous_mlp_out():
        for c in range(n_out_chunks):
            mlp_out(c)


def _bcast_cols(v, n):
    return jnp.broadcast_to(v.astype(jnp.float32)[:, None], (v.shape[0], n))


@jax.jit
def kernel(x, norm_mix, w_in, q_norm, k_norm, w_pool_grp, pool_scale, w_up_attn, w_up_pool,
           w_out, norm_mlp, w_ff1, w_ff2):
    batch, seq, d_model = x.shape
    assert d_model == D_MODEL and seq % TOKENS == 0 and w_in.shape[0] == 1
    n_blocks = seq // TOKENS
    assert n_blocks <= BIAS_ROWS
    bf16 = jnp.bfloat16

    half = jnp.arange(ROPE_HALF, dtype=jnp.float32)
    inv_freq = ROPE_THETA ** (-half / ROPE_HALF)
    ang = inv_freq[:, None] * jnp.arange(seq).astype(jnp.float32)[None, :]

    n_tiles = batch * n_blocks

    def tile(n):
        t = jnp.clip(n, 0, n_tiles - 1)
        return t // n_blocks, t % n_blocks

    whole = pl.BlockSpec(memory_space=pltpu.VMEM)
    in_hbm = pl.BlockSpec(memory_space=pl.ANY)
    weights = (w_in, w_up_attn, w_up_pool, w_out, w_ff1, w_ff2)
    x_spec = pl.BlockSpec((None, TOKENS, D_MODEL), lambda n: (*tile(n), 0))
    out_spec = pl.BlockSpec((None, TOKENS, D_MODEL), lambda n: (*tile(n - 1), 0))
    rope_spec = pl.BlockSpec((2 * ROPE_HALF, TOKENS), lambda n: (0, tile(n)[1]))
    gains = jnp.concatenate([norm_mix[0], norm_mlp[0], q_norm[0], k_norm[0], pool_scale[0]])

    return pl.pallas_call(
        _layer_kernel,
        grid=(n_tiles + 1,),
        in_specs=[x_spec, rope_spec, whole, whole] + [in_hbm] * len(weights),
        out_specs=out_spec,
        out_shape=jax.ShapeDtypeStruct(x.shape, x.dtype),
        scratch_shapes=[
            *[pltpu.VMEM((w.shape[2], w.shape[1]), bf16) for w in weights],
            pltpu.VMEM(w_pool_grp.shape[1:], bf16),
            pltpu.SemaphoreType.DMA((2 * HEADS,)),
            pltpu.VMEM((n_blocks, HEADS, MOBA_BLOCK, KQ_DIM), bf16),
            pltpu.VMEM((n_blocks, HEADS, V_ROWS, MOBA_BLOCK), bf16),
            pltpu.VMEM((HEADS, 2 * BIAS_ROWS, KQ_DIM), jnp.float32),
            pltpu.VMEM((POOL_WIDTH, LANES), jnp.float32),
            pltpu.VMEM((HEADS, KQ_DIM, TOKENS), bf16),
            pltpu.VMEM((2, HEADS, TOKENS), jnp.float32),
            pltpu.VMEM((2, HEADS, TOKENS), jnp.float32),
            pltpu.VMEM((HEADS, V_ROWS, TOKENS), jnp.float32),
            pltpu.VMEM((2, HEADS, MOBA_BLOCK, TOKENS), jnp.float32),
            pltpu.VMEM((ATTN_WIDTH, TOKENS), bf16),
            pltpu.VMEM((POOL_WIDTH, TOKENS), bf16),
            pltpu.VMEM((D_MODEL, TOKENS), jnp.float32),
            pltpu.VMEM((D_MODEL, TOKENS), jnp.float32),
            pltpu.VMEM((D_MODEL, TOKENS), bf16),
            pltpu.VMEM((D_MODEL, TOKENS), jnp.float32),
            pltpu.VMEM((D_MODEL, TOKENS), bf16),
            pltpu.VMEM((16, TOKENS), jnp.float32),
            pltpu.VMEM((D_FF, TOKENS), bf16),
            pltpu.VMEM((D_MODEL, TOKENS), jnp.float32),
        ],
        compiler_params=pltpu.CompilerParams(
            dimension_semantics=("arbitrary",),
            vmem_limit_bytes=VMEM_LIMIT_BYTES),
        name="moba_pool_layer",
    )(x, jnp.concatenate([jnp.cos(ang), jnp.sin(ang)]), _bcast_cols(gains, TOKENS),
      w_pool_grp, *weights)
```

```python
import jax
import jax.numpy as jnp
from jax import lax
from jax.experimental import pallas as pl
from jax.experimental.pallas import tpu as pltpu

D_MODEL = 1024
HEADS = 8
HEAD_DIM = 64
ATTN_WIDTH = HEADS * HEAD_DIM
MOBA_BLOCK = 256
MOBA_TOPK = 3
ROPE_THETA = 500000.0
ROPE_HALF = HEAD_DIM // 8
POOL_WINDOWS = (2, 4, 8, 16)
POOL_GROUP_WIDTH = 128
POOL_WIDTH = len(POOL_WINDOWS) * POOL_GROUP_WIDTH
D_FF = 4 * D_MODEL
RMS_EPS = 1e-6

LANES = 128
TOKENS = MOBA_BLOCK
MASKED = -1e30
KQ_DIM = LANES
BIAS_ROWS = 8
V_ROWS = HEAD_DIM + 16
Q_SCALE = HEAD_DIM ** -0.5 * 1.4426950408889634
FF_CHUNK = 1024
OUT_CHUNK = 512
VMEM_LIMIT_BYTES = 61 * 1024 * 1024

_Q0, _K0, _V0, _U0 = 0, ATTN_WIDTH, 2 * ATTN_WIDTH, 3 * ATTN_WIDTH
_GA0 = 3 * ATTN_WIDTH + POOL_WIDTH
_GP0 = _GA0 + D_MODEL
IN_WIDTH = _GP0 + D_MODEL
_GAIN_ROWS = (0, D_MODEL, 2 * D_MODEL, 2 * D_MODEL + HEAD_DIM, 2 * D_MODEL + 2 * HEAD_DIM,
              2 * D_MODEL + 2 * HEAD_DIM + POOL_WIDTH)


def _bdot(a, b):
    return jnp.dot(a, b, preferred_element_type=jnp.float32)


def _wdot2(w_ref, rows, act, cols=slice(None)):
    mid = (rows.start + rows.stop) // 2
    return [_bdot(w_ref[rows.start:mid, cols], act), _bdot(w_ref[mid:rows.stop, cols], act)]


def _halves(rows):
    mid = (rows.start + rows.stop) // 2
    return [slice(rows.start, mid), slice(mid, rows.stop)]


def _sigmoid(x):
    return 0.5 * jnp.tanh(0.5 * x) + 0.5


def _rms_scale(t):
    return lax.rsqrt(jnp.mean(t * t, axis=0, keepdims=True) + RMS_EPS)


def _head_norm_rope(t, gain, cos, sin):
    y = t * _rms_scale(t) * gain
    x1 = y[0:ROPE_HALF, :]
    x2 = y[ROPE_HALF:2 * ROPE_HALF, :]
    return jnp.concatenate(
        [x1 * cos - x2 * sin, x2 * cos + x1 * sin, y[2 * ROPE_HALF:, :]], axis=0)


def _shift_tokens(tiles, shift):
    lane = lax.broadcasted_iota(jnp.int32, tiles[0].shape, 1)
    rolled = [pltpu.roll(t, shift, axis=1) for t in tiles]
    out = [rolled[0]]
    for c in range(1, len(tiles)):
        out.append(jnp.where(lane < shift, rolled[c - 1], rolled[c]))
    return out


def _load_weights(pairs, stage, sems):
    side = stage.shape[-1]
    slots_per_row = stage.shape[1]
    n_slots = stage.shape[0] * slots_per_row
    tiles = [(src, dst, k0, n0)
             for src, dst in pairs
             for k0 in range(0, dst.shape[1], side)
             for n0 in range(0, dst.shape[0], side)]

    def slot_of(t):
        slot = t % n_slots
        return slot, stage.at[slot // slots_per_row, slot % slots_per_row]

    def copy(t):
        src, _, k0, n0 = tiles[t]
        slot, buf = slot_of(t)
        return pltpu.make_async_copy(src.at[0, pl.ds(k0, side), pl.ds(n0, side)], buf, sems.at[slot])

    for t in range(min(n_slots, len(tiles))):
        copy(t).start()
    group = 4
    for t0 in range(0, len(tiles), group):
        ts = range(t0, min(t0 + group, len(tiles)))
        for t in ts:
            copy(t).wait()
        for t in ts:
            _, dst, k0, n0 = tiles[t]
            dst[n0:n0 + side, k0:k0 + side] = slot_of(t)[1][...].T.astype(dst.dtype)
        for t in ts:
            if t + n_slots < len(tiles):
                copy(t + n_slots).start()


def _layer_kernel(x_ref, rope_ref, gains_ref, wgrp_in,
                  win_hbm, wupa_hbm, wupp_hbm, wout_hbm, wff1_hbm, wff2_hbm,
                  o_ref,
                  win_ref, wupa_ref, wupp_ref, wout_ref, wff1_ref, wff2_ref, wgrp_ref, wsem,
                  kbuf, vbuf, kmbuf, uprev, qbuf, m_ref, alpha_ref, acc_ref, s_ref,
                  o_sc, op_sc, sga_sc, sgp_sc, mg_sc, x1_sc, h2_sc, r2_sc, f_sc):
    f32, bf16 = jnp.float32, jnp.bfloat16
    gmix_ref, gmlp_ref, qn_ref, kn_ref, ps_ref = (
        gains_ref.at[lo:hi] for lo, hi in zip(_GAIN_ROWS[:-1], _GAIN_ROWS[1:]))
    n = pl.program_id(0)
    n_tiles = pl.num_programs(0) - 1
    n_blocks = kbuf.shape[0]
    i = lax.rem(jnp.minimum(n, n_tiles - 1), n_blocks)

    @pl.when(n == 0)
    def _first_step():
        _load_weights([(win_hbm, win_ref), (wupa_hbm, wupa_ref), (wupp_hbm, wupp_ref),
                       (wout_hbm, wout_ref), (wff1_hbm, wff1_ref), (wff2_hbm, wff2_ref)],
                      s_ref, wsem)
        for g in range(len(POOL_WINDOWS)):
            wgrp_ref[g] = wgrp_in[0, g].T.astype(bf16)
        x1_sc[...] = jnp.zeros_like(x1_sc)
        h2_sc[...] = jnp.zeros_like(h2_sc)
        r2_sc[...] = jnp.zeros_like(r2_sc)

    @pl.when(i == 0)
    def _start_of_sequence():
        kmbuf[...] = jnp.zeros_like(kmbuf)
        uprev[...] = jnp.zeros_like(uprev)

    def score_head(j, slot, h, own=False):
        s = _bdot(kbuf[j, h], qbuf[h])
        if own:
            key_pos = lax.broadcasted_iota(jnp.int32, (MOBA_BLOCK, TOKENS), 0)
            qry_pos = lax.broadcasted_iota(jnp.int32, (MOBA_BLOCK, TOKENS), 1)
            s = jnp.where(key_pos <= qry_pos, s, MASKED)
        s_ref[slot, h] = s
        m_new = jnp.max(s, axis=0, keepdims=True)
        if not own:
            m_old = m_ref[1 - slot, h:h + 1, :]
            m_new = jnp.maximum(m_old, m_new)
            alpha_ref[slot, h:h + 1, :] = jnp.exp2(m_old - m_new)
        m_ref[slot, h:h + 1, :] = m_new

    def value_head(j, slot, h, own=False):
        p = jnp.exp2(s_ref[slot, h] - m_ref[slot, h:h + 1, :]).astype(bf16)
        pv = _bdot(vbuf[j, h], p)
        acc_ref[h] = pv if own else acc_ref[h] * alpha_ref[slot, h:h + 1, :] + pv

    def score_pass(j, slot, own=False):
        for h in range(HEADS):
            score_head(j, slot, h, own)

    def value_pass(j, slot, own=False):
        for h in range(HEADS):
            value_head(j, slot, h, own)

    all_rows = slice(0, D_MODEL)
    mixing = n < n_tiles

    def mlp_hidden():
        h2T = h2_sc[...]
        for c in range(D_FF // FF_CHUNK):
            chunk = slice(c * FF_CHUNK, (c + 1) * FF_CHUNK)
            for rows, z in zip(_halves(chunk), _wdot2(wff1_ref, chunk, h2T)):
                f = jnp.maximum(z, 0.0)
                f_sc[rows, :] = (f * f).astype(bf16)

    def mix_front():
        xT = x_ref[...].T
        hT = (xT * _rms_scale(xT) * gmix_ref[...]).astype(bf16)

        def proj2(lo, width):
            return _wdot2(win_ref, slice(lo, lo + width), hT)

        def head_rows(parts, h):
            per_part = HEADS // 2
            hh = h % per_part
            return parts[h // per_part][hh * HEAD_DIM:(hh + 1) * HEAD_DIM, :]

        cos = rope_ref[:ROPE_HALF, :]
        sin = rope_ref[ROPE_HALF:, :]

        k_parts = proj2(_K0, ATTN_WIDTH)
        q_parts = proj2(_Q0, ATTN_WIDTH)
        v_parts = proj2(_V0, ATTN_WIDTH)
        u_parts = proj2(_U0, POOL_WIDTH)
        for rows, ga, gp in zip(_halves(all_rows), proj2(_GA0, D_MODEL), proj2(_GP0, D_MODEL)):
            sga_sc[rows, :] = _sigmoid(ga)
            sgp_sc[rows, :] = _sigmoid(gp)
        pad_row = lax.broadcasted_iota(jnp.int32, (KQ_DIM - HEAD_DIM, TOKENS), 0)
        own_block = jnp.where(pad_row == i, 1.0, 0.0)
        ones_row = jnp.where(
            lax.broadcasted_iota(jnp.int32, (V_ROWS - HEAD_DIM, TOKENS), 0) == 0, 1.0, 0.0)
        for h in range(HEADS):
            k_h = _head_norm_rope(head_rows(k_parts, h), kn_ref[...], cos, sin)
            k_tok = jnp.concatenate([k_h, own_block], axis=0).T
            kbuf[i, h] = k_tok.astype(bf16)
            kmean = jnp.sum(k_tok, axis=0, keepdims=True) * (1.0 / MOBA_BLOCK)
            kmbuf[h, pl.ds(i, 1), :] = kmean
            vbuf[i, h, :HEAD_DIM, :] = head_rows(v_parts, h).astype(bf16)
            vbuf[i, h, HEAD_DIM:, :] = ones_row.astype(bf16)

        blk = lax.broadcasted_iota(jnp.int32, (BIAS_ROWS, TOKENS), 0)
        fully_past = blk < i
        for h in range(HEADS):
            q_h = _head_norm_rope(head_rows(q_parts, h), qn_ref[...], cos, sin)
            q_h = (q_h * Q_SCALE).astype(bf16)
            gate = _bdot(kmbuf[h, :, :HEAD_DIM].astype(bf16), q_h)[:BIAS_ROWS, :]
            g = jnp.where(fully_past, gate, -jnp.inf)
            rank = jnp.zeros((BIAS_ROWS, TOKENS), jnp.int32)
            for jp in range(n_blocks - 1):
                g_jp = g[jp:jp + 1, :]
                ahead = jnp.logical_or(g_jp > g, jnp.logical_and(g_jp == g, blk > jp))
                rank = rank + ahead.astype(jnp.int32)
            chosen = jnp.logical_and(rank < MOBA_TOPK, fully_past)
            bias = jnp.where(jnp.logical_or(chosen, blk == i), 0.0, MASKED)
            zeros = jnp.zeros((KQ_DIM - HEAD_DIM - BIAS_ROWS, TOKENS), f32)
            qbuf[h, :HEAD_DIM, :] = q_h
            qbuf[h, HEAD_DIM:, :] = jnp.concatenate([bias, zeros], axis=0).astype(bf16)

        tok = i * TOKENS + lax.broadcasted_iota(jnp.int32, (1, TOKENS), 1)
        groups_per_part = len(POOL_WINDOWS) // 2
        for gi, w in enumerate(POOL_WINDOWS):
            rows = slice(gi * POOL_GROUP_WIDTH, (gi + 1) * POOL_GROUP_WIDTH)
            gg = gi % groups_per_part
            u_g = u_parts[gi // groups_per_part][gg * POOL_GROUP_WIDTH:(gg + 1) * POOL_GROUP_WIDTH, :]
            tiles = [uprev[rows, :]] + [u_g[:, c * LANES:(c + 1) * LANES]
                                        for c in range(TOKENS // LANES)]
            uprev[rows, :] = tiles[-1]
            shift = 1
            while shift < w:
                shifted = _shift_tokens(tiles, shift)
                tiles = [a + b for a, b in zip(tiles, shifted)]
                shift *= 2
            win_sum = jnp.concatenate(tiles[1:], axis=1)
            inv_cnt = 1.0 / jnp.minimum(tok + 1, w).astype(f32)
            pooled = win_sum * inv_cnt - u_g
            y = _bdot(wgrp_ref[gi], pooled.astype(bf16)) * ps_ref[rows, :]
            op_sc[rows, :] = y.astype(bf16)

        score_pass(i, 1, own=True)

    @pl.when(mixing)
    def _mix_front_with_previous_mlp():
        mlp_hidden()
        mix_front()

    @pl.when(jnp.logical_not(mixing))
    def _only_previous_mlp():
        mlp_hidden()

    n_past = i

    def gate_pooling_branch():
        for rows, upp in zip(_halves(all_rows), _wdot2(wupp_ref, all_rows, op_sc[...])):
            sgp_sc[rows, :] = sgp_sc[rows, :] * upp

    def visit(blocks):
        pending = (i, 1, True)
        for k, t in enumerate(blocks):
            for h in range(HEADS):
                score_head(t, k % 2, h)
                value_head(pending[0], pending[1], h, pending[2])
            pending = (t, k % 2, False)
        gate_pooling_branch()
        value_pass(*pending)

    for count in range(n_blocks):
        pl.when(jnp.logical_and(mixing, n_past == count))(
            lambda count=count: visit(list(range(count))))

    n_out_chunks = D_MODEL // OUT_CHUNK

    def mlp_out(c):
        chunk = slice(c * OUT_CHUNK, (c + 1) * OUT_CHUNK)
        r2_prev = r2_sc[0:1, :]
        for rows, y in zip(_halves(chunk), _wdot2(wff2_ref, chunk, f_sc[...])):
            o_ref[:, rows] = (x1_sc[rows, :] + y * r2_prev).T

    @pl.when(mixing)
    def _mix_back_with_previous_mlp():
        for c in range(n_out_chunks // 2):
            mlp_out(c)

        for h in range(HEADS):
            o_h = acc_ref[h, :HEAD_DIM, :] * (1.0 / acc_ref[h, HEAD_DIM:HEAD_DIM + 1, :])
            o_sc[h * HEAD_DIM:(h + 1) * HEAD_DIM, :] = o_h.astype(bf16)

        upa_parts = _wdot2(wupa_ref, all_rows, o_sc[...])
        for c in range(n_out_chunks // 2, n_out_chunks):
            mlp_out(c)
        for rows, upa in zip(_halves(all_rows), upa_parts):
            mg_sc[rows, :] = (sga_sc[rows, :] * upa + sgp_sc[rows, :]).astype(bf16)
        sumsq = jnp.zeros((1, TOKENS), f32)
        for rows, d in zip(_halves(all_rows), _wdot2(wout_ref, all_rows, mg_sc[...])):
            x1 = x_ref[:, rows].T + d
            x1_sc[rows, :] = x1
            h2_sc[rows, :] = (x1 * gmlp_ref[rows, :]).astype(bf16)
            sumsq = sumsq + jnp.sum(x1 * x1, axis=0, keepdims=True)
        r = lax.rsqrt(sumsq * (1.0 / D_MODEL) + RMS_EPS)
        r2_sc[...] = jnp.broadcast_to(r * r, r2_sc.shape)

    @pl.when(jnp.logical_not(mixing))
    def _only_previous_mlp_out():
        for c in range(n_out_chunks):
            mlp_out(c)


def _bcast_cols(v, n):
    return jnp.broadcast_to(v.astype(jnp.float32)[:, None], (v.shape[0], n))


@jax.jit
def kernel(x, norm_mix, w_in, q_norm, k_norm, w_pool_grp, pool_scale, w_up_attn, w_up_pool,
           w_out, norm_mlp, w_ff1, w_ff2):
    batch, seq, d_model = x.shape
    assert d_model == D_MODEL and seq % TOKENS == 0 and w_in.shape[0] == 1
    n_blocks = seq // TOKENS
    assert n_blocks <= BIAS_ROWS
    bf16 = jnp.bfloat16

    half = jnp.arange(ROPE_HALF, dtype=jnp.float32)
    inv_freq = ROPE_THETA ** (-half / ROPE_HALF)
    ang = inv_freq[:, None] * jnp.arange(seq).astype(jnp.float32)[None, :]

    n_tiles = batch * n_blocks

    def tile(n):
        t = jnp.clip(n, 0, n_tiles - 1)
        return t // n_blocks, t % n_blocks

    whole = pl.BlockSpec(memory_space=pltpu.VMEM)
    in_hbm = pl.BlockSpec(memory_space=pl.ANY)
    weights = (w_in, w_up_attn, w_up_pool, w_out, w_ff1, w_ff2)
    x_spec = pl.BlockSpec((None, TOKENS, D_MODEL), lambda n: (*tile(n), 0))
    out_spec = pl.BlockSpec((None, TOKENS, D_MODEL), lambda n: (*tile(n - 1), 0))
    rope_spec = pl.BlockSpec((2 * ROPE_HALF, TOKENS), lambda n: (0, tile(n)[1]))
    gains = jnp.concatenate([norm_mix[0], norm_mlp[0], q_norm[0], k_norm[0], pool_scale[0]])

    return pl.pallas_call(
        _layer_kernel,
        grid=(n_tiles + 1,),
        in_specs=[x_spec, rope_spec, whole, whole] + [in_hbm] * len(weights),
        out_specs=out_spec,
        out_shape=jax.ShapeDtypeStruct(x.shape, x.dtype),
        scratch_shapes=[
            *[pltpu.VMEM((w.shape[2], w.shape[1]), bf16) for w in weights],
            pltpu.VMEM(w_pool_grp.shape[1:], bf16),
            pltpu.SemaphoreType.DMA((2 * HEADS,)),
            pltpu.VMEM((n_blocks, HEADS, MOBA_BLOCK, KQ_DIM), bf16),
            pltpu.VMEM((n_blocks, HEADS, V_ROWS, MOBA_BLOCK), bf16),
            pltpu.VMEM((HEADS, 2 * BIAS_ROWS, KQ_DIM), jnp.float32),
            pltpu.VMEM((POOL_WIDTH, LANES), jnp.float32),
            pltpu.VMEM((HEADS, KQ_DIM, TOKENS), bf16),
            pltpu.VMEM((2, HEADS, TOKENS), jnp.float32),
            pltpu.VMEM((2, HEADS, TOKENS), jnp.float32),
            pltpu.VMEM((HEADS, V_ROWS, TOKENS), jnp.float32),
            pltpu.VMEM((2, HEADS, MOBA_BLOCK, TOKENS), jnp.float32),
            pltpu.VMEM((ATTN_WIDTH, TOKENS), bf16),
            pltpu.VMEM((POOL_WIDTH, TOKENS), bf16),
            pltpu.VMEM((D_MODEL, TOKENS), jnp.float32),
            pltpu.VMEM((D_MODEL, TOKENS), jnp.float32),
            pltpu.VMEM((D_MODEL, TOKENS), bf16),
            pltpu.VMEM((D_MODEL, TOKENS), jnp.float32),
            pltpu.VMEM((D_MODEL, TOKENS), bf16),
            pltpu.VMEM((16, TOKENS), jnp.float32),
            pltpu.VMEM((D_FF, TOKENS), bf16),
        ],
        compiler_params=pltpu.CompilerParams(
            dimension_semantics=("arbitrary",),
            vmem_limit_bytes=VMEM_LIMIT_BYTES),
        name="moba_pool_layer",
    )(x, jnp.concatenate([jnp.cos(ang), jnp.sin(ang)]), _bcast_cols(gains, TOKENS),
      w_pool_grp, *weights)
```
